```python
import math
import numpy as np
import jax
import jax.numpy as jnp
from jax import lax

D_MODEL = 2048
BATCH = 8
SEQ = 4096
DEPTH = 2

PLE_DIM = 256
N_EVEN = (DEPTH + 1) // 2
N_ODD = DEPTH // 2
DN_ALPHA = (2.0 * DEPTH) ** 0.25
DN_BETA = (8.0 * DEPTH) ** -0.25
LN_EPS = 1e-5
RMS_EPS = 1e-6
A_HEADS = 16
A_HEAD_DIM = 64
A_WIDTH = A_HEADS * A_HEAD_DIM
A_W_LORA = 96
A_A_LORA = 96
A_G_LORA = 256
A_GN_EPS = 64e-5
A_SPLITS = (A_WIDTH, A_WIDTH, A_WIDTH, A_W_LORA, A_A_LORA, A_G_LORA)
A_COLS = sum(A_SPLITS)
B_HEADS = 8
B_HEAD_DIM = 128
B_WIDTH = B_HEADS * B_HEAD_DIM
B_CONV = 4
B_CHUNK = 64
B_SPLITS = (B_WIDTH, B_WIDTH, B_WIDTH, B_WIDTH, B_HEADS, B_HEADS)
C_HEADS = 32
C_HEAD_DIM = 64
C_WIDTH = C_HEADS * C_HEAD_DIM
C_GROUPS = 4
C_STATE = 128
C_CONV = 4
C_CHUNK = 128
C_XBC = C_WIDTH + 2 * C_GROUPS * C_STATE
D_WIDTH = 2048
D_BLOCKS = 16
D_BLOCK_DIM = D_WIDTH // D_BLOCKS
D_CONV = 4
LRU_C = 8.0
D_FF = 5632
FFN_CONV = 3
EVEN_SPLITS = A_SPLITS + B_SPLITS
EVEN_IN = sum(EVEN_SPLITS)
EVEN_OUT = A_WIDTH + B_WIDTH
ODD_SPLITS = (C_WIDTH, C_XBC, C_HEADS, D_WIDTH, D_WIDTH)
ODD_IN = sum(ODD_SPLITS)
ODD_OUT = C_WIDTH + D_WIDTH

kernel_name = "hybrid_rwkv7_gdn_mamba2_rglru_trunk"


def _split(h, sizes):
    idx = [int(s) for s in np.cumsum(sizes)[:-1]]
    return jnp.split(h, idx, axis=-1)


def layer_norm(x, g, b, eps=LN_EPS):
    xf = x.astype(jnp.float32)
    mu = jnp.mean(xf, -1, keepdims=True)
    var = jnp.mean(jnp.square(xf - mu), -1, keepdims=True)
    return ((xf - mu) * lax.rsqrt(var + eps)).astype(x.dtype) * g + b


def rms_norm(x, g, eps=RMS_EPS):
    xf = x.astype(jnp.float32)
    y = xf * lax.rsqrt(jnp.mean(xf * xf, -1, keepdims=True) + eps)
    return y.astype(x.dtype) * g


def l2_normalize(x, eps=1e-6):
    xf = x.astype(jnp.float32)
    return (xf * lax.rsqrt(jnp.sum(xf * xf, -1, keepdims=True) + eps)).astype(x.dtype)


def causal_dwconv(x, w):
    K, C = w.shape
    return lax.conv_general_dilated(
        x, w[:, None, :].astype(x.dtype), window_strides=(1,), padding=[(K - 1, 0)],
        dimension_numbers=('NWC', 'WIO', 'NWC'), feature_group_count=C)


def token_shift_lerp(h, mu):
    prev = jnp.pad(h, ((0, 0), (1, 0), (0, 0)))[:, :-1]
    return h + (prev - h) * mu


def segsum(a):
    L = a.shape[-1]
    cs = jnp.cumsum(a, -1)
    return jnp.where(jnp.tril(jnp.ones((L, L), bool)), cs[..., :, None] - cs[..., None, :], -jnp.inf)


def linear_recurrence(a, u):
    def combine(left, right):
        a_l, u_l = left
        a_r, u_r = right
        return a_l * a_r, a_r * u_l + u_r
    _, h = lax.associative_scan(combine, (a, u), axis=1)
    return h


def rwkv7_scan(r, w, k, v, a, b):
    Bsz, T, H, N = r.shape
    tm = lambda t: jnp.moveaxis(t.astype(jnp.float32), 1, 0)

    def step(S, inp):
        r_t, w_t, k_t, v_t, a_t, b_t = inp
        sa = jnp.einsum('bhvk,bhk->bhv', S, a_t)
        S = S * w_t[:, :, None, :] + sa[..., None] * b_t[:, :, None, :] + v_t[..., None] * k_t[:, :, None, :]
        return S, jnp.einsum('bhvk,bhk->bhv', S, r_t)

    S0 = jnp.zeros((Bsz, H, N, N), jnp.float32)
    _, y = lax.scan(step, S0, (tm(r), tm(w), tm(k), tm(v), tm(a), tm(b)))
    return jnp.moveaxis(y, 0, 1).astype(r.dtype)


def rwkv7_mix(cols, mu, w0, w2, a0, a2, g2, k_k, k_a, r_k, gn_g, gn_b):
    Bsz, T, _ = cols.shape
    r, k, v, w_lo, a_lo, g_lo = _split(token_shift_lerp(cols, mu), A_SPLITS)
    w_raw = (w0 + jnp.tanh(w_lo) @ w2).astype(jnp.float32)
    decay = jnp.exp(-jnp.exp(-jax.nn.softplus(-w_raw) - 0.5))
    a = jax.nn.sigmoid(a0 + a_lo @ a2)
    g = jax.nn.sigmoid(g_lo) @ g2
    heads = lambda t: t.reshape(Bsz, T, A_HEADS, A_HEAD_DIM)
    kk = l2_normalize(heads(k * k_k))
    k = k * (1.0 + (a - 1.0) * k_a)
    rh, kh, vh = heads(r), heads(k), heads(v)
    out = rwkv7_scan(rh, heads(decay), kh, vh, -kk, kk * heads(a))
    out = layer_norm(out, gn_g, gn_b, eps=A_GN_EPS)
    bonus = jnp.sum(rh * kh * r_k, -1, keepdims=True) * vh
    return (out + bonus).reshape(Bsz, T, A_WIDTH) * g


def gated_delta_chunked(q, k, v, log_g, beta):
    Bsz, T, H, dk = q.shape
    dv = v.shape[-1]
    C = B_CHUNK
    n = T // C

    def chunks(t):
        t = t.astype(jnp.float32).reshape((Bsz, n, C, H) + t.shape[3:])
        return jnp.moveaxis(t, 3, 1)

    q = chunks(q) * dk ** -0.5
    k, v, beta = chunks(k), chunks(v), chunks(beta)
    gc = jnp.cumsum(chunks(log_g), -1)
    causal = jnp.tril(jnp.ones((C, C), bool))
    strict = jnp.tril(jnp.ones((C, C), bool), -1)
    decay = jnp.exp(jnp.where(causal, gc[..., :, None] - gc[..., None, :], -jnp.inf))
    k_beta = k * beta[..., None]
    M = jnp.where(strict, jnp.einsum('bhncd,bhnsd->bhncs', k_beta, k) * decay, 0.0)
    rhs = jnp.concatenate([v * beta[..., None], k_beta * jnp.exp(gc)[..., None]], -1)
    sol = lax.linalg.triangular_solve(M + jnp.eye(C, dtype=M.dtype), rhs,
                                      left_side=True, lower=True, unit_diagonal=True)
    u, w = sol[..., :dv], sol[..., dv:]
    attn = jnp.einsum('bhncd,bhnsd->bhncs', q, k) * decay
    q_dec = q * jnp.exp(gc)[..., None]
    k_dec = k * jnp.exp(gc[..., -1:] - gc)[..., None]
    g_last = jnp.exp(gc[..., -1])

    def step(S, inp):
        u_i, w_i, a_i, qd_i, kd_i, gl_i = inp
        v_new = u_i - jnp.einsum('bhcd,bhde->bhce', w_i, S)
        o = jnp.einsum('bhcd,bhde->bhce', qd_i, S) + jnp.einsum('bhcs,bhse->bhce', a_i, v_new)
        S = S * gl_i[..., None, None] + jnp.einsum('bhcd,bhce->bhde', kd_i, v_new)
        return S, o

    xs = tuple(jnp.moveaxis(t, 2, 0) for t in (u, w, attn, q_dec, k_dec, g_last))
    _, o = lax.scan(step, jnp.zeros((Bsz, H, dk, dv), jnp.float32), xs)
    o = jnp.moveaxis(jnp.moveaxis(o, 0, 2), 1, 3).reshape(Bsz, T, H, dv)
    return o


def gdn_mix(q, k, v, z, beta_raw, alpha_raw, conv_w, A_log, dt_bias, norm_g):
    Bsz, T, _ = q.shape
    qkv = jax.nn.silu(causal_dwconv(jnp.concatenate([q, k, v], -1), conv_w))
    q, k, v = jnp.split(qkv, 3, axis=-1)
    heads = lambda t: t.reshape(Bsz, T, B_HEADS, B_HEAD_DIM)
    q, k, v = l2_normalize(heads(q)), l2_normalize(heads(k)), heads(v)
    beta = jax.nn.sigmoid(beta_raw)
    log_g = -jnp.exp(A_log) * jax.nn.softplus(alpha_raw + dt_bias)
    o = gated_delta_chunked(q, k, v, log_g, beta).astype(q.dtype)
    o = rms_norm(o, norm_g) * jax.nn.silu(heads(z))
    return o.reshape(Bsz, T, B_WIDTH)


def ssd_chunked(X, A, Bm, Cm):
    Bsz, T, H, P = X.shape
    G, N = Bm.shape[2], Bm.shape[3]
    Hg = H // G
    L = C_CHUNK
    c = T // L
    X = X.astype(jnp.float32).reshape(Bsz, c, L, G, Hg, P)
    A = jnp.moveaxis(A.astype(jnp.float32).reshape(Bsz, c, L, G, Hg), 2, -1)
    Bm = Bm.astype(jnp.float32).reshape(Bsz, c, L, G, N)
    Cm = Cm.astype(jnp.float32).reshape(Bsz, c, L, G, N)
    A_cum = jnp.cumsum(A, -1)
    CB = jnp.einsum('bclgn,bcsgn->bcgls', Cm, Bm)
    Wd = CB[:, :, :, None] * jnp.exp(segsum(A))
    Y_diag = jnp.einsum('bcghls,bcsghp->bclghp', Wd, X)
    decay_states = jnp.moveaxis(jnp.exp(A_cum[..., -1:] - A_cum), -1, 2)
    states = jnp.einsum('bclgn,bclghp->bcghpn', Bm, X * decay_states[..., None])
    states = jnp.concatenate([jnp.zeros_like(states[:, :1]), states], 1)
    chunk_A = jnp.pad(jnp.moveaxis(A_cum[..., -1], 1, -1), ((0, 0), (0, 0), (0, 0), (1, 0)))
    states = jnp.einsum('bghzc,bcghpn->bzghpn', jnp.exp(segsum(chunk_A)), states)[:, :-1]
    Y_off = jnp.einsum('bclgn,bcghpn->bclghp', Cm, states) * jnp.moveaxis(jnp.exp(A_cum), -1, 2)[..., None]
    return (Y_diag + Y_off).reshape(Bsz, T, H, P)


def mamba2_mix(z, xbc, dt_raw, conv_w, conv_b, dt_bias, A_log, D_skip, norm_g):
    Bsz, T, _ = z.shape
    xbc = jax.nn.silu(causal_dwconv(xbc, conv_w) + conv_b)
    xs, Bm, Cm = _split(xbc, (C_WIDTH, C_GROUPS * C_STATE, C_GROUPS * C_STATE))
    xs = xs.reshape(Bsz, T, C_HEADS, C_HEAD_DIM)
    Bm = Bm.reshape(Bsz, T, C_GROUPS, C_STATE)
    Cm = Cm.reshape(Bsz, T, C_GROUPS, C_STATE)
    dt = jax.nn.softplus(dt_raw + dt_bias)
    y = ssd_chunked(xs * dt[..., None], dt * (-jnp.exp(A_log)), Bm, Cm).astype(xs.dtype)
    y = (y + xs * D_skip[:, None]).reshape(Bsz, T, C_WIDTH) * jax.nn.silu(z)
    y = rms_norm(y.reshape(Bsz, T, C_GROUPS, C_WIDTH // C_GROUPS), norm_g.reshape(C_GROUPS, -1))
    return y.reshape(Bsz, T, C_WIDTH)


def rglru_mix(y_br, x_br, conv_w, conv_b, wa, ba, wx, bx, lam):
    Bsz, T, _ = x_br.shape
    x = causal_dwconv(x_br, conv_w) + conv_b
    xb = x.reshape(Bsz, T, D_BLOCKS, D_BLOCK_DIM)
    r = jax.nn.sigmoid(jnp.einsum('btnd,nde->btne', xb, wa).reshape(Bsz, T, D_WIDTH) + ba)
    i = jax.nn.sigmoid(jnp.einsum('btnd,nde->btne', xb, wx).reshape(Bsz, T, D_WIDTH) + bx)
    log_a = LRU_C * r.astype(jnp.float32) * jax.nn.log_sigmoid(lam.astype(jnp.float32))
    u = jnp.sqrt(-jnp.expm1(2.0 * log_a)) * (i * x).astype(jnp.float32)
    h = linear_recurrence(jnp.exp(log_a), u)
    return h.astype(x.dtype) * jax.nn.gelu(y_br)


def conv_ffn(h, w_up, conv_w, conv_b, w_down):
    u = causal_dwconv(h @ w_up, conv_w) + conv_b
    gate, val = jnp.split(u, 2, axis=-1)
    return (jax.nn.silu(gate) * val) @ w_down


def even_mixer(x, w_in, w_out, a_mu, a_w0, a_w2, a_a0, a_a2, a_g2, a_k_k, a_k_a, a_r_k, a_gn_g, a_gn_b,
               b_conv_w, b_A_log, b_dt_bias, b_norm_g):
    hcols = x @ w_in
    a_cols, b_cols = hcols[..., :A_COLS], hcols[..., A_COLS:]
    ya = rwkv7_mix(a_cols, a_mu, a_w0, a_w2, a_a0, a_a2, a_g2, a_k_k, a_k_a, a_r_k, a_gn_g, a_gn_b)
    q, k, v, z, beta_raw, alpha_raw = _split(b_cols, B_SPLITS)
    yb = gdn_mix(q, k, v, z, beta_raw, alpha_raw, b_conv_w, b_A_log, b_dt_bias, b_norm_g)
    return jnp.concatenate([ya, yb], -1) @ w_out


def odd_mixer(x, w_in, w_out, m_conv_w, m_conv_b, m_dt_bias, m_A_log, m_D, m_norm_g,
              l_conv_w, l_conv_b, l_wa, l_ba, l_wx, l_bx, l_lam):
    z, xbc, dt_raw, y_br, x_br = _split(x @ w_in, ODD_SPLITS)
    yc = mamba2_mix(z, xbc, dt_raw, m_conv_w, m_conv_b, m_dt_bias, m_A_log, m_D, m_norm_g)
    yd = rglru_mix(y_br, x_br, l_conv_w, l_conv_b, l_wa, l_ba, l_wx, l_bx, l_lam)
    return jnp.concatenate([yc, yd], -1) @ w_out


def _fwd_setup_inputs(seed: int = 0) -> dict:
    key = jax.random.key(seed)
    ks = iter(jax.random.split(key, 64))
    nrm = lambda shape, scale: scale * jax.random.normal(next(ks), shape, jnp.float32)
    unif = lambda shape, lo, hi: jax.random.uniform(next(ks), shape, jnp.float32, lo, hi)

    def dt_bias(shape):
        dt = jnp.exp(unif(shape, math.log(1e-3), math.log(1e-1)))
        return dt + jnp.log(-jnp.expm1(-dt))

    Ld, E, O, D = DEPTH, N_EVEN, N_ODD, D_MODEL
    chan = jnp.arange(A_WIDTH, dtype=jnp.float32) / (A_WIDTH - 1)
    lam_s = unif((O, D_WIDTH), 0.9, 0.999) ** (1.0 / LRU_C)
    return {
        "x": nrm((BATCH, SEQ, D), 1.0),
        "p": nrm((DEPTH, BATCH, SEQ, PLE_DIM), 1.0),
        "ln1_g": 1.0 + nrm((Ld, D), 0.02), "ln1_b": nrm((Ld, D), 0.02),
        "ln2_g": 1.0 + nrm((Ld, D), 0.02), "ln2_b": nrm((Ld, D), 0.02),
        "ffn_up": nrm((Ld, D, 2 * D_FF), D ** -0.5),
        "ffn_conv_w": nrm((Ld, FFN_CONV, 2 * D_FF), FFN_CONV ** -0.5),
        "ffn_conv_b": nrm((Ld, 2 * D_FF), 0.02),
        "ffn_down": nrm((Ld, D_FF, D), DN_BETA * D_FF ** -0.5),
        "ple_proj": nrm((Ld, PLE_DIM, D), PLE_DIM ** -0.5),
        "ple_norm_g": 1.0 + nrm((Ld, D), 0.02),
        "ple_gate_w": nrm((Ld, D, D), D ** -0.5),
        "ple_gate_b": nrm((Ld, D), 0.02),
        "even_w_in": nrm((E, D, EVEN_IN), D ** -0.5),
        "even_w_out": nrm((E, EVEN_OUT, D), DN_BETA * EVEN_OUT ** -0.5),
        "rwkv_mu": unif((E, A_COLS), 0.0, 1.0),
        "rwkv_w0": (-6.5 + 5.0 * chan ** 0.85)[None] + nrm((E, A_WIDTH), 0.1),
        "rwkv_w2": nrm((E, A_W_LORA, A_WIDTH), A_W_LORA ** -0.5),
        "rwkv_a0": nrm((E, A_WIDTH), 0.1),
        "rwkv_a2": nrm((E, A_A_LORA, A_WIDTH), A_A_LORA ** -0.5),
        "rwkv_g2": nrm((E, A_G_LORA, A_WIDTH), A_G_LORA ** -0.5),
        "rwkv_k_k": 0.85 + nrm((E, A_WIDTH), 0.02),
        "rwkv_k_a": 1.0 + nrm((E, A_WIDTH), 0.02),
        "rwkv_r_k": nrm((E, A_HEADS, A_HEAD_DIM), 0.1),
        "rwkv_gn_g": 1.0 + nrm((E, A_HEADS, A_HEAD_DIM), 0.02),
        "rwkv_gn_b": nrm((E, A_HEADS, A_HEAD_DIM), 0.02),
        "gdn_conv_w": nrm((E, B_CONV, 3 * B_WIDTH), B_CONV ** -0.5),
        "gdn_A_log": jnp.log(unif((E, B_HEADS), 1.0, 16.0)),
        "gdn_dt_bias": dt_bias((E, B_HEADS)),
        "gdn_norm_g": 1.0 + nrm((E, B_HEAD_DIM), 0.02),
        "odd_w_in": nrm((O, D, ODD_IN), D ** -0.5),
        "odd_w_out": nrm((O, ODD_OUT, D), DN_BETA * ODD_OUT ** -0.5),
        "mamba_conv_w": nrm((O, C_CONV, C_XBC), C_CONV ** -0.5),
        "mamba_conv_b": nrm((O, C_XBC), 0.02),
        "mamba_dt_bias": dt_bias((O, C_HEADS)),
        "mamba_A_log": jnp.log(unif((O, C_HEADS), 1.0, 16.0)),
        "mamba_D": 1.0 + nrm((O, C_HEADS), 0.02),
        "mamba_norm_g": 1.0 + nrm((O, C_WIDTH), 0.02),
        "lru_conv_w": nrm((O, D_CONV, D_WIDTH), D_CONV ** -0.5),
        "lru_conv_b": nrm((O, D_WIDTH), 0.02),
        "lru_wa": nrm((O, D_BLOCKS, D_BLOCK_DIM, D_BLOCK_DIM), D_BLOCK_DIM ** -0.5),
        "lru_ba": nrm((O, D_WIDTH), 0.02),
        "lru_wx": nrm((O, D_BLOCKS, D_BLOCK_DIM, D_BLOCK_DIM), D_BLOCK_DIM ** -0.5),
        "lru_bx": nrm((O, D_WIDTH), 0.02),
        "lru_lambda": jnp.log(lam_s) - jnp.log1p(-lam_s),
    }


def _fwd_reference(x, p, ln1_g, ln1_b, ln2_g, ln2_b, ffn_up, ffn_conv_w, ffn_conv_b, ffn_down,
              ple_proj, ple_norm_g, ple_gate_w, ple_gate_b,
              even_w_in, even_w_out, rwkv_mu, rwkv_w0, rwkv_w2, rwkv_a0, rwkv_a2, rwkv_g2,
              rwkv_k_k, rwkv_k_a, rwkv_r_k, rwkv_gn_g, rwkv_gn_b,
              gdn_conv_w, gdn_A_log, gdn_dt_bias, gdn_norm_g,
              odd_w_in, odd_w_out, mamba_conv_w, mamba_conv_b, mamba_dt_bias, mamba_A_log, mamba_D,
              mamba_norm_g, lru_conv_w, lru_conv_b, lru_wa, lru_ba, lru_wx, lru_bx, lru_lambda):
    for i in range(DEPTH):
        j = i // 2
        if i % 2 == 0:
            y = even_mixer(x, even_w_in[j], even_w_out[j], rwkv_mu[j], rwkv_w0[j], rwkv_w2[j], rwkv_a0[j],
                           rwkv_a2[j], rwkv_g2[j], rwkv_k_k[j], rwkv_k_a[j], rwkv_r_k[j], rwkv_gn_g[j],
                           rwkv_gn_b[j], gdn_conv_w[j], gdn_A_log[j], gdn_dt_bias[j], gdn_norm_g[j])
        else:
            y = odd_mixer(x, odd_w_in[j], odd_w_out[j], mamba_conv_w[j], mamba_conv_b[j], mamba_dt_bias[j],
                          mamba_A_log[j], mamba_D[j], mamba_norm_g[j], lru_conv_w[j], lru_conv_b[j],
                          lru_wa[j], lru_ba[j], lru_wx[j], lru_bx[j], lru_lambda[j])
        h = layer_norm(DN_ALPHA * x + y, ln1_g[i], ln1_b[i])
        h = layer_norm(DN_ALPHA * h + conv_ffn(h, ffn_up[i], ffn_conv_w[i], ffn_conv_b[i], ffn_down[i]),
                       ln2_g[i], ln2_b[i])
        e = rms_norm(p[i] @ ple_proj[i], ple_norm_g[i])
        x = h + jax.nn.sigmoid(h @ ple_gate_w[i] + ple_gate_b[i]) * e
    return x


import jax as _jax
import jax.numpy as _jnp

TWIN_FORMAT = 'train_step'
FWD_PARAMS = ['x', 'p', 'ln1_g', 'ln1_b', 'ln2_g', 'ln2_b', 'ffn_up', 'ffn_conv_w', 'ffn_conv_b', 'ffn_down', 'ple_proj', 'ple_norm_g', 'ple_gate_w', 'ple_gate_b', 'even_w_in', 'even_w_out', 'rwkv_mu', 'rwkv_w0', 'rwkv_w2', 'rwkv_a0', 'rwkv_a2', 'rwkv_g2', 'rwkv_k_k', 'rwkv_k_a', 'rwkv_r_k', 'rwkv_gn_g', 'rwkv_gn_b', 'gdn_conv_w', 'gdn_A_log', 'gdn_dt_bias', 'gdn_norm_g', 'odd_w_in', 'odd_w_out', 'mamba_conv_w', 'mamba_conv_b', 'mamba_dt_bias', 'mamba_A_log', 'mamba_D', 'mamba_norm_g', 'lru_conv_w', 'lru_conv_b', 'lru_wa', 'lru_ba', 'lru_wx', 'lru_bx', 'lru_lambda']
TWIN_WEIGHTS = ['ln1_g', 'ln1_b', 'ln2_g', 'ln2_b', 'ffn_up', 'ffn_conv_w', 'ffn_conv_b', 'ffn_down', 'ple_proj', 'ple_norm_g', 'ple_gate_w', 'ple_gate_b', 'even_w_in', 'even_w_out', 'rwkv_mu', 'rwkv_w0', 'rwkv_w2', 'rwkv_a0', 'rwkv_a2', 'rwkv_g2', 'rwkv_k_k', 'rwkv_k_a', 'rwkv_r_k', 'rwkv_gn_g', 'rwkv_gn_b', 'gdn_conv_w', 'gdn_A_log', 'gdn_dt_bias', 'gdn_norm_g', 'odd_w_in', 'odd_w_out', 'mamba_conv_w', 'mamba_conv_b', 'mamba_dt_bias', 'mamba_A_log', 'mamba_D', 'mamba_norm_g', 'lru_conv_w', 'lru_conv_b', 'lru_wa', 'lru_ba', 'lru_wx', 'lru_bx', 'lru_lambda']
TWIN_DIFF_INPUT = 'x'
TWIN_INPUTS = ['x', 'p', 'ln1_g', 'ln1_b', 'ln2_g', 'ln2_b', 'ffn_up', 'ffn_conv_w', 'ffn_conv_b', 'ffn_down', 'ple_proj', 'ple_norm_g', 'ple_gate_w', 'ple_gate_b', 'even_w_in', 'even_w_out', 'rwkv_mu', 'rwkv_w0', 'rwkv_w2', 'rwkv_a0', 'rwkv_a2', 'rwkv_g2', 'rwkv_k_k', 'rwkv_k_a', 'rwkv_r_k', 'rwkv_gn_g', 'rwkv_gn_b', 'gdn_conv_w', 'gdn_A_log', 'gdn_dt_bias', 'gdn_norm_g', 'odd_w_in', 'odd_w_out', 'mamba_conv_w', 'mamba_conv_b', 'mamba_dt_bias', 'mamba_A_log', 'mamba_D', 'mamba_norm_g', 'lru_conv_w', 'lru_conv_b', 'lru_wa', 'lru_ba', 'lru_wx', 'lru_bx', 'lru_lambda', 'loss_target', 'm_ln1_g', 'm_ln1_b', 'm_ln2_g', 'm_ln2_b', 'm_ffn_up', 'm_ffn_conv_w', 'm_ffn_conv_b', 'm_ffn_down', 'm_ple_proj', 'm_ple_norm_g', 'm_ple_gate_w', 'm_ple_gate_b', 'm_even_w_in', 'm_even_w_out', 'm_rwkv_mu', 'm_rwkv_w0', 'm_rwkv_w2', 'm_rwkv_a0', 'm_rwkv_a2', 'm_rwkv_g2', 'm_rwkv_k_k', 'm_rwkv_k_a', 'm_rwkv_r_k', 'm_rwkv_gn_g', 'm_rwkv_gn_b', 'm_gdn_conv_w', 'm_gdn_A_log', 'm_gdn_dt_bias', 'm_gdn_norm_g', 'm_odd_w_in', 'm_odd_w_out', 'm_mamba_conv_w', 'm_mamba_conv_b', 'm_mamba_dt_bias', 'm_mamba_A_log', 'm_mamba_D', 'm_mamba_norm_g', 'm_lru_conv_w', 'm_lru_conv_b', 'm_lru_wa', 'm_lru_ba', 'm_lru_wx', 'm_lru_bx', 'm_lru_lambda', 'v_ln1_g', 'v_ln1_b', 'v_ln2_g', 'v_ln2_b', 'v_ffn_up', 'v_ffn_conv_w', 'v_ffn_conv_b', 'v_ffn_down', 'v_ple_proj', 'v_ple_norm_g', 'v_ple_gate_w', 'v_ple_gate_b', 'v_even_w_in', 'v_even_w_out', 'v_rwkv_mu', 'v_rwkv_w0', 'v_rwkv_w2', 'v_rwkv_a0', 'v_rwkv_a2', 'v_rwkv_g2', 'v_rwkv_k_k', 'v_rwkv_k_a', 'v_rwkv_r_k', 'v_rwkv_gn_g', 'v_rwkv_gn_b', 'v_gdn_conv_w', 'v_gdn_A_log', 'v_gdn_dt_bias', 'v_gdn_norm_g', 'v_odd_w_in', 'v_odd_w_out', 'v_mamba_conv_w', 'v_mamba_conv_b', 'v_mamba_dt_bias', 'v_mamba_A_log', 'v_mamba_D', 'v_mamba_norm_g', 'v_lru_conv_w', 'v_lru_conv_b', 'v_lru_wa', 'v_lru_ba', 'v_lru_wx', 'v_lru_bx', 'v_lru_lambda']
TWIN_OUTPUTS = ['loss', 'grad_x', 'grad_ln1_g', 'grad_ln1_b', 'grad_ln2_g', 'grad_ln2_b', 'grad_ffn_up', 'grad_ffn_conv_w', 'grad_ffn_conv_b', 'grad_ffn_down', 'grad_ple_proj', 'grad_ple_norm_g', 'grad_ple_gate_w', 'grad_ple_gate_b', 'grad_even_w_in', 'grad_even_w_out', 'grad_rwkv_mu', 'grad_rwkv_w0', 'grad_rwkv_w2', 'grad_rwkv_a0', 'grad_rwkv_a2', 'grad_rwkv_g2', 'grad_rwkv_k_k', 'grad_rwkv_k_a', 'grad_rwkv_r_k', 'grad_rwkv_gn_g', 'grad_rwkv_gn_b', 'grad_gdn_conv_w', 'grad_gdn_A_log', 'grad_gdn_dt_bias', 'grad_gdn_norm_g', 'grad_odd_w_in', 'grad_odd_w_out', 'grad_mamba_conv_w', 'grad_mamba_conv_b', 'grad_mamba_dt_bias', 'grad_mamba_A_log', 'grad_mamba_D', 'grad_mamba_norm_g', 'grad_lru_conv_w', 'grad_lru_conv_b', 'grad_lru_wa', 'grad_lru_ba', 'grad_lru_wx', 'grad_lru_bx', 'grad_lru_lambda', 'delta_ln1_g', 'delta_ln1_b', 'delta_ln2_g', 'delta_ln2_b', 'delta_ffn_up', 'delta_ffn_conv_w', 'delta_ffn_conv_b', 'delta_ffn_down', 'delta_ple_proj', 'delta_ple_norm_g', 'delta_ple_gate_w', 'delta_ple_gate_b', 'delta_even_w_in', 'delta_even_w_out', 'delta_rwkv_mu', 'delta_rwkv_w0', 'delta_rwkv_w2', 'delta_rwkv_a0', 'delta_rwkv_a2', 'delta_rwkv_g2', 'delta_rwkv_k_k', 'delta_rwkv_k_a', 'delta_rwkv_r_k', 'delta_rwkv_gn_g', 'delta_rwkv_gn_b', 'delta_gdn_conv_w', 'delta_gdn_A_log', 'delta_gdn_dt_bias', 'delta_gdn_norm_g', 'delta_odd_w_in', 'delta_odd_w_out', 'delta_mamba_conv_w', 'delta_mamba_conv_b', 'delta_mamba_dt_bias', 'delta_mamba_A_log', 'delta_mamba_D', 'delta_mamba_norm_g', 'delta_lru_conv_w', 'delta_lru_conv_b', 'delta_lru_wa', 'delta_lru_ba', 'delta_lru_wx', 'delta_lru_bx', 'delta_lru_lambda', 'new_m_ln1_g', 'new_m_ln1_b', 'new_m_ln2_g', 'new_m_ln2_b', 'new_m_ffn_up', 'new_m_ffn_conv_w', 'new_m_ffn_conv_b', 'new_m_ffn_down', 'new_m_ple_proj', 'new_m_ple_norm_g', 'new_m_ple_gate_w', 'new_m_ple_gate_b', 'new_m_even_w_in', 'new_m_even_w_out', 'new_m_rwkv_mu', 'new_m_rwkv_w0', 'new_m_rwkv_w2', 'new_m_rwkv_a0', 'new_m_rwkv_a2', 'new_m_rwkv_g2', 'new_m_rwkv_k_k', 'new_m_rwkv_k_a', 'new_m_rwkv_r_k', 'new_m_rwkv_gn_g', 'new_m_rwkv_gn_b', 'new_m_gdn_conv_w', 'new_m_gdn_A_log', 'new_m_gdn_dt_bias', 'new_m_gdn_norm_g', 'new_m_odd_w_in', 'new_m_odd_w_out', 'new_m_mamba_conv_w', 'new_m_mamba_conv_b', 'new_m_mamba_dt_bias', 'new_m_mamba_A_log', 'new_m_mamba_D', 'new_m_mamba_norm_g', 'new_m_lru_conv_w', 'new_m_lru_conv_b', 'new_m_lru_wa', 'new_m_lru_ba', 'new_m_lru_wx', 'new_m_lru_bx', 'new_m_lru_lambda', 'new_v_ln1_g', 'new_v_ln1_b', 'new_v_ln2_g', 'new_v_ln2_b', 'new_v_ffn_up', 'new_v_ffn_conv_w', 'new_v_ffn_conv_b', 'new_v_ffn_down', 'new_v_ple_proj', 'new_v_ple_norm_g', 'new_v_ple_gate_w', 'new_v_ple_gate_b', 'new_v_even_w_in', 'new_v_even_w_out', 'new_v_rwkv_mu', 'new_v_rwkv_w0', 'new_v_rwkv_w2', 'new_v_rwkv_a0', 'new_v_rwkv_a2', 'new_v_rwkv_g2', 'new_v_rwkv_k_k', 'new_v_rwkv_k_a', 'new_v_rwkv_r_k', 'new_v_rwkv_gn_g', 'new_v_rwkv_gn_b', 'new_v_gdn_conv_w', 'new_v_gdn_A_log', 'new_v_gdn_dt_bias', 'new_v_gdn_norm_g', 'new_v_odd_w_in', 'new_v_odd_w_out', 'new_v_mamba_conv_w', 'new_v_mamba_conv_b', 'new_v_mamba_dt_bias', 'new_v_mamba_A_log', 'new_v_mamba_D', 'new_v_mamba_norm_g', 'new_v_lru_conv_w', 'new_v_lru_conv_b', 'new_v_lru_wa', 'new_v_lru_ba', 'new_v_lru_wx', 'new_v_lru_bx', 'new_v_lru_lambda']
TWIN_LEAF_KINDS = {'loss': 'loss', 'grad_x': 'grad_x', 'grad_ln1_g': 'grad_w', 'grad_ln1_b': 'grad_w', 'grad_ln2_g': 'grad_w', 'grad_ln2_b': 'grad_w', 'grad_ffn_up': 'grad_w', 'grad_ffn_conv_w': 'grad_w', 'grad_ffn_conv_b': 'grad_w', 'grad_ffn_down': 'grad_w', 'grad_ple_proj': 'grad_w', 'grad_ple_norm_g': 'grad_w', 'grad_ple_gate_w': 'grad_w', 'grad_ple_gate_b': 'grad_w', 'grad_even_w_in': 'grad_w', 'grad_even_w_out': 'grad_w', 'grad_rwkv_mu': 'grad_w', 'grad_rwkv_w0': 'grad_w', 'grad_rwkv_w2': 'grad_w', 'grad_rwkv_a0': 'grad_w', 'grad_rwkv_a2': 'grad_w', 'grad_rwkv_g2': 'grad_w', 'grad_rwkv_k_k': 'grad_w', 'grad_rwkv_k_a': 'grad_w', 'grad_rwkv_r_k': 'grad_w', 'grad_rwkv_gn_g': 'grad_w', 'grad_rwkv_gn_b': 'grad_w', 'grad_gdn_conv_w': 'grad_w', 'grad_gdn_A_log': 'grad_w', 'grad_gdn_dt_bias': 'grad_w', 'grad_gdn_norm_g': 'grad_w', 'grad_odd_w_in': 'grad_w', 'grad_odd_w_out': 'grad_w', 'grad_mamba_conv_w': 'grad_w', 'grad_mamba_conv_b': 'grad_w', 'grad_mamba_dt_bias': 'grad_w', 'grad_mamba_A_log': 'grad_w', 'grad_mamba_D': 'grad_w', 'grad_mamba_norm_g': 'grad_w', 'grad_lru_conv_w': 'grad_w', 'grad_lru_conv_b': 'grad_w', 'grad_lru_wa': 'grad_w', 'grad_lru_ba': 'grad_w', 'grad_lru_wx': 'grad_w', 'grad_lru_bx': 'grad_w', 'grad_lru_lambda': 'grad_w', 'delta_ln1_g': 'delta_w', 'delta_ln1_b': 'delta_w', 'delta_ln2_g': 'delta_w', 'delta_ln2_b': 'delta_w', 'delta_ffn_up': 'delta_w', 'delta_ffn_conv_w': 'delta_w', 'delta_ffn_conv_b': 'delta_w', 'delta_ffn_down': 'delta_w', 'delta_ple_proj': 'delta_w', 'delta_ple_norm_g': 'delta_w', 'delta_ple_gate_w': 'delta_w', 'delta_ple_gate_b': 'delta_w', 'delta_even_w_in': 'delta_w', 'delta_even_w_out': 'delta_w', 'delta_rwkv_mu': 'delta_w', 'delta_rwkv_w0': 'delta_w', 'delta_rwkv_w2': 'delta_w', 'delta_rwkv_a0': 'delta_w', 'delta_rwkv_a2': 'delta_w', 'delta_rwkv_g2': 'delta_w', 'delta_rwkv_k_k': 'delta_w', 'delta_rwkv_k_a': 'delta_w', 'delta_rwkv_r_k': 'delta_w', 'delta_rwkv_gn_g': 'delta_w', 'delta_rwkv_gn_b': 'delta_w', 'delta_gdn_conv_w': 'delta_w', 'delta_gdn_A_log': 'delta_w', 'delta_gdn_dt_bias': 'delta_w', 'delta_gdn_norm_g': 'delta_w', 'delta_odd_w_in': 'delta_w', 'delta_odd_w_out': 'delta_w', 'delta_mamba_conv_w': 'delta_w', 'delta_mamba_conv_b': 'delta_w', 'delta_mamba_dt_bias': 'delta_w', 'delta_mamba_A_log': 'delta_w', 'delta_mamba_D': 'delta_w', 'delta_mamba_norm_g': 'delta_w', 'delta_lru_conv_w': 'delta_w', 'delta_lru_conv_b': 'delta_w', 'delta_lru_wa': 'delta_w', 'delta_lru_ba': 'delta_w', 'delta_lru_wx': 'delta_w', 'delta_lru_bx': 'delta_w', 'delta_lru_lambda': 'delta_w', 'new_m_ln1_g': 'new_m', 'new_m_ln1_b': 'new_m', 'new_m_ln2_g': 'new_m', 'new_m_ln2_b': 'new_m', 'new_m_ffn_up': 'new_m', 'new_m_ffn_conv_w': 'new_m', 'new_m_ffn_conv_b': 'new_m', 'new_m_ffn_down': 'new_m', 'new_m_ple_proj': 'new_m', 'new_m_ple_norm_g': 'new_m', 'new_m_ple_gate_w': 'new_m', 'new_m_ple_gate_b': 'new_m', 'new_m_even_w_in': 'new_m', 'new_m_even_w_out': 'new_m', 'new_m_rwkv_mu': 'new_m', 'new_m_rwkv_w0': 'new_m', 'new_m_rwkv_w2': 'new_m', 'new_m_rwkv_a0': 'new_m', 'new_m_rwkv_a2': 'new_m', 'new_m_rwkv_g2': 'new_m', 'new_m_rwkv_k_k': 'new_m', 'new_m_rwkv_k_a': 'new_m', 'new_m_rwkv_r_k': 'new_m', 'new_m_rwkv_gn_g': 'new_m', 'new_m_rwkv_gn_b': 'new_m', 'new_m_gdn_conv_w': 'new_m', 'new_m_gdn_A_log': 'new_m', 'new_m_gdn_dt_bias': 'new_m', 'new_m_gdn_norm_g': 'new_m', 'new_m_odd_w_in': 'new_m', 'new_m_odd_w_out': 'new_m', 'new_m_mamba_conv_w': 'new_m', 'new_m_mamba_conv_b': 'new_m', 'new_m_mamba_dt_bias': 'new_m', 'new_m_mamba_A_log': 'new_m', 'new_m_mamba_D': 'new_m', 'new_m_mamba_norm_g': 'new_m', 'new_m_lru_conv_w': 'new_m', 'new_m_lru_conv_b': 'new_m', 'new_m_lru_wa': 'new_m', 'new_m_lru_ba': 'new_m', 'new_m_lru_wx': 'new_m', 'new_m_lru_bx': 'new_m', 'new_m_lru_lambda': 'new_m', 'new_v_ln1_g': 'new_v', 'new_v_ln1_b': 'new_v', 'new_v_ln2_g': 'new_v', 'new_v_ln2_b': 'new_v', 'new_v_ffn_up': 'new_v', 'new_v_ffn_conv_w': 'new_v', 'new_v_ffn_conv_b': 'new_v', 'new_v_ffn_down': 'new_v', 'new_v_ple_proj': 'new_v', 'new_v_ple_norm_g': 'new_v', 'new_v_ple_gate_w': 'new_v', 'new_v_ple_gate_b': 'new_v', 'new_v_even_w_in': 'new_v', 'new_v_even_w_out': 'new_v', 'new_v_rwkv_mu': 'new_v', 'new_v_rwkv_w0': 'new_v', 'new_v_rwkv_w2': 'new_v', 'new_v_rwkv_a0': 'new_v', 'new_v_rwkv_a2': 'new_v', 'new_v_rwkv_g2': 'new_v', 'new_v_rwkv_k_k': 'new_v', 'new_v_rwkv_k_a': 'new_v', 'new_v_rwkv_r_k': 'new_v', 'new_v_rwkv_gn_g': 'new_v', 'new_v_rwkv_gn_b': 'new_v', 'new_v_gdn_conv_w': 'new_v', 'new_v_gdn_A_log': 'new_v', 'new_v_gdn_dt_bias': 'new_v', 'new_v_gdn_norm_g': 'new_v', 'new_v_odd_w_in': 'new_v', 'new_v_odd_w_out': 'new_v', 'new_v_mamba_conv_w': 'new_v', 'new_v_mamba_conv_b': 'new_v', 'new_v_mamba_dt_bias': 'new_v', 'new_v_mamba_A_log': 'new_v', 'new_v_mamba_D': 'new_v', 'new_v_mamba_norm_g': 'new_v', 'new_v_lru_conv_w': 'new_v', 'new_v_lru_conv_b': 'new_v', 'new_v_lru_wa': 'new_v', 'new_v_lru_ba': 'new_v', 'new_v_lru_wx': 'new_v', 'new_v_lru_bx': 'new_v', 'new_v_lru_lambda': 'new_v'}


def _forward(args):
    return _fwd_reference(*[args[k] for k in FWD_PARAMS])


def _output_shape():
    def fwd():
        inp = _fwd_setup_inputs(0)
        return _fwd_reference(*[inp[k] for k in FWD_PARAMS])
    out = _jax.eval_shape(fwd)
    return out.shape, out.dtype

N_MICROBATCH = 1
ADAM_LR = 0.001
ADAM_B1 = 0.9
ADAM_B2 = 0.999
ADAM_EPS = 1e-08
ADAM_WD = 0.01
ADAM_STEP = 10
PER_EXAMPLE_BATCH_AXIS = {'x': 0, 'p': 1, 'loss_target': 0}
SHARED_INPUTS = []
_WEIGHT_DTYPES = {'ln1_g': _jnp.float32, 'ln1_b': _jnp.float32, 'ln2_g': _jnp.float32, 'ln2_b': _jnp.float32, 'ffn_up': _jnp.float32, 'ffn_conv_w': _jnp.float32, 'ffn_conv_b': _jnp.float32, 'ffn_down': _jnp.float32, 'ple_proj': _jnp.float32, 'ple_norm_g': _jnp.float32, 'ple_gate_w': _jnp.float32, 'ple_gate_b': _jnp.float32, 'even_w_in': _jnp.float32, 'even_w_out': _jnp.float32, 'rwkv_mu': _jnp.float32, 'rwkv_w0': _jnp.float32, 'rwkv_w2': _jnp.float32, 'rwkv_a0': _jnp.float32, 'rwkv_a2': _jnp.float32, 'rwkv_g2': _jnp.float32, 'rwkv_k_k': _jnp.float32, 'rwkv_k_a': _jnp.float32, 'rwkv_r_k': _jnp.float32, 'rwkv_gn_g': _jnp.float32, 'rwkv_gn_b': _jnp.float32, 'gdn_conv_w': _jnp.float32, 'gdn_A_log': _jnp.float32, 'gdn_dt_bias': _jnp.float32, 'gdn_norm_g': _jnp.float32, 'odd_w_in': _jnp.float32, 'odd_w_out': _jnp.float32, 'mamba_conv_w': _jnp.float32, 'mamba_conv_b': _jnp.float32, 'mamba_dt_bias': _jnp.float32, 'mamba_A_log': _jnp.float32, 'mamba_D': _jnp.float32, 'mamba_norm_g': _jnp.float32, 'lru_conv_w': _jnp.float32, 'lru_conv_b': _jnp.float32, 'lru_wa': _jnp.float32, 'lru_ba': _jnp.float32, 'lru_wx': _jnp.float32, 'lru_bx': _jnp.float32, 'lru_lambda': _jnp.float32}
MOMENT_SCALE = {'ln1_g': 5.215701e-01, 'ln1_b': 1.583762e+00, 'ln2_g': 1.169194e+01, 'ln2_b': 1.815976e+00, 'ffn_up': 1.349393e-02, 'ffn_conv_w': 1.358947e-02, 'ffn_conv_b': 4.941351e-02, 'ffn_down': 4.489740e-02, 'ple_proj': 6.315984e-02, 'ple_norm_g': 3.340258e+00, 'ple_gate_w': 7.378292e-02, 'ple_gate_b': 1.173635e+00, 'even_w_in': 1.890177e-02, 'even_w_out': 8.153145e-02, 'rwkv_mu': 3.311557e-02, 'rwkv_w0': 9.854099e-03, 'rwkv_w2': 1.425076e-03, 'rwkv_a0': 7.830213e-03, 'rwkv_a2': 6.976088e-03, 'rwkv_g2': 2.032659e-02, 'rwkv_k_k': 1.967980e-02, 'rwkv_k_a': 2.075506e-02, 'rwkv_r_k': 4.598570e-02, 'rwkv_gn_g': 1.874561e-02, 'rwkv_gn_b': 2.652830e-01, 'gdn_conv_w': 2.618249e-02, 'gdn_A_log': 2.130540e-01, 'gdn_dt_bias': 2.087991e-01, 'gdn_norm_g': 1.965996e-01, 'odd_w_in': 2.102017e-02, 'odd_w_out': 1.721499e-01, 'mamba_conv_w': 4.018969e-02, 'mamba_conv_b': 1.115911e-01, 'mamba_dt_bias': 7.937928e-02, 'mamba_A_log': 3.194115e-01, 'mamba_D': 1.515486e-01, 'mamba_norm_g': 7.420503e-02, 'lru_conv_w': 4.206305e-02, 'lru_conv_b': 4.976312e-01, 'lru_wa': 1.933015e-02, 'lru_ba': 1.255692e-02, 'lru_wx': 3.422494e-02, 'lru_bx': 1.544691e-02, 'lru_lambda': 2.241113e-02}


def _to_microbatches(a, axis):
    t = _jnp.moveaxis(a, axis, 0)
    t = t.reshape((N_MICROBATCH, t.shape[0] // N_MICROBATCH) + t.shape[1:])
    return _jnp.moveaxis(t, 1, axis + 1)


def setup_inputs(seed: int = 0) -> dict:
    inp = _fwd_setup_inputs(seed)
    key = _jax.random.fold_in(_jax.random.key(seed), 7919)
    shape, _ = _output_shape()
    out = dict(inp)
    out["loss_target"] = _jax.random.normal(_jax.random.fold_in(key, 0), shape, _jnp.float32)
    for i, name in enumerate(TWIN_WEIGHTS):
        w = inp[name].astype(_jnp.float32)
        if MOMENT_SCALE is None:
            s = _jnp.sqrt(_jnp.mean(_jnp.square(w)) + 1e-30)
        else:
            s = MOMENT_SCALE[name]
        km, kv = _jax.random.split(_jax.random.fold_in(key, i + 1))
        out[name] = w
        out["m_" + name] = s * _jax.random.normal(km, w.shape, _jnp.float32)
        out["v_" + name] = (s * s) * _jax.random.uniform(kv, w.shape, _jnp.float32, 0.5, 1.5)
    if N_MICROBATCH > 1:
        for name, axis in PER_EXAMPLE_BATCH_AXIS.items():
            out[name] = _to_microbatches(out[name], axis)
    return {'x': out['x'], 'p': out['p'], 'ln1_g': out['ln1_g'], 'ln1_b': out['ln1_b'], 'ln2_g': out['ln2_g'], 'ln2_b': out['ln2_b'], 'ffn_up': out['ffn_up'], 'ffn_conv_w': out['ffn_conv_w'], 'ffn_conv_b': out['ffn_conv_b'], 'ffn_down': out['ffn_down'], 'ple_proj': out['ple_proj'], 'ple_norm_g': out['ple_norm_g'], 'ple_gate_w': out['ple_gate_w'], 'ple_gate_b': out['ple_gate_b'], 'even_w_in': out['even_w_in'], 'even_w_out': out['even_w_out'], 'rwkv_mu': out['rwkv_mu'], 'rwkv_w0': out['rwkv_w0'], 'rwkv_w2': out['rwkv_w2'], 'rwkv_a0': out['rwkv_a0'], 'rwkv_a2': out['rwkv_a2'], 'rwkv_g2': out['rwkv_g2'], 'rwkv_k_k': out['rwkv_k_k'], 'rwkv_k_a': out['rwkv_k_a'], 'rwkv_r_k': out['rwkv_r_k'], 'rwkv_gn_g': out['rwkv_gn_g'], 'rwkv_gn_b': out['rwkv_gn_b'], 'gdn_conv_w': out['gdn_conv_w'], 'gdn_A_log': out['gdn_A_log'], 'gdn_dt_bias': out['gdn_dt_bias'], 'gdn_norm_g': out['gdn_norm_g'], 'odd_w_in': out['odd_w_in'], 'odd_w_out': out['odd_w_out'], 'mamba_conv_w': out['mamba_conv_w'], 'mamba_conv_b': out['mamba_conv_b'], 'mamba_dt_bias': out['mamba_dt_bias'], 'mamba_A_log': out['mamba_A_log'], 'mamba_D': out['mamba_D'], 'mamba_norm_g': out['mamba_norm_g'], 'lru_conv_w': out['lru_conv_w'], 'lru_conv_b': out['lru_conv_b'], 'lru_wa': out['lru_wa'], 'lru_ba': out['lru_ba'], 'lru_wx': out['lru_wx'], 'lru_bx': out['lru_bx'], 'lru_lambda': out['lru_lambda'], 'loss_target': out['loss_target'], 'm_ln1_g': out['m_ln1_g'], 'm_ln1_b': out['m_ln1_b'], 'm_ln2_g': out['m_ln2_g'], 'm_ln2_b': out['m_ln2_b'], 'm_ffn_up': out['m_ffn_up'], 'm_ffn_conv_w': out['m_ffn_conv_w'], 'm_ffn_conv_b': out['m_ffn_conv_b'], 'm_ffn_down': out['m_ffn_down'], 'm_ple_proj': out['m_ple_proj'], 'm_ple_norm_g': out['m_ple_norm_g'], 'm_ple_gate_w': out['m_ple_gate_w'], 'm_ple_gate_b': out['m_ple_gate_b'], 'm_even_w_in': out['m_even_w_in'], 'm_even_w_out': out['m_even_w_out'], 'm_rwkv_mu': out['m_rwkv_mu'], 'm_rwkv_w0': out['m_rwkv_w0'], 'm_rwkv_w2': out['m_rwkv_w2'], 'm_rwkv_a0': out['m_rwkv_a0'], 'm_rwkv_a2': out['m_rwkv_a2'], 'm_rwkv_g2': out['m_rwkv_g2'], 'm_rwkv_k_k': out['m_rwkv_k_k'], 'm_rwkv_k_a': out['m_rwkv_k_a'], 'm_rwkv_r_k': out['m_rwkv_r_k'], 'm_rwkv_gn_g': out['m_rwkv_gn_g'], 'm_rwkv_gn_b': out['m_rwkv_gn_b'], 'm_gdn_conv_w': out['m_gdn_conv_w'], 'm_gdn_A_log': out['m_gdn_A_log'], 'm_gdn_dt_bias': out['m_gdn_dt_bias'], 'm_gdn_norm_g': out['m_gdn_norm_g'], 'm_odd_w_in': out['m_odd_w_in'], 'm_odd_w_out': out['m_odd_w_out'], 'm_mamba_conv_w': out['m_mamba_conv_w'], 'm_mamba_conv_b': out['m_mamba_conv_b'], 'm_mamba_dt_bias': out['m_mamba_dt_bias'], 'm_mamba_A_log': out['m_mamba_A_log'], 'm_mamba_D': out['m_mamba_D'], 'm_mamba_norm_g': out['m_mamba_norm_g'], 'm_lru_conv_w': out['m_lru_conv_w'], 'm_lru_conv_b': out['m_lru_conv_b'], 'm_lru_wa': out['m_lru_wa'], 'm_lru_ba': out['m_lru_ba'], 'm_lru_wx': out['m_lru_wx'], 'm_lru_bx': out['m_lru_bx'], 'm_lru_lambda': out['m_lru_lambda'], 'v_ln1_g': out['v_ln1_g'], 'v_ln1_b': out['v_ln1_b'], 'v_ln2_g': out['v_ln2_g'], 'v_ln2_b': out['v_ln2_b'], 'v_ffn_up': out['v_ffn_up'], 'v_ffn_conv_w': out['v_ffn_conv_w'], 'v_ffn_conv_b': out['v_ffn_conv_b'], 'v_ffn_down': out['v_ffn_down'], 'v_ple_proj': out['v_ple_proj'], 'v_ple_norm_g': out['v_ple_norm_g'], 'v_ple_gate_w': out['v_ple_gate_w'], 'v_ple_gate_b': out['v_ple_gate_b'], 'v_even_w_in': out['v_even_w_in'], 'v_even_w_out': out['v_even_w_out'], 'v_rwkv_mu': out['v_rwkv_mu'], 'v_rwkv_w0': out['v_rwkv_w0'], 'v_rwkv_w2': out['v_rwkv_w2'], 'v_rwkv_a0': out['v_rwkv_a0'], 'v_rwkv_a2': out['v_rwkv_a2'], 'v_rwkv_g2': out['v_rwkv_g2'], 'v_rwkv_k_k': out['v_rwkv_k_k'], 'v_rwkv_k_a': out['v_rwkv_k_a'], 'v_rwkv_r_k': out['v_rwkv_r_k'], 'v_rwkv_gn_g': out['v_rwkv_gn_g'], 'v_rwkv_gn_b': out['v_rwkv_gn_b'], 'v_gdn_conv_w': out['v_gdn_conv_w'], 'v_gdn_A_log': out['v_gdn_A_log'], 'v_gdn_dt_bias': out['v_gdn_dt_bias'], 'v_gdn_norm_g': out['v_gdn_norm_g'], 'v_odd_w_in': out['v_odd_w_in'], 'v_odd_w_out': out['v_odd_w_out'], 'v_mamba_conv_w': out['v_mamba_conv_w'], 'v_mamba_conv_b': out['v_mamba_conv_b'], 'v_mamba_dt_bias': out['v_mamba_dt_bias'], 'v_mamba_A_log': out['v_mamba_A_log'], 'v_mamba_D': out['v_mamba_D'], 'v_mamba_norm_g': out['v_mamba_norm_g'], 'v_lru_conv_w': out['v_lru_conv_w'], 'v_lru_conv_b': out['v_lru_conv_b'], 'v_lru_wa': out['v_lru_wa'], 'v_lru_ba': out['v_lru_ba'], 'v_lru_wx': out['v_lru_wx'], 'v_lru_bx': out['v_lru_bx'], 'v_lru_lambda': out['v_lru_lambda']}


def _loss(weights, diff, rest, loss_target):
    with _jax.named_scope("forward"):
        args = {**rest, TWIN_DIFF_INPUT: diff, **{k: w.astype(_WEIGHT_DTYPES[k]) for k, w in weights.items()}}
        y = _forward(args)
    with _jax.named_scope("loss_head"):
        err = _jnp.square(y.astype(_jnp.float32) - loss_target)
        return 0.5 * _jnp.sum(_jnp.mean(err, axis=-1)) if err.ndim else 0.5 * err


def _adamw(w, g, m, v):
    m = ADAM_B1 * m + (1.0 - ADAM_B1) * g
    v = ADAM_B2 * v + (1.0 - ADAM_B2) * _jnp.square(g)
    m_hat = m / (1.0 - ADAM_B1 ** ADAM_STEP)
    v_hat = v / (1.0 - ADAM_B2 ** ADAM_STEP)
    delta = -ADAM_LR * (m_hat / (_jnp.sqrt(v_hat) + ADAM_EPS) + ADAM_WD * w)
    return delta, m, v


def reference(x, p, ln1_g, ln1_b, ln2_g, ln2_b, ffn_up, ffn_conv_w, ffn_conv_b, ffn_down, ple_proj, ple_norm_g, ple_gate_w, ple_gate_b, even_w_in, even_w_out, rwkv_mu, rwkv_w0, rwkv_w2, rwkv_a0, rwkv_a2, rwkv_g2, rwkv_k_k, rwkv_k_a, rwkv_r_k, rwkv_gn_g, rwkv_gn_b, gdn_conv_w, gdn_A_log, gdn_dt_bias, gdn_norm_g, odd_w_in, odd_w_out, mamba_conv_w, mamba_conv_b, mamba_dt_bias, mamba_A_log, mamba_D, mamba_norm_g, lru_conv_w, lru_conv_b, lru_wa, lru_ba, lru_wx, lru_bx, lru_lambda, loss_target, m_ln1_g, m_ln1_b, m_ln2_g, m_ln2_b, m_ffn_up, m_ffn_conv_w, m_ffn_conv_b, m_ffn_down, m_ple_proj, m_ple_norm_g, m_ple_gate_w, m_ple_gate_b, m_even_w_in, m_even_w_out, m_rwkv_mu, m_rwkv_w0, m_rwkv_w2, m_rwkv_a0, m_rwkv_a2, m_rwkv_g2, m_rwkv_k_k, m_rwkv_k_a, m_rwkv_r_k, m_rwkv_gn_g, m_rwkv_gn_b, m_gdn_conv_w, m_gdn_A_log, m_gdn_dt_bias, m_gdn_norm_g, m_odd_w_in, m_odd_w_out, m_mamba_conv_w, m_mamba_conv_b, m_mamba_dt_bias, m_mamba_A_log, m_mamba_D, m_mamba_norm_g, m_lru_conv_w, m_lru_conv_b, m_lru_wa, m_lru_ba, m_lru_wx, m_lru_bx, m_lru_lambda, v_ln1_g, v_ln1_b, v_ln2_g, v_ln2_b, v_ffn_up, v_ffn_conv_w, v_ffn_conv_b, v_ffn_down, v_ple_proj, v_ple_norm_g, v_ple_gate_w, v_ple_gate_b, v_even_w_in, v_even_w_out, v_rwkv_mu, v_rwkv_w0, v_rwkv_w2, v_rwkv_a0, v_rwkv_a2, v_rwkv_g2, v_rwkv_k_k, v_rwkv_k_a, v_rwkv_r_k, v_rwkv_gn_g, v_rwkv_gn_b, v_gdn_conv_w, v_gdn_A_log, v_gdn_dt_bias, v_gdn_norm_g, v_odd_w_in, v_odd_w_out, v_mamba_conv_w, v_mamba_conv_b, v_mamba_dt_bias, v_mamba_A_log, v_mamba_D, v_mamba_norm_g, v_lru_conv_w, v_lru_conv_b, v_lru_wa, v_lru_ba, v_lru_wx, v_lru_bx, v_lru_lambda):
    given = dict(x=x, p=p, ln1_g=ln1_g, ln1_b=ln1_b, ln2_g=ln2_g, ln2_b=ln2_b, ffn_up=ffn_up, ffn_conv_w=ffn_conv_w, ffn_conv_b=ffn_conv_b, ffn_down=ffn_down, ple_proj=ple_proj, ple_norm_g=ple_norm_g, ple_gate_w=ple_gate_w, ple_gate_b=ple_gate_b, even_w_in=even_w_in, even_w_out=even_w_out, rwkv_mu=rwkv_mu, rwkv_w0=rwkv_w0, rwkv_w2=rwkv_w2, rwkv_a0=rwkv_a0, rwkv_a2=rwkv_a2, rwkv_g2=rwkv_g2, rwkv_k_k=rwkv_k_k, rwkv_k_a=rwkv_k_a, rwkv_r_k=rwkv_r_k, rwkv_gn_g=rwkv_gn_g, rwkv_gn_b=rwkv_gn_b, gdn_conv_w=gdn_conv_w, gdn_A_log=gdn_A_log, gdn_dt_bias=gdn_dt_bias, gdn_norm_g=gdn_norm_g, odd_w_in=odd_w_in, odd_w_out=odd_w_out, mamba_conv_w=mamba_conv_w, mamba_conv_b=mamba_conv_b, mamba_dt_bias=mamba_dt_bias, mamba_A_log=mamba_A_log, mamba_D=mamba_D, mamba_norm_g=mamba_norm_g, lru_conv_w=lru_conv_w, lru_conv_b=lru_conv_b, lru_wa=lru_wa, lru_ba=lru_ba, lru_wx=lru_wx, lru_bx=lru_bx, lru_lambda=lru_lambda, loss_target=loss_target, m_ln1_g=m_ln1_g, m_ln1_b=m_ln1_b, m_ln2_g=m_ln2_g, m_ln2_b=m_ln2_b, m_ffn_up=m_ffn_up, m_ffn_conv_w=m_ffn_conv_w, m_ffn_conv_b=m_ffn_conv_b, m_ffn_down=m_ffn_down, m_ple_proj=m_ple_proj, m_ple_norm_g=m_ple_norm_g, m_ple_gate_w=m_ple_gate_w, m_ple_gate_b=m_ple_gate_b, m_even_w_in=m_even_w_in, m_even_w_out=m_even_w_out, m_rwkv_mu=m_rwkv_mu, m_rwkv_w0=m_rwkv_w0, m_rwkv_w2=m_rwkv_w2, m_rwkv_a0=m_rwkv_a0, m_rwkv_a2=m_rwkv_a2, m_rwkv_g2=m_rwkv_g2, m_rwkv_k_k=m_rwkv_k_k, m_rwkv_k_a=m_rwkv_k_a, m_rwkv_r_k=m_rwkv_r_k, m_rwkv_gn_g=m_rwkv_gn_g, m_rwkv_gn_b=m_rwkv_gn_b, m_gdn_conv_w=m_gdn_conv_w, m_gdn_A_log=m_gdn_A_log, m_gdn_dt_bias=m_gdn_dt_bias, m_gdn_norm_g=m_gdn_norm_g, m_odd_w_in=m_odd_w_in, m_odd_w_out=m_odd_w_out, m_mamba_conv_w=m_mamba_conv_w, m_mamba_conv_b=m_mamba_conv_b, m_mamba_dt_bias=m_mamba_dt_bias, m_mamba_A_log=m_mamba_A_log, m_mamba_D=m_mamba_D, m_mamba_norm_g=m_mamba_norm_g, m_lru_conv_w=m_lru_conv_w, m_lru_conv_b=m_lru_conv_b, m_lru_wa=m_lru_wa, m_lru_ba=m_lru_ba, m_lru_wx=m_lru_wx, m_lru_bx=m_lru_bx, m_lru_lambda=m_lru_lambda, v_ln1_g=v_ln1_g, v_ln1_b=v_ln1_b, v_ln2_g=v_ln2_g, v_ln2_b=v_ln2_b, v_ffn_up=v_ffn_up, v_ffn_conv_w=v_ffn_conv_w, v_ffn_conv_b=v_ffn_conv_b, v_ffn_down=v_ffn_down, v_ple_proj=v_ple_proj, v_ple_norm_g=v_ple_norm_g, v_ple_gate_w=v_ple_gate_w, v_ple_gate_b=v_ple_gate_b, v_even_w_in=v_even_w_in, v_even_w_out=v_even_w_out, v_rwkv_mu=v_rwkv_mu, v_rwkv_w0=v_rwkv_w0, v_rwkv_w2=v_rwkv_w2, v_rwkv_a0=v_rwkv_a0, v_rwkv_a2=v_rwkv_a2, v_rwkv_g2=v_rwkv_g2, v_rwkv_k_k=v_rwkv_k_k, v_rwkv_k_a=v_rwkv_k_a, v_rwkv_r_k=v_rwkv_r_k, v_rwkv_gn_g=v_rwkv_gn_g, v_rwkv_gn_b=v_rwkv_gn_b, v_gdn_conv_w=v_gdn_conv_w, v_gdn_A_log=v_gdn_A_log, v_gdn_dt_bias=v_gdn_dt_bias, v_gdn_norm_g=v_gdn_norm_g, v_odd_w_in=v_odd_w_in, v_odd_w_out=v_odd_w_out, v_mamba_conv_w=v_mamba_conv_w, v_mamba_conv_b=v_mamba_conv_b, v_mamba_dt_bias=v_mamba_dt_bias, v_mamba_A_log=v_mamba_A_log, v_mamba_D=v_mamba_D, v_mamba_norm_g=v_mamba_norm_g, v_lru_conv_w=v_lru_conv_w, v_lru_conv_b=v_lru_conv_b, v_lru_wa=v_lru_wa, v_lru_ba=v_lru_ba, v_lru_wx=v_lru_wx, v_lru_bx=v_lru_bx, v_lru_lambda=v_lru_lambda)
    weights = {n: given[n] for n in TWIN_WEIGHTS}
    shared = {n: given[n] for n in SHARED_INPUTS}
    per_example = {n: given[n] for n in ['x', 'p']}
    grad_fn = _jax.value_and_grad(_loss, argnums=(0, 1))

    def one_microbatch(ex, loss_target):
        ex = dict(ex)
        diff = ex.pop(TWIN_DIFF_INPUT)
        return grad_fn(weights, diff, {**shared, **ex}, loss_target)

    if N_MICROBATCH == 1:
        loss, (grad_w, grad_x) = one_microbatch(per_example, given["loss_target"])
    else:
        def body(carry, xs):
            loss_sum, grad_sum = carry
            l_k, (gw_k, gx_k) = one_microbatch(xs[0], xs[1])
            with _jax.named_scope("update"):
                return (loss_sum + l_k, _jax.tree.map(_jnp.add, grad_sum, gw_k)), gx_k

        init = (_jnp.zeros((), _jnp.float32), _jax.tree.map(_jnp.zeros_like, weights))
        (loss, grad_w), grad_x = _jax.lax.scan(body, init, (per_example, given["loss_target"]))
    with _jax.named_scope("update"):
        delta_w, new_m, new_v = {}, {}, {}
        for n in TWIN_WEIGHTS:
            delta_w[n], new_m[n], new_v[n] = _adamw(weights[n], grad_w[n], given["m_" + n], given["v_" + n])
    return (loss, grad_x, *[grad_w[n] for n in TWIN_WEIGHTS], *[delta_w[n] for n in TWIN_WEIGHTS],
            *[new_m[n] for n in TWIN_WEIGHTS], *[new_v[n] for n in TWIN_WEIGHTS])
```

```python
import functools
import math

import jax
import jax.numpy as jnp
from jax import lax
from jax.experimental import pallas as pl
from jax.experimental.pallas import tpu as pltpu

F32 = jnp.float32
BF16 = jnp.bfloat16
HI = lax.Precision.HIGHEST
SYS = lax.Precision.HIGH
SDS = jax.ShapeDtypeStruct
MESH = pl.DeviceIdType.MESH

LANES = 128
VMEM_LIMIT = 56 * 1024 * 1024
N_DEV = 8
FLAT_COLS = 1024
MM_VMEM = 40 * 1024 * 1024

LN_EPS = 1e-5
RMS_EPS = 1e-6
L2_EPS = 1e-6
A_GN_EPS = 64e-5
LRU_C = 8.0
C_GROUPS = 4
RWKV_CHUNK = 64
GDN_CHUNK = 64
SSD_CHUNK = 128
SCAN_HEADS = 16

ADAM_LR, ADAM_B1, ADAM_B2, ADAM_EPS, ADAM_WD, ADAM_STEP = 0.001, 0.9, 0.999, 1e-08, 0.01, 10

WEIGHTS = ['ln1_g', 'ln1_b', 'ln2_g', 'ln2_b', 'ffn_up', 'ffn_conv_w', 'ffn_conv_b', 'ffn_down', 'ple_proj',
           'ple_norm_g', 'ple_gate_w', 'ple_gate_b', 'even_w_in', 'even_w_out', 'rwkv_mu', 'rwkv_w0', 'rwkv_w2',
           'rwkv_a0', 'rwkv_a2', 'rwkv_g2', 'rwkv_k_k', 'rwkv_k_a', 'rwkv_r_k', 'rwkv_gn_g', 'rwkv_gn_b',
           'gdn_conv_w', 'gdn_A_log', 'gdn_dt_bias', 'gdn_norm_g', 'odd_w_in', 'odd_w_out', 'mamba_conv_w',
           'mamba_conv_b', 'mamba_dt_bias', 'mamba_A_log', 'mamba_D', 'mamba_norm_g', 'lru_conv_w', 'lru_conv_b',
           'lru_wa', 'lru_ba', 'lru_wx', 'lru_bx', 'lru_lambda']
SHARD_AXIS = {'ffn_up': 2, 'ffn_conv_w': 2, 'ffn_down': 1, 'ple_proj': 2, 'ple_gate_w': 1, 'even_w_in': 2,
              'even_w_out': 1, 'rwkv_w2': 2, 'rwkv_a2': 2, 'rwkv_g2': 2, 'gdn_conv_w': 2, 'odd_w_in': 2,
              'odd_w_out': 1, 'mamba_conv_w': 2, 'mamba_conv_b': 1, 'mamba_norm_g': 1, 'lru_conv_w': 2,
              'lru_conv_b': 1, 'lru_ba': 1, 'lru_bx': 1, 'lru_lambda': 1}
MATRICES = ['ffn_up', 'ffn_down', 'ple_proj', 'ple_gate_w', 'even_w_in', 'even_w_out', 'rwkv_w2', 'rwkv_a2',
            'rwkv_g2', 'odd_w_in', 'odd_w_out']
SMALL_SHARDED = [n for n in WEIGHTS if n in SHARD_AXIS and n not in MATRICES]
REPLICATED = [n for n in WEIGHTS if n not in SHARD_AXIS]


def _cparams(sem):
    return pltpu.CompilerParams(dimension_semantics=sem, vmem_limit_bytes=VMEM_LIMIT)


def _rup(n, m):
    return -(-n // m) * m


def _pick(n, cands):
    for c in cands:
        if n % c == 0:
            return c
    return n


_DIMS = {'nn': (((1,), (0,)), ((), ())), 'nt': (((1,), (1,)), ((), ())), 'tn': (((0,), (0,)), ((), ()))}


def _mm(a, b, mode, name):
    if mode == 'tn':
        K, M = a.shape
    else:
        M, K = a.shape
    N = b.shape[0] if mode == 'nt' else b.shape[1]
    tm = _pick(M, (1024, 512, 256, 128))
    tn = _pick(N, (1024, 512, 256, 128))
    room = MM_VMEM - 3 * tm * tn * 4
    per_k = 2 * (tm * a.dtype.itemsize + tn * b.dtype.itemsize)
    tk = max([t for t in range(LANES, K + 1, LANES) if K % t == 0 and t * per_k <= room] or [K])
    nk = K // tk

    def body(a_ref, b_ref, o_ref, acc_ref):
        k = pl.program_id(2)
        part = lax.dot_general(a_ref[...].astype(BF16), b_ref[...].astype(BF16), _DIMS[mode],
                               preferred_element_type=F32)

        @pl.when(k == 0)
        def _():
            acc_ref[...] = part

        @pl.when(k > 0)
        def _():
            acc_ref[...] += part

        @pl.when(k == nk - 1)
        def _():
            o_ref[...] = acc_ref[...]

    a_spec = pl.BlockSpec((tk, tm), lambda i, j, k: (k, i)) if mode == 'tn' else pl.BlockSpec((tm, tk), lambda i, j, k: (i, k))
    b_spec = pl.BlockSpec((tn, tk), lambda i, j, k: (j, k)) if mode == 'nt' else pl.BlockSpec((tk, tn), lambda i, j, k: (k, j))
    return pl.pallas_call(
        body, grid=(M // tm, N // tn, nk), in_specs=[a_spec, b_spec],
        out_specs=pl.BlockSpec((tm, tn), lambda i, j, k: (i, j)), out_shape=SDS((M, N), F32),
        scratch_shapes=[pltpu.VMEM((tm, tn), F32)], name=name,
        compiler_params=_cparams(("parallel", "parallel", "arbitrary")))(a, b)


def _matmul(x, w16, wz, name):
    @jax.custom_vjp
    def op(x, wz):
        return _mm(x.astype(BF16), w16, 'nn', name + "_f")

    def op_f(x, wz):
        x16 = x.astype(BF16)
        return _mm(x16, w16, 'nn', name + "_f"), x16

    def op_b(x16, g):
        g16 = g.astype(BF16)
        return _mm(g16, w16, 'nt', name + "_dx"), _mm(x16, g16, 'tn', name + "_dw")

    op.defvjp(op_f, op_b)
    return op(x, wz)


def _bdmm_call(a, b, mode, nb, name):
    T = a.shape[0]
    bd = a.shape[1] // nb
    tt = _pick(T, (512, 256, 128))
    nt = T // tt
    if mode == 'tn':
        def body(a_ref, b_ref, o_ref):
            part = lax.dot_general(a_ref[...].astype(BF16), b_ref[...].astype(BF16), _DIMS['tn'],
                                   preferred_element_type=F32)

            @pl.when(pl.program_id(1) == 0)
            def _():
                o_ref[0] = part

            @pl.when(pl.program_id(1) > 0)
            def _():
                o_ref[0] += part

        return pl.pallas_call(
            body, grid=(nb, nt),
            in_specs=[pl.BlockSpec((tt, bd), lambda n, t: (t, n)), pl.BlockSpec((tt, bd), lambda n, t: (t, n))],
            out_specs=pl.BlockSpec((1, bd, bd), lambda n, t: (n, 0, 0)), out_shape=SDS((nb, bd, bd), F32),
            name=name, compiler_params=_cparams(("parallel", "arbitrary")))(a, b)

    def body(a_ref, b_ref, o_ref):
        o_ref[...] = lax.dot_general(a_ref[...].astype(BF16), b_ref[0].astype(BF16), _DIMS[mode],
                                     preferred_element_type=F32)

    return pl.pallas_call(
        body, grid=(nb, nt),
        in_specs=[pl.BlockSpec((tt, bd), lambda n, t: (t, n)), pl.BlockSpec((1, bd, bd), lambda n, t: (n, 0, 0))],
        out_specs=pl.BlockSpec((tt, bd), lambda n, t: (t, n)), out_shape=SDS(a.shape, F32),
        name=name, compiler_params=_cparams(("parallel", "parallel")))(a, b)


def _bdmm(x, w, name):
    nb = w.shape[0]

    @jax.custom_vjp
    def op(x, w):
        return _bdmm_call(x, w, 'nn', nb, name + "_f")

    def op_f(x, w):
        return op(x, w), (x, w)

    def op_b(res, g):
        x, w = res
        return _bdmm_call(g, w, 'nt', nb, name + "_dx"), _bdmm_call(x, g, 'tn', nb, name + "_dw")

    op.defvjp(op_f, op_b)
    return op(x, w)


def _tile_op(name, fn, arrs, params, consts, by_rows, width=LANES):
    arrs, params, consts = tuple(arrs), tuple(params), tuple(consts)
    na, npar, nc = len(arrs), len(params), len(consts)
    T = arrs[0].shape[0]
    if by_rows:
        tile = _pick(T, (256, 128, 64, 32, 16, 8))
        grid = (T // tile,)
        arr_block = lambda a: (tile, a.shape[1])
        arr_spec = lambda a: pl.BlockSpec((tile, a.shape[1]), lambda i: (i, 0))
        par_block = lambda p: p.shape
        par_spec = lambda p: pl.BlockSpec(p.shape, lambda i: (0, 0))
    else:
        grid = (arrs[0].shape[1] // width,)
        arr_block = lambda a: (T, width)
        arr_spec = lambda a: pl.BlockSpec((T, width), lambda i: (0, i))
        par_block = lambda p: (p.shape[0], width)
        par_spec = lambda p: pl.BlockSpec((p.shape[0], width), lambda i: (0, i))
    const_spec = lambda c: pl.BlockSpec(c.shape, lambda i: (0,) * c.ndim)
    outs_sds = jax.eval_shape(fn, *[SDS(arr_block(a), F32) for a in arrs], *[SDS(par_block(p), F32) for p in params],
                              *[SDS(c.shape, c.dtype) for c in consts])
    out_widths = [o.shape[1] for o in outs_sds]
    nout = len(out_widths)
    if by_rows:
        out_shapes = [SDS((T, w), F32) for w in out_widths]
        out_specs = [pl.BlockSpec((tile, w), lambda i: (i, 0)) for w in out_widths]
    else:
        out_shapes = [SDS((T, grid[0] * w), F32) for w in out_widths]
        out_specs = [pl.BlockSpec((T, w), lambda i: (0, i)) for w in out_widths]

    def fwd_call(arrs, params):
        def body(*refs):
            outs = fn(*[r[...] for r in refs[:na + npar + nc]])
            for o_ref, o in zip(refs[na + npar + nc:], outs):
                o_ref[...] = o

        return pl.pallas_call(
            body, grid=grid, in_specs=[arr_spec(a) for a in arrs] + [par_spec(p) for p in params] + [const_spec(c) for c in consts],
            out_specs=out_specs, out_shape=out_shapes, name=name + "_f",
            compiler_params=_cparams(("parallel",)))(*arrs, *params, *consts)

    def bwd_call(arrs, params, cts):
        def body(*refs):
            ins = refs[:na + npar + nc + nout]
            outs = refs[na + npar + nc + nout:]
            av = [r[...] for r in ins[:na]]
            pv = [r[...] for r in ins[na:na + npar]]
            cv = [r[...] for r in ins[na + npar:na + npar + nc]]
            gv = [r[...] for r in ins[na + npar + nc:]]
            _, vjp = jax.vjp(lambda *t: fn(*t, *cv), *av, *pv)
            grads = vjp(tuple(gv))
            for o_ref, g in zip(outs[:na], grads[:na]):
                o_ref[...] = g
            if by_rows and npar:
                @pl.when(pl.program_id(0) == 0)
                def _():
                    for o_ref in outs[na:]:
                        o_ref[...] = jnp.zeros_like(o_ref)

                for o_ref, g in zip(outs[na:], grads[na:]):
                    o_ref[...] += g
            else:
                for o_ref, g in zip(outs[na:], grads[na:]):
                    o_ref[...] = g

        return pl.pallas_call(
            body, grid=grid,
            in_specs=[arr_spec(a) for a in arrs] + [par_spec(p) for p in params] + [const_spec(c) for c in consts] + out_specs,
            out_specs=[arr_spec(a) for a in arrs] + [par_spec(p) for p in params],
            out_shape=[SDS(a.shape, F32) for a in arrs] + [SDS(p.shape, F32) for p in params], name=name + "_b",
            compiler_params=_cparams(("arbitrary",) if by_rows else ("parallel",)))(*arrs, *params, *consts, *cts)

    @jax.custom_vjp
    def op(arrs, params):
        return tuple(fwd_call(arrs, params))

    def op_f(arrs, params):
        return op(arrs, params), (arrs, params)

    def op_b(res, cts):
        arrs, params = res
        g = bwd_call(arrs, params, cts)
        return tuple(g[:na]), tuple(g[na:])

    op.defvjp(op_f, op_b)
    return op(arrs, params)


def _rowwise(name, fn, arrs, params=(), consts=()):
    return _tile_op(name, fn, arrs, params, consts, True)


def _colwise(name, fn, arrs, params=(), width=LANES):
    return _tile_op(name, fn, arrs, params, (), False, width)


@functools.partial(jax.custom_vjp, nondiff_argnums=(1,))
def _shift(x, k):
    rows = lax.broadcasted_iota(jnp.int32, x.shape, 0)
    return jnp.where(rows >= k, pltpu.roll(x, k, 0), 0.0)


def _shift_f(x, k):
    return _shift(x, k), None


def _shift_b(k, _, g):
    n = g.shape[0]
    rows = lax.broadcasted_iota(jnp.int32, g.shape, 0)
    return (jnp.where(rows < n - k, pltpu.roll(g, n - k, 0), 0.0),)


_shift.defvjp(_shift_f, _shift_b)


def _causal_conv(x, w):
    K = w.shape[0]
    y = x * w[K - 1:K, :]
    for j in range(K - 1):
        y = y + _shift(x, K - 1 - j) * w[j:j + 1, :]
    return y


def _silu(x):
    return x * jax.nn.sigmoid(x)


def _softplus(x):
    return jnp.maximum(x, 0.0) + jnp.log1p(jnp.exp(-jnp.abs(x)))


def _split_cols(h, offs, widths):
    @jax.custom_vjp
    def op(h):
        return tuple(h[:, o:o + w] for o, w in zip(offs, widths))

    def op_f(h):
        return op(h), None

    def op_b(_, cts):
        parts, pos = [], 0
        T = cts[0].shape[0]
        for o, w, c in zip(offs, widths, cts):
            if o > pos:
                parts.append(jnp.zeros((T, o - pos), F32))
            parts.append(c)
            pos = o + w
        if pos < h.shape[1]:
            parts.append(jnp.zeros((T, h.shape[1] - pos), F32))
        return (jnp.concatenate(parts, axis=1),)

    op.defvjp(op_f, op_b)
    return op(h)


def _group_ones(width, group):
    g = jnp.arange(width) // group
    return (g[:, None] == g[None, :]).astype(F32)


def _layer_norm_rows(x, g, b, eps):
    mu = jnp.mean(x, axis=1, keepdims=True)
    var = jnp.mean(jnp.square(x - mu), axis=1, keepdims=True)
    return (x - mu) * lax.rsqrt(var + eps) * g + b


def _tri(L, strict=False):
    i = lax.broadcasted_iota(jnp.int32, (L, L), 0)
    j = lax.broadcasted_iota(jnp.int32, (L, L), 1)
    return (i > j) if strict else (i >= j)


def _cumsum_rows(x):
    H, L, _ = x.shape
    tri = jnp.broadcast_to(_tri(L).astype(F32)[None], (H, L, L))
    return jnp.einsum('hls,hsn->hln', tri, x, precision=HI)


def _col_to_row(c):
    L = c.shape[1]
    return jnp.sum(c * _tri_eye(L)[None], axis=1, keepdims=True)


def _row_to_col(r):
    N = r.shape[2]
    return jnp.sum(r * _tri_eye(N)[None], axis=2, keepdims=True)


def _scalar_col(t):
    return _row_to_col(t.reshape(t.shape[0], 1, t.shape[3]))


def _tri_eye(L):
    i = lax.broadcasted_iota(jnp.int32, (L, L), 0)
    j = lax.broadcasted_iota(jnp.int32, (L, L), 1)
    return (i == j).astype(F32)


def _unit_lower_inverse(n_strict):
    L = n_strict.shape[1]
    inv = _tri_eye(L)[None] + n_strict
    x = n_strict
    p = 2
    while p < L:
        x = jnp.einsum('hij,hjk->hik', x, x)
        inv = inv + jnp.einsum('hij,hjk->hik', inv, x)
        p *= 2
    return inv


def _rwkv_chunk(r, lw, k, v, a, b, h0):
    L = r.shape[1]
    sm = functools.partial(jnp.einsum, precision=SYS)
    mm = jnp.einsum
    cum = _cumsum_rows(lw)
    cum_l = jnp.sum(lw, axis=1, keepdims=True)
    e_neg = jnp.exp(-cum)
    rt, bt, kt, at = r * jnp.exp(cum), b * e_neg, k * e_neg, a * jnp.exp(cum - lw)
    to_end = jnp.exp(cum_l - cum)
    strict, incl = _tri(L, True)[None], _tri(L)[None]
    n = jnp.where(strict, sm('hld,hsd->hls', at, bt), 0.0)
    mk = jnp.where(strict, sm('hld,hsd->hls', at, kt), 0.0)
    u = sm('hls,hsv->hlv', _unit_lower_inverse(n), mm('hld,hdv->hlv', at, h0) + mm('hls,hsv->hlv', mk, v))
    y = (mm('hld,hdv->hlv', rt, h0) + mm('hls,hsv->hlv', jnp.where(incl, sm('hld,hsd->hls', rt, bt), 0.0), u)
         + mm('hls,hsv->hlv', jnp.where(incl, sm('hld,hsd->hls', rt, kt), 0.0), v))
    h1 = (_row_to_col(jnp.exp(cum_l)) * h0 + mm('hld,hlv->hdv', b * to_end, u) + mm('hld,hlv->hdv', k * to_end, v))
    return y, h1


def _gdn_chunk(q, k, v, beta, lg, h0):
    C, D = q.shape[1], q.shape[2]
    scale = D ** -0.5
    beta, lg = _scalar_col(beta), _scalar_col(lg)
    gc = _cumsum_rows(lg)
    gc_l = jnp.sum(lg, axis=1, keepdims=True)
    causal, strict = _tri(C)[None], _tri(C, True)[None]
    decay = jnp.exp(jnp.where(causal, gc - _col_to_row(gc), -jnp.inf))
    k_beta = k * beta
    m = jnp.where(strict, jnp.einsum('hcd,hsd->hcs', k_beta, k) * decay, 0.0)
    inv = _unit_lower_inverse(-m)
    e_gc = jnp.exp(gc)
    u = jnp.einsum('hcs,hsd->hcd', inv, v * beta, precision=SYS)
    w = jnp.einsum('hcs,hsd->hcd', inv, k_beta * e_gc, precision=SYS)
    attn = jnp.where(causal, jnp.einsum('hcd,hsd->hcs', q * scale, k) * decay, 0.0)
    v_new = u - jnp.einsum('hcd,hde->hce', w, h0)
    o = jnp.einsum('hcd,hde->hce', q * scale * e_gc, h0) + jnp.einsum('hcs,hse->hce', attn, v_new)
    h1 = h0 * jnp.exp(gc_l) + jnp.einsum('hcd,hce->hde', k * jnp.exp(gc_l - gc), v_new)
    return o, h1


def _ssd_chunk(xs, dt, aa, bm, cm, h0):
    H, L, _ = xs.shape
    dt, aa = _scalar_col(dt), _scalar_col(aa)
    x = xs * dt
    cs = _cumsum_rows(aa)
    cs_l = jnp.sum(aa, axis=1, keepdims=True)
    causal = _tri(L)[None]
    cb = jnp.einsum('gln,gsn->gls', cm, bm)
    wd = jnp.where(causal, cb * jnp.exp(jnp.where(causal, cs - _col_to_row(cs), -jnp.inf)), 0.0)
    cmb = jnp.broadcast_to(cm, (H,) + cm.shape[1:])
    bmb = jnp.broadcast_to(bm, (H,) + bm.shape[1:])
    y = jnp.einsum('hls,hsp->hlp', wd, x) + jnp.einsum('hln,hnp->hlp', cmb, h0) * jnp.exp(cs)
    h1 = jnp.exp(cs_l) * h0 + jnp.einsum('hln,hlp->hnp', bmb, x * jnp.exp(cs_l - cs))
    return y, h1


def _chunk_scan(name, fn, seqs, hb, L, state_shape, out_width):
    seqs = tuple(seqs)
    ns = len(seqs)
    H = max(s.shape[0] for s in seqs)
    T = max(s.shape[1] for s in seqs)
    nc, nh = T // L, H // hb
    lead = [hb if s.shape[0] == H else 1 for s in seqs]
    st_block = (hb,) + state_shape

    def seq_spec(s, l, imap):
        if s.ndim == 4:
            return pl.BlockSpec((l, 1, 1, L), lambda h, c: imap(h, c) + (0,))
        return pl.BlockSpec((l, L, s.shape[2]), imap)

    fmap = lambda h, c: (h, c, 0)
    rmap = lambda h, c: (h, nc - 1 - c, 0)

    def fwd_call(seqs):
        def body(*refs):
            y_ref, st_ref, carry = refs[ns], refs[ns + 1], refs[ns + 2]

            @pl.when(pl.program_id(1) == 0)
            def _():
                carry[...] = jnp.zeros_like(carry)

            h0 = carry[...]
            st_ref[0] = h0
            y, h1 = fn(*[r[...] for r in refs[:ns]], h0)
            y_ref[...] = y
            carry[...] = h1

        return pl.pallas_call(
            body, grid=(nh, nc), in_specs=[seq_spec(s, l, fmap) for s, l in zip(seqs, lead)],
            out_specs=[pl.BlockSpec((hb, L, out_width), fmap),
                       pl.BlockSpec((1,) + st_block, lambda h, c: (c, h) + (0,) * len(state_shape))],
            out_shape=[SDS((H, T, out_width), F32), SDS((nc, H) + state_shape, F32)],
            scratch_shapes=[pltpu.VMEM(st_block, F32)], name=name + "_f",
            compiler_params=_cparams(("parallel", "arbitrary")))(*seqs)

    def bwd_call(seqs, states, dy):
        def body(*refs):
            st_ref, dy_ref = refs[ns], refs[ns + 1]
            outs, carry = refs[ns + 2:2 * ns + 2], refs[2 * ns + 2]

            @pl.when(pl.program_id(1) == 0)
            def _():
                carry[...] = jnp.zeros_like(carry)

            _, vjp = jax.vjp(fn, *[r[...] for r in refs[:ns]], st_ref[0])
            grads = vjp((dy_ref[...], carry[...]))
            for o_ref, g in zip(outs, grads[:ns]):
                o_ref[...] = g
            carry[...] = grads[ns]

        return pl.pallas_call(
            body, grid=(nh, nc),
            in_specs=[seq_spec(s, l, rmap) for s, l in zip(seqs, lead)]
            + [pl.BlockSpec((1,) + st_block, lambda h, c: (nc - 1 - c, h) + (0,) * len(state_shape)),
               pl.BlockSpec((hb, L, out_width), rmap)],
            out_specs=[seq_spec(s, l, rmap) for s, l in zip(seqs, lead)],
            out_shape=[SDS(s.shape, F32) for s in seqs],
            scratch_shapes=[pltpu.VMEM(st_block, F32)], name=name + "_b",
            compiler_params=_cparams(("parallel", "arbitrary")))(*seqs, states, dy)

    @jax.custom_vjp
    def op(seqs):
        return fwd_call(seqs)[0]

    def op_f(seqs):
        y, states = fwd_call(seqs)
        return y, (seqs, states)

    def op_b(res, dy):
        seqs, states = res
        return (tuple(bwd_call(seqs, states, dy)),)

    op.defvjp(op_f, op_b)
    return op(seqs)


def _lru_scan_call(a, u, h, reverse, name):
    T, C = a.shape
    cw = _pick(C, (1024, 512, 256, 128))
    tt = _pick(T, (512, 256, 128, 64, 32, 16, 8))
    nt, ng = T // tt, tt // 8
    sub = lambda: lax.broadcasted_iota(jnp.int32, (8, cw), 0)
    first = lambda: pl.program_id(1) == 0

    def fwd_body(a_ref, u_ref, h_ref, carry_ref):
        @pl.when(first())
        def _():
            carry_ref[...] = jnp.zeros_like(carry_ref)

        def group(i, carry):
            r0 = pl.multiple_of(i * 8, 8)
            ab, ub = a_ref[pl.ds(r0, 8), :], u_ref[pl.ds(r0, 8), :]
            out = jnp.zeros((8, cw), F32)
            for j in range(8):
                carry = ab[j:j + 1, :] * carry + ub[j:j + 1, :]
                out = jnp.where(sub() == j, carry, out)
            h_ref[pl.ds(r0, 8), :] = out
            return carry

        carry_ref[...] = lax.fori_loop(0, ng, group, carry_ref[...])

    def bwd_body(a_ref, u_ref, h_ref, hp_ref, g_ref, da_ref, cg_ref, ca_ref):
        @pl.when(first())
        def _():
            cg_ref[...] = jnp.zeros_like(cg_ref)
            ca_ref[...] = jnp.zeros_like(ca_ref)

        h_before = jnp.where(pl.program_id(1) < nt - 1, hp_ref[7:8, :], 0.0)

        def group(i, carry):
            g_next, a_next = carry
            gi = ng - 1 - i
            r0 = pl.multiple_of(gi * 8, 8)
            rp = pl.multiple_of(jnp.maximum(gi - 1, 0) * 8, 8)
            ab, ub, hb = a_ref[pl.ds(r0, 8), :], u_ref[pl.ds(r0, 8), :], h_ref[pl.ds(r0, 8), :]
            h_last_prev = jnp.where(gi > 0, h_ref[pl.ds(rp, 8), :][7:8, :], h_before)
            g_out = jnp.zeros((8, cw), F32)
            da_out = jnp.zeros((8, cw), F32)
            for j in range(7, -1, -1):
                g_next = ub[j:j + 1, :] + a_next * g_next
                a_next = ab[j:j + 1, :]
                h_prev = hb[j - 1:j, :] if j > 0 else h_last_prev
                g_out = jnp.where(sub() == j, g_next, g_out)
                da_out = jnp.where(sub() == j, g_next * h_prev, da_out)
            g_ref[pl.ds(r0, 8), :] = g_out
            da_ref[pl.ds(r0, 8), :] = da_out
            return g_next, a_next

        cg_ref[...], ca_ref[...] = lax.fori_loop(0, ng, group, (cg_ref[...], ca_ref[...]))

    row = pltpu.VMEM((1, cw), F32)
    if not reverse:
        spec = pl.BlockSpec((tt, cw), lambda i, t: (t, i))
        return pl.pallas_call(fwd_body, grid=(C // cw, nt), in_specs=[spec, spec], out_specs=spec,
                              out_shape=SDS((T, C), F32), scratch_shapes=[row], name=name,
                              compiler_params=_cparams(("parallel", "arbitrary")))(a, u)
    spec = pl.BlockSpec((tt, cw), lambda i, t: (nt - 1 - t, i))
    before = pl.BlockSpec((8, cw), lambda i, t: (jnp.maximum((nt - 1 - t) * ng - 1, 0), i))
    return pl.pallas_call(bwd_body, grid=(C // cw, nt), in_specs=[spec, spec, spec, before], out_specs=[spec, spec],
                          out_shape=[SDS((T, C), F32), SDS((T, C), F32)], scratch_shapes=[row, row], name=name,
                          compiler_params=_cparams(("parallel", "arbitrary")))(a, u, h, h)


@jax.custom_vjp
def _lru_scan(a, u):
    return _lru_scan_call(a, u, None, False, "lru_scan_f")


def _lru_scan_f(a, u):
    h = _lru_scan(a, u)
    return h, (a, h)


def _lru_scan_b(res, dh):
    a, h = res
    g, da = _lru_scan_call(a, dh, h, True, "lru_scan_b")
    return da, g


_lru_scan.defvjp(_lru_scan_f, _lru_scan_b)


def _heads_major(x, nheads):
    T, W = x.shape
    return jnp.transpose(x.reshape(T, nheads, W // nheads), (1, 0, 2))


def _tokens_major(x):
    H, T, N = x.shape
    return jnp.transpose(x, (1, 0, 2)).reshape(T, H * N)


def _pad_cols(w, offs, widths, total):
    parts, pos, src = [], 0, 0
    for o, wd in zip(offs, widths):
        if o > pos:
            parts.append(jnp.zeros((w.shape[0], o - pos), w.dtype))
        parts.append(w[:, src:src + wd])
        src += wd
        pos = o + wd
    if pos < total:
        parts.append(jnp.zeros((w.shape[0], total - pos), w.dtype))
    return jnp.concatenate(parts, axis=1)


def _unpad_cols(w, offs, widths):
    return jnp.concatenate([w[:, o:o + wd] for o, wd in zip(offs, widths)], axis=1)


def _aligned_layout(widths):
    offs, pos = [], 0
    for w in widths:
        offs.append(pos)
        pos += _rup(w, LANES)
    return offs, _rup(pos, 512)


def _pad_lanes(v, n):
    return jnp.pad(v, ((0, 0), (0, n - v.shape[1])))


def _even_mixer(x, q, wz, li):
    T = x.shape[0]
    ah, an = q['rwkv_r_k'].shape
    aw = ah * an
    bh, bn = q['gdn_A_log'].shape[0], q['gdn_norm_g'].shape[0]
    bw = bh * bn
    lw_, la_, lg_ = q['rwkv_w2'].shape[0], q['rwkv_a2'].shape[0], q['rwkv_g2'].shape[0]
    widths = [aw, aw, aw, lw_, la_, lg_, bw, bw, bw, bw, bh, bh]
    offs, total = _aligned_layout(widths)
    pw = [_rup(w, LANES) for w in widths]
    hcols = _matmul(x, _pad_cols(q['even_w_in'], offs, widths, total), wz['even_w_in'], f"even_in{li}")
    a_w = offs[6]
    a_cols, bq, bk, bv, bz, beta_raw, alpha_raw = _split_cols(hcols, [0] + offs[6:], [a_w] + pw[6:])

    mu = _pad_cols(q['rwkv_mu'][None], offs[:6], widths[:6], a_w)
    (xs,) = _colwise(f"rwkv_shift{li}", lambda h, m: (h + (_shift(h, 1) - h) * m,), [a_cols], [mu])
    r, k, v, w_lo, a_lo, g_lo = _split_cols(xs, offs[:6], pw[:6])
    tw, sg = _rowwise(f"rwkv_lora_act{li}", lambda w, g: (jnp.tanh(w), jax.nn.sigmoid(g)), [w_lo, g_lo])
    pad_rows = lambda w, n: jnp.pad(w, ((0, n - w.shape[0]), (0, 0)))
    wl = _matmul(tw, pad_rows(q['rwkv_w2'], pw[3]), wz['rwkv_w2'], f"rwkv_w2{li}")
    al = _matmul(a_lo, pad_rows(q['rwkv_a2'], pw[4]), wz['rwkv_a2'], f"rwkv_a2{li}")
    g = _matmul(sg, pad_rows(q['rwkv_g2'], pw[5]), wz['rwkv_g2'], f"rwkv_g2{li}")
    ones_a = _group_ones(aw, an)

    def pre(k, wl, al, w0, a0, k_k, k_a, ones):
        lw = -jnp.exp(-_softplus(-(w0 + wl)) - 0.5)
        a = jax.nn.sigmoid(a0 + al)
        kk = k * k_k
        kk = kk * lax.rsqrt(jnp.dot(kk * kk, ones, precision=SYS) + L2_EPS)
        return lw, k * (1.0 + (a - 1.0) * k_a), -kk, kk * a

    lw, k2, sa, sb = _rowwise(f"rwkv_pre{li}", pre, [k, wl, al],
                              [q['rwkv_w0'][None], q['rwkv_a0'][None], q['rwkv_k_k'][None], q['rwkv_k_a'][None]], [ones_a])
    hm = lambda t: _heads_major(t, ah)
    out = _chunk_scan(f"rwkv_scan{li}", _rwkv_chunk, [hm(r), hm(lw), hm(k2), hm(v), hm(sa), hm(sb)],
                      min(ah, SCAN_HEADS), min(RWKV_CHUNK, T), (an, an), an)
    out = _tokens_major(out)

    def post(out, r, k2, v, g, gn_g, gn_b, r_k, ones):
        mean = jnp.dot(out, ones, precision=SYS) * (1.0 / an)
        cen = out - mean
        var = jnp.dot(cen * cen, ones, precision=SYS) * (1.0 / an)
        normed = cen * lax.rsqrt(var + A_GN_EPS) * gn_g + gn_b
        bonus = jnp.dot(r * k2 * r_k, ones, precision=SYS) * v
        return ((normed + bonus) * g,)

    flat = lambda t: t.reshape(1, -1)
    (ya,) = _rowwise(f"rwkv_post{li}", post, [out, r, k2, v, g],
                     [flat(q['rwkv_gn_g']), flat(q['rwkv_gn_b']), flat(q['rwkv_r_k'])], [ones_a])

    cw = q['gdn_conv_w']

    def conv_l2(x, w):
        y = _silu(_causal_conv(x, w))
        return (y * lax.rsqrt(jnp.sum(y * y, axis=1, keepdims=True) + L2_EPS),)

    if bn == LANES:
        (gq,) = _colwise(f"gdn_conv_q{li}", conv_l2, [bq], [cw[:, :bw]])
        (gk,) = _colwise(f"gdn_conv_k{li}", conv_l2, [bk], [cw[:, bw:2 * bw]])
    else:
        raise NotImplementedError("gated DeltaNet head width must equal the lane count")
    (gv,) = _colwise(f"gdn_conv_v{li}", lambda x, w: (_silu(_causal_conv(x, w)),), [bv], [cw[:, 2 * bw:]])

    def gates(beta_raw, alpha_raw, a_log, dt_bias):
        return jax.nn.sigmoid(beta_raw), -jnp.exp(a_log) * _softplus(alpha_raw + dt_bias)

    beta, lg = _rowwise(f"gdn_gates{li}", gates, [beta_raw, alpha_raw],
                        [_pad_lanes(q['gdn_A_log'][None], pw[10]), _pad_lanes(q['gdn_dt_bias'][None], pw[11])])
    gl = min(GDN_CHUNK, T)
    col = lambda t: jnp.transpose(t[:, :bh]).reshape(bh, T // gl, 1, gl)
    hmb = lambda t: _heads_major(t, bh)
    o = _chunk_scan(f"gdn_scan{li}", _gdn_chunk, [hmb(gq), hmb(gk), hmb(gv), col(beta), col(lg)],
                    min(bh, SCAN_HEADS), gl, (bn, bn), bn)
    o = _tokens_major(o)
    ones_b = _group_ones(bw, bn)

    def gdn_post(o, z, ng, ones):
        ms = jnp.dot(o * o, ones, precision=SYS) * (1.0 / bn)
        return (o * lax.rsqrt(ms + RMS_EPS) * ng * _silu(z),)

    (yb,) = _rowwise(f"gdn_post{li}", gdn_post, [o, bz], [jnp.tile(q['gdn_norm_g'][None], (1, bh))], [ones_b])
    return _matmul(jnp.concatenate([ya, yb], axis=1), q['even_w_out'], wz['even_w_out'], f"even_out{li}")


def _odd_mixer(x, q, wz, li):
    T = x.shape[0]
    ch = q['mamba_dt_bias'].shape[0]
    cwid = q['mamba_norm_g'].shape[0]
    cp = cwid // ch
    xbc_w = q['mamba_conv_w'].shape[1]
    cn = (xbc_w - cwid) // (2 * C_GROUPS)
    dw = q['lru_lambda'].shape[0]
    widths = [cwid, xbc_w, ch, dw, dw]
    offs, total = _aligned_layout(widths)
    pw = [_rup(w, LANES) for w in widths]
    hcols = _matmul(x, _pad_cols(q['odd_w_in'], offs, widths, total), wz['odd_w_in'], f"odd_in{li}")
    z, xbc, dt_raw, y_br, x_br = _split_cols(hcols, offs, pw)

    (xbc_c,) = _colwise(f"mamba_conv{li}", lambda x, w, b: (_silu(_causal_conv(x, w) + b),), [xbc],
                        [q['mamba_conv_w'], q['mamba_conv_b'][None]])
    gn = C_GROUPS * cn
    xs, bm, cm = _split_cols(xbc_c, [0, cwid, cwid + gn], [cwid, gn, gn])

    def dts(dt_raw, dt_bias, a_log):
        dt = _softplus(dt_raw + dt_bias)
        return dt, dt * (-jnp.exp(a_log))

    dt, aa = _rowwise(f"mamba_dt{li}", dts, [dt_raw],
                      [_pad_lanes(q['mamba_dt_bias'][None], pw[2]), _pad_lanes(q['mamba_A_log'][None], pw[2])])
    sl = min(SSD_CHUNK, T)
    col = lambda t: jnp.transpose(t[:, :ch]).reshape(ch, T // sl, 1, sl)
    y = _chunk_scan(f"ssd_scan{li}", _ssd_chunk,
                    [_heads_major(xs, ch), col(dt), col(aa), _heads_major(bm, C_GROUPS), _heads_major(cm, C_GROUPS)],
                    ch // C_GROUPS, sl, (cn, cp), cp)
    y = _tokens_major(y)
    gsz = cwid // C_GROUPS

    def mamba_post(y, xs, z, d, ng):
        yy = (y + xs * d) * _silu(z)
        lane = lax.broadcasted_iota(jnp.int32, yy.shape, 1)
        ms = jnp.zeros_like(yy)
        for gi in range(C_GROUPS):
            sel = (lane >= gi * gsz) & (lane < (gi + 1) * gsz)
            ms = jnp.where(sel, jnp.sum(jnp.where(sel, yy * yy, 0.0), axis=1, keepdims=True) * (1.0 / gsz), ms)
        return (yy * lax.rsqrt(ms + RMS_EPS) * ng,)

    (yc,) = _rowwise(f"mamba_post{li}", mamba_post, [y, xs, z],
                     [jnp.repeat(q['mamba_D'], cp)[None], q['mamba_norm_g'][None]])

    (xc,) = _colwise(f"lru_conv{li}", lambda x, w, b: (_causal_conv(x, w) + b,), [x_br],
                     [q['lru_conv_w'], q['lru_conv_b'][None]])
    ra = _bdmm(xc, q['lru_wa'], f"lru_wa{li}")
    ia = _bdmm(xc, q['lru_wx'], f"lru_wx{li}")

    def lru_pre(ra, ia, xc, ba, bx, lam):
        r = jax.nn.sigmoid(ra + ba)
        i = jax.nn.sigmoid(ia + bx)
        log_a = LRU_C * r * (-_softplus(-lam))
        t = 2.0 * log_a
        series = t * (1.0 + t * (0.5 + t * (1.0 / 6.0 + t * (1.0 / 24.0 + t * (1.0 / 120.0 + t * (1.0 / 720.0))))))
        expm1 = jnp.where(t > -0.2, series, jnp.exp(t) - 1.0)
        return jnp.exp(log_a), jnp.sqrt(-expm1) * (i * xc)

    a, u = _rowwise(f"lru_pre{li}", lru_pre, [ra, ia, xc], [q['lru_ba'][None], q['lru_bx'][None], q['lru_lambda'][None]])
    h = _lru_scan(a, u)
    (yd,) = _rowwise(f"lru_post{li}", lambda h, y: (h * jax.nn.gelu(y),), [h, y_br])
    return _matmul(jnp.concatenate([yc, yd], axis=1), q['odd_w_out'], wz['odd_w_out'], f"odd_out{li}")


def _forward(x, wz, sp, p, w16, depth):
    alpha = (2.0 * depth) ** 0.25
    for i in range(depth):
        j = i // 2
        even = i % 2 == 0
        names = [n for n in WEIGHTS if n.startswith(('rwkv_', 'gdn_', 'even_') if even else ('mamba_', 'lru_', 'odd_'))]
        q = {n: (w16[n][j] if n in MATRICES else sp[n][j]) for n in names}
        wzl = {n: wz[f"{n}.{j}"] for n in names if n in MATRICES}
        y = (_even_mixer if even else _odd_mixer)(x, q, wzl, i)

        def ln_res(x, y, g, b):
            return (_layer_norm_rows(alpha * x + y, g, b, LN_EPS),)

        (h,) = _rowwise(f"ln1_{i}", ln_res, [x, y], [sp['ln1_g'][i][None], sp['ln1_b'][i][None]])
        dff = w16['ffn_up'].shape[2] // 2
        gate = _matmul(h, w16['ffn_up'][i][:, :dff], wz[f"ffn_up.{i}"], f"ffn_gate{i}")
        val = _matmul(h, w16['ffn_up'][i][:, dff:], wz[f"ffn_up_val.{i}"], f"ffn_val{i}")
        cw, cb = sp['ffn_conv_w'][i], sp['ffn_conv_b'][i][None]

        def ffn_act(gate, val, wg, wv, bg, bv):
            return (_silu(_causal_conv(gate, wg) + bg) * (_causal_conv(val, wv) + bv),)

        (act,) = _colwise(f"ffn_act{i}", ffn_act, [gate, val], [cw[:, :dff], cw[:, dff:], cb[:, :dff], cb[:, dff:]])
        f = _matmul(act, w16['ffn_down'][i], wz[f"ffn_down.{i}"], f"ffn_down{i}")
        (h2,) = _rowwise(f"ln2_{i}", ln_res, [h, f], [sp['ln2_g'][i][None], sp['ln2_b'][i][None]])
        e0 = _matmul(p[i], w16['ple_proj'][i], wz[f"ple_proj.{i}"], f"ple_proj{i}")
        gl = _matmul(h2, w16['ple_gate_w'][i], wz[f"ple_gate_w.{i}"], f"ple_gate{i}")

        def ple(h2, gl, e0, gb, ng):
            e = e0 * lax.rsqrt(jnp.mean(e0 * e0, axis=1, keepdims=True) + RMS_EPS) * ng
            return (h2 + jax.nn.sigmoid(gl + gb) * e,)

        (x,) = _rowwise(f"ple{i}", ple, [h2, gl, e0], [sp['ple_gate_b'][i][None], sp['ple_norm_g'][i][None]])
    return x


def _loss_head(y, target):
    T, D = y.shape
    tile = _pick(T, (256, 128, 64, 32, 16, 8))

    def body(y_ref, t_ref, dy_ref, l_ref):
        err = y_ref[...] - t_ref[...]
        dy_ref[...] = err * (1.0 / D)

        @pl.when(pl.program_id(0) == 0)
        def _():
            l_ref[...] = jnp.zeros_like(l_ref)

        l_ref[...] += jnp.sum(jnp.sum(err * err, axis=1, keepdims=True), axis=0, keepdims=True) * (0.5 / D) + jnp.zeros_like(l_ref)

    spec = pl.BlockSpec((tile, D), lambda i: (i, 0))
    dy, l = pl.pallas_call(body, grid=(T // tile,), in_specs=[spec, spec],
                           out_specs=[spec, pl.BlockSpec((8, LANES), lambda i: (0, 0))],
                           out_shape=[SDS((T, D), F32), SDS((8, LANES), F32)], name="loss_head",
                           compiler_params=_cparams(("arbitrary",)))(y, target)
    return l[0, 0], dy


def _my_index():
    return 4 * lax.axis_index("x") + 2 * lax.axis_index("y") + lax.axis_index("c")


def _hbm_specs(n):
    return [pl.BlockSpec(memory_space=pl.ANY)] * n


def _all_gather(blocks, name):
    blocks = tuple(blocks)
    n = len(blocks)
    half = [b.shape[0] // 2 for b in blocks]

    def body(*refs):
        ins, outs = refs[:n], refs[n:2 * n]
        send_sems, recv_sems, local_sems = refs[2 * n:]
        x, y, c = lax.axis_index("x"), lax.axis_index("y"), lax.axis_index("c")
        me, sibling, other = (x, y, c), (x, y, 1 - c), 1 - c
        xn, yn, dg = (1 - x, y), (x, 1 - y), (1 - x, 1 - y)

        def slot(i, px, py, pc, h=None):
            ref = outs[i].at[4 * px + 2 * py + pc]
            return ref if h is None else ref.at[pl.ds(h * half[i], half[i])]

        def copy(i, k, blk, to, h=None, src=None):
            dst = slot(i, *blk, h)
            return pltpu.make_async_remote_copy(
                src_ref=dst if src is None else src, dst_ref=dst, send_sem=send_sems.at[9 * i + k],
                recv_sem=recv_sems.at[9 * i + k], device_id=to, device_id_type=MESH)

        mine = [pltpu.make_async_copy(ins[i], slot(i, *me), local_sems.at[i]) for i in range(n)]
        sent = []
        for i in range(n):
            sent += [copy(i, 1, me, (*xn, c), src=ins[i]), copy(i, 2, me, (*yn, c), src=ins[i])]
        sent += [copy(i, 0, me, sibling, src=ins[i]) for i in range(n)]
        for cp in mine + sent:
            cp.start()

        def after(i, k_in, blk, h_in, forwards):
            copy(i, k_in, blk, me, h_in).wait_recv()
            for k_out, to, h_out in forwards:
                sent.append(copy(i, k_out, blk, to, h_out))
                sent[-1].start()

        for i in range(n):
            after(i, 1, (*xn, c), None, [(3, (*yn, c), 0), (5, sibling, None)])
        for i in range(n):
            after(i, 2, (*yn, c), None, [(4, (*xn, c), 1), (6, sibling, None)])
        for i in range(n):
            after(i, 3, (*dg, c), 0, [(7, sibling, 0)])
        for i in range(n):
            after(i, 4, (*dg, c), 1, [(8, sibling, 1)])
        for i in range(n):
            copy(i, 0, sibling, me).wait_recv()
            copy(i, 5, (*xn, other), me).wait_recv()
            copy(i, 6, (*yn, other), me).wait_recv()
            copy(i, 7, (*dg, other), me, 0).wait_recv()
            copy(i, 8, (*dg, other), me, 1).wait_recv()
        for cp in sent:
            cp.wait_send()
        for cp in mine:
            cp.wait()

    return pl.pallas_call(
        body, out_shape=[SDS((N_DEV,) + b.shape, b.dtype) for b in blocks], in_specs=_hbm_specs(n), out_specs=_hbm_specs(n),
        scratch_shapes=[pltpu.SemaphoreType.DMA((9 * n,)), pltpu.SemaphoreType.DMA((9 * n,)), pltpu.SemaphoreType.DMA((n,))],
        name=name)(*blocks)


def _sibling_exchange(parts, name):
    parts = tuple(parts)
    n = len(parts)

    def body(*refs):
        ins, outs = refs[:n], refs[n:2 * n]
        send_sems, recv_sems = refs[2 * n:]
        x, y, c = lax.axis_index("x"), lax.axis_index("y"), lax.axis_index("c")
        copies = [pltpu.make_async_remote_copy(
            src_ref=ins[i].at[q, 1 - c], dst_ref=outs[i].at[q], send_sem=send_sems.at[4 * i + q],
            recv_sem=recv_sems.at[4 * i + q], device_id=(x, y, 1 - c), device_id_type=MESH)
            for i in range(n) for q in range(4)]
        for cp in copies:
            cp.start()
        for cp in copies:
            cp.wait_recv()
        for cp in copies:
            cp.wait_send()

    return pl.pallas_call(
        body, out_shape=[SDS((4,) + p.shape[2:], p.dtype) for p in parts], in_specs=_hbm_specs(n), out_specs=_hbm_specs(n),
        scratch_shapes=[pltpu.SemaphoreType.DMA((4 * n,)), pltpu.SemaphoreType.DMA((4 * n,))], name=name)(*parts)


def _chip_exchange(sends, name):
    sends = tuple(sends)
    n = len(sends)

    def body(*refs):
        ins, outs = refs[:n], refs[n:2 * n]
        send_sems, recv_sems = refs[2 * n:]
        x, y, c = lax.axis_index("x"), lax.axis_index("y"), lax.axis_index("c")
        chips = [(1 - x, y), (x, 1 - y), (1 - x, 1 - y)]
        copies = [pltpu.make_async_remote_copy(
            src_ref=ins[i].at[k], dst_ref=outs[i].at[k], send_sem=send_sems.at[3 * i + k],
            recv_sem=recv_sems.at[3 * i + k], device_id=(*chip, c), device_id_type=MESH)
            for k, chip in enumerate(chips) for i in range(n)]
        for cp in copies:
            cp.start()
        for cp in copies:
            cp.wait_recv()
        for cp in copies:
            cp.wait_send()

    return pl.pallas_call(
        body, out_shape=[SDS(s.shape, s.dtype) for s in sends], in_specs=_hbm_specs(n), out_specs=_hbm_specs(n),
        scratch_shapes=[pltpu.SemaphoreType.DMA((3 * n,)), pltpu.SemaphoreType.DMA((3 * n,))], name=name)(*sends)


def _place_ids():
    x, y, c = lax.axis_index("x"), lax.axis_index("y"), lax.axis_index("c")
    return jnp.stack([c, 2 * x + y, 2 * (1 - x) + y, 2 * x + (1 - y), 2 * (1 - x) + (1 - y)]).astype(jnp.int32)


def _row_tile(rows, cols):
    best = None
    for t in range(16, rows + 1, 16):
        if rows % t == 0 and t * cols <= 256 * 1024:
            best = t
    return best or rows


def _chip_partials(part, recv_a, ids, dtype, name):
    _, _, R, C = part.shape
    tr = _row_tile(R, C)

    def body(ids_ref, p_ref, a_ref, o_ref):
        o_ref[...] = (p_ref[...] + a_ref[...]).astype(dtype)

    return pl.pallas_call(
        body, out_shape=SDS((3, R, C), dtype),
        grid_spec=pltpu.PrefetchScalarGridSpec(
            num_scalar_prefetch=1, grid=(3, R // tr),
            in_specs=[pl.BlockSpec((None, None, tr, C), lambda s, i, ids: (ids[2 + s], ids[0], i, 0)),
                      pl.BlockSpec((None, tr, C), lambda s, i, ids: (ids[2 + s], i, 0))],
            out_specs=pl.BlockSpec((None, tr, C), lambda s, i, ids: (s, i, 0))),
        name=name, compiler_params=_cparams(("parallel", "parallel")))(ids, part, recv_a)


def _adamw(part, recv_a, recv_c, ids, w, m, v, name):
    R, C = w.shape
    tr = _row_tile(R, C)
    c1 = 1.0 / (1.0 - ADAM_B1 ** ADAM_STEP)
    c2 = 1.0 / (1.0 - ADAM_B2 ** ADAM_STEP)

    def body(ids_ref, p_ref, a_ref, c_ref, w_ref, m_ref, v_ref, g_ref, d_ref, nm_ref, nv_ref):
        g = p_ref[...] + a_ref[...]
        for k in range(3):
            g = g + c_ref[k].astype(F32)
        nm = ADAM_B1 * m_ref[...] + (1.0 - ADAM_B1) * g
        nv = ADAM_B2 * v_ref[...] + (1.0 - ADAM_B2) * jnp.square(g)
        g_ref[...] = g
        nm_ref[...] = nm
        nv_ref[...] = nv
        d_ref[...] = -ADAM_LR * ((nm * c1) / (jnp.sqrt(nv * c2) + ADAM_EPS) + ADAM_WD * w_ref[...])

    spec = pl.BlockSpec((tr, C), lambda i, ids: (i, 0))
    return pl.pallas_call(
        body, out_shape=[SDS((R, C), F32)] * 4,
        grid_spec=pltpu.PrefetchScalarGridSpec(
            num_scalar_prefetch=1, grid=(R // tr,),
            in_specs=[pl.BlockSpec((None, None, tr, C), lambda i, ids: (ids[1], ids[0], i, 0)),
                      pl.BlockSpec((None, tr, C), lambda i, ids: (ids[1], i, 0)),
                      pl.BlockSpec((3, tr, C), lambda i, ids: (0, i, 0)), spec, spec, spec],
            out_specs=[spec] * 4),
        name=name, compiler_params=_cparams(("parallel",)))(ids, part, recv_a, recv_c, w, m, v)


def _to_flat(vecs, quantum):
    flat = jnp.concatenate([v.reshape(-1) for v in vecs])
    n = _rup(flat.shape[0], quantum * FLAT_COLS)
    return jnp.pad(flat, (0, n - flat.shape[0])).reshape(n // FLAT_COLS, FLAT_COLS)


def _gathered_to_full(g, names, blocks):
    flat = g.reshape(N_DEV, -1)
    out, pos = {}, 0
    for n in names:
        shp = blocks[n]
        size = math.prod(shp)
        out[n] = _blocks_to_full(flat[:, pos:pos + size].reshape((N_DEV,) + shp), SHARD_AXIS[n])
        pos += size
    return out


def _blocks_to_full(g, ax):
    shp = g.shape[1:]
    return jnp.moveaxis(g, 0, ax).reshape(shp[:ax] + (N_DEV * shp[ax],) + shp[ax + 1:])


def _full_to_blocks(g, ax, ndev=N_DEV):
    shp = g.shape
    t = g.reshape(shp[:ax] + (ndev, shp[ax] // ndev) + shp[ax + 1:])
    return jnp.moveaxis(t, ax, 0)


def kernel(x, p, ln1_g, ln1_b, ln2_g, ln2_b, ffn_up, ffn_conv_w, ffn_conv_b, ffn_down, ple_proj, ple_norm_g, ple_gate_w, ple_gate_b, even_w_in, even_w_out, rwkv_mu, rwkv_w0, rwkv_w2, rwkv_a0, rwkv_a2, rwkv_g2, rwkv_k_k, rwkv_k_a, rwkv_r_k, rwkv_gn_g, rwkv_gn_b, gdn_conv_w, gdn_A_log, gdn_dt_bias, gdn_norm_g, odd_w_in, odd_w_out, mamba_conv_w, mamba_conv_b, mamba_dt_bias, mamba_A_log, mamba_D, mamba_norm_g, lru_conv_w, lru_conv_b, lru_wa, lru_ba, lru_wx, lru_bx, lru_lambda, loss_target, m_ln1_g, m_ln1_b, m_ln2_g, m_ln2_b, m_ffn_up, m_ffn_conv_w, m_ffn_conv_b, m_ffn_down, m_ple_proj, m_ple_norm_g, m_ple_gate_w, m_ple_gate_b, m_even_w_in, m_even_w_out, m_rwkv_mu, m_rwkv_w0, m_rwkv_w2, m_rwkv_a0, m_rwkv_a2, m_rwkv_g2, m_rwkv_k_k, m_rwkv_k_a, m_rwkv_r_k, m_rwkv_gn_g, m_rwkv_gn_b, m_gdn_conv_w, m_gdn_A_log, m_gdn_dt_bias, m_gdn_norm_g, m_odd_w_in, m_odd_w_out, m_mamba_conv_w, m_mamba_conv_b, m_mamba_dt_bias, m_mamba_A_log, m_mamba_D, m_mamba_norm_g, m_lru_conv_w, m_lru_conv_b, m_lru_wa, m_lru_ba, m_lru_wx, m_lru_bx, m_lru_lambda, v_ln1_g, v_ln1_b, v_ln2_g, v_ln2_b, v_ffn_up, v_ffn_conv_w, v_ffn_conv_b, v_ffn_down, v_ple_proj, v_ple_norm_g, v_ple_gate_w, v_ple_gate_b, v_even_w_in, v_even_w_out, v_rwkv_mu, v_rwkv_w0, v_rwkv_w2, v_rwkv_a0, v_rwkv_a2, v_rwkv_g2, v_rwkv_k_k, v_rwkv_k_a, v_rwkv_r_k, v_rwkv_gn_g, v_rwkv_gn_b, v_gdn_conv_w, v_gdn_A_log, v_gdn_dt_bias, v_gdn_norm_g, v_odd_w_in, v_odd_w_out, v_mamba_conv_w, v_mamba_conv_b, v_mamba_dt_bias, v_mamba_A_log, v_mamba_D, v_mamba_norm_g, v_lru_conv_w, v_lru_conv_b, v_lru_wa, v_lru_ba, v_lru_wx, v_lru_bx, v_lru_lambda):
    args = locals()
    w = {n: args[n] for n in WEIGHTS}
    m = {n: args["m_" + n] for n in WEIGHTS}
    v = {n: args["v_" + n] for n in WEIGHTS}
    depth = ln1_g.shape[0]
    me = _my_index()
    blocks = {n: w[n].shape for n in WEIGHTS}

    as_rows = lambda t: t.reshape(-1, t.shape[-1])
    small = _to_flat([w[n] for n in SMALL_SHARDED], 16)
    gathered = _all_gather([as_rows(w[n].astype(BF16)) for n in MATRICES] + [small], "gather_params")
    w16 = {n: _blocks_to_full(g.reshape((N_DEV,) + blocks[n]), SHARD_AXIS[n]) for n, g in zip(MATRICES, gathered)}
    sp = _gathered_to_full(gathered[-1], SMALL_SHARDED, blocks)
    sp.update({n: w[n] for n in REPLICATED})

    lay = _matrix_layouts(w16, sp)
    wz = {k: jnp.zeros(shape, F32) for k, shape in lay['padded'].items()}
    y, vjp = jax.vjp(lambda x_, wz_, sp_: _forward(x_, wz_, sp_, p[:, 0], w16, depth), x[0], wz, sp)
    loss_local, dy = _loss_head(y, loss_target[0])
    dx, dwz, dsp = vjp(dy)
    loss = lax.psum(loss_local, ("x", "y", "c"))
    gfull = dict(dsp)
    gblocks = {}
    for n in MATRICES:
        layers = range(w16[n].shape[0])
        if n == 'ffn_up':
            halves = [jnp.stack([dwz[f"{k}.{j}"] for j in layers]) for k in ("ffn_up", "ffn_up_val")]
            gblocks[n] = jnp.concatenate([_full_to_blocks(h, SHARD_AXIS[n], N_DEV // 2) for h in halves], axis=0)
        else:
            gblocks[n] = _full_to_blocks(jnp.stack([lay['unpad'][n](dwz[f"{n}.{j}"]) for j in layers]), SHARD_AXIS[n])

    rep_flat = jnp.concatenate([gfull[n].reshape(-1) for n in REPLICATED])
    rep_n = rep_flat.shape[0]
    piece = _rup(rep_n, N_DEV * LANES) // N_DEV
    rep_pad = lambda t: jnp.pad(t, (0, N_DEV * piece - rep_n))
    small_parts = jnp.concatenate([_full_to_blocks(gfull[n], SHARD_AXIS[n]).reshape(N_DEV, -1) for n in SMALL_SHARDED]
                                  + [rep_pad(rep_flat).reshape(N_DEV, piece)], axis=1)
    n_flat = small_parts.shape[1]
    n_pad = _rup(n_flat, 16 * FLAT_COLS)
    small_parts = jnp.pad(small_parts, ((0, 0), (0, n_pad - n_flat)))

    def my_small(d):
        rep = rep_pad(jnp.concatenate([d[n].reshape(-1) for n in REPLICATED]))
        mine = lax.dynamic_slice(rep, (me * piece,), (piece,))
        flat = jnp.concatenate([d[n].reshape(-1) for n in SMALL_SHARDED] + [mine])
        return jnp.pad(flat, (0, n_pad - n_flat)).reshape(n_pad // FLAT_COLS, FLAT_COLS)

    by_chip = lambda t, cols: t.reshape(4, 2, -1, cols)
    parts = [by_chip(gblocks[n], blocks[n][-1]) for n in MATRICES]
    parts.append(by_chip(small_parts, FLAT_COLS))
    wire = [BF16] * len(MATRICES) + [F32]
    tags = MATRICES + ["small"]
    ids = _place_ids()
    recv_a = _sibling_exchange(parts, "reduce_sibling")
    recv_c = _chip_exchange([_chip_partials(pt, ra, ids, dt, f"chip_partials_{t}")
                             for pt, ra, dt, t in zip(parts, recv_a, wire, tags)], "reduce_chips")
    mine = [(as_rows(w[n]), as_rows(m[n]), as_rows(v[n])) for n in MATRICES] + [(my_small(w), my_small(m), my_small(v))]
    results = [_adamw(pt, ra, rc, ids, *wmv, f"adamw_{t}")
               for pt, ra, rc, wmv, t in zip(parts, recv_a, recv_c, mine, tags)]
    small_res = [r.reshape(-1) for r in results[-1]]
    rep_res = jnp.stack([r[n_flat - piece:n_flat] for r in small_res])
    rep_rows = _rup(4 * piece, 8 * FLAT_COLS) // FLAT_COLS
    rep_blk = jnp.pad(rep_res.reshape(-1), (0, rep_rows * FLAT_COLS - 4 * piece)).reshape(rep_rows, FLAT_COLS)
    (rep_all,) = _all_gather([rep_blk], "gather_replicated")
    rep_all = rep_all.reshape(N_DEV, -1)[:, :4 * piece]
    rep_all = jnp.transpose(rep_all.reshape(N_DEV, 4, piece), (1, 0, 2)).reshape(4, N_DEV * piece)

    outs = [{}, {}, {}, {}]
    for k in range(4):
        for n, res in zip(MATRICES, results):
            outs[k][n] = res[k].reshape(blocks[n])
        pos = 0
        for n in SMALL_SHARDED:
            size = math.prod(blocks[n])
            outs[k][n] = small_res[k][pos:pos + size].reshape(blocks[n])
            pos += size
        pos = 0
        for n in REPLICATED:
            size = math.prod(blocks[n])
            outs[k][n] = rep_all[k, pos:pos + size].reshape(blocks[n])
            pos += size
    return (loss, dx[None], *[outs[0][n] for n in WEIGHTS], *[outs[1][n] for n in WEIGHTS],
            *[outs[2][n] for n in WEIGHTS], *[outs[3][n] for n in WEIGHTS])


def _matrix_layouts(w16, sp):
    padded, unpad = {}, {}
    ident = lambda g: g
    for n in ('ffn_down', 'ple_proj', 'ple_gate_w', 'even_w_out', 'odd_w_out'):
        for j in range(w16[n].shape[0]):
            padded[f"{n}.{j}"] = w16[n].shape[1:]
        unpad[n] = ident
    for j in range(w16['ffn_up'].shape[0]):
        half = (w16['ffn_up'].shape[1], w16['ffn_up'].shape[2] // 2)
        padded[f"ffn_up.{j}"] = padded[f"ffn_up_val.{j}"] = half
    for n in ('rwkv_w2', 'rwkv_a2', 'rwkv_g2'):
        rows, cols = w16[n].shape[1:]
        for j in range(w16[n].shape[0]):
            padded[f"{n}.{j}"] = (_rup(rows, LANES), cols)
        unpad[n] = functools.partial(lambda g, rows: g[:rows], rows=rows)
    ah, an = sp['rwkv_r_k'].shape[1:]
    bh, bn = sp['gdn_A_log'].shape[1], sp['gdn_norm_g'].shape[1]
    ew = [ah * an] * 3 + [w16['rwkv_w2'].shape[1], w16['rwkv_a2'].shape[1], w16['rwkv_g2'].shape[1]] + [bh * bn] * 4 + [bh, bh]
    cwid, ch = sp['mamba_norm_g'].shape[1], sp['mamba_dt_bias'].shape[1]
    dw = sp['lru_lambda'].shape[1]
    ow = [cwid, sp['mamba_conv_w'].shape[2], ch, dw, dw]
    for n, widths in (('even_w_in', ew), ('odd_w_in', ow)):
        offs, total = _aligned_layout(widths)
        for j in range(w16[n].shape[0]):
            padded[f"{n}.{j}"] = (w16[n].shape[1], total)
        unpad[n] = functools.partial(_unpad_cols, offs=offs, widths=widths)
    return {'padded': padded, 'unpad': unpad}
```

```python
import functools
import math

import jax
import jax.numpy as jnp
from jax import lax
from jax.experimental import pallas as pl
from jax.experimental.pallas import tpu as pltpu

F32 = jnp.float32
BF16 = jnp.bfloat16
HI = lax.Precision.HIGHEST
SDS = jax.ShapeDtypeStruct
MESH = pl.DeviceIdType.MESH

LANES = 128
VMEM_LIMIT = 56 * 1024 * 1024
N_DEV = 8
FLAT_COLS = 1024
MM_VMEM = 40 * 1024 * 1024

LN_EPS = 1e-5
RMS_EPS = 1e-6
L2_EPS = 1e-6
A_GN_EPS = 64e-5
LRU_C = 8.0
C_GROUPS = 4
RWKV_CHUNK = 64
GDN_CHUNK = 64
SSD_CHUNK = 128
SCAN_HEADS = 16

ADAM_LR, ADAM_B1, ADAM_B2, ADAM_EPS, ADAM_WD, ADAM_STEP = 0.001, 0.9, 0.999, 1e-08, 0.01, 10

WEIGHTS = ['ln1_g', 'ln1_b', 'ln2_g', 'ln2_b', 'ffn_up', 'ffn_conv_w', 'ffn_conv_b', 'ffn_down', 'ple_proj',
           'ple_norm_g', 'ple_gate_w', 'ple_gate_b', 'even_w_in', 'even_w_out', 'rwkv_mu', 'rwkv_w0', 'rwkv_w2',
           'rwkv_a0', 'rwkv_a2', 'rwkv_g2', 'rwkv_k_k', 'rwkv_k_a', 'rwkv_r_k', 'rwkv_gn_g', 'rwkv_gn_b',
           'gdn_conv_w', 'gdn_A_log', 'gdn_dt_bias', 'gdn_norm_g', 'odd_w_in', 'odd_w_out', 'mamba_conv_w',
           'mamba_conv_b', 'mamba_dt_bias', 'mamba_A_log', 'mamba_D', 'mamba_norm_g', 'lru_conv_w', 'lru_conv_b',
           'lru_wa', 'lru_ba', 'lru_wx', 'lru_bx', 'lru_lambda']
SHARD_AXIS = {'ffn_up': 2, 'ffn_conv_w': 2, 'ffn_down': 1, 'ple_proj': 2, 'ple_gate_w': 1, 'even_w_in': 2,
              'even_w_out': 1, 'rwkv_w2': 2, 'rwkv_a2': 2, 'rwkv_g2': 2, 'gdn_conv_w': 2, 'odd_w_in': 2,
              'odd_w_out': 1, 'mamba_conv_w': 2, 'mamba_conv_b': 1, 'mamba_norm_g': 1, 'lru_conv_w': 2,
              'lru_conv_b': 1, 'lru_ba': 1, 'lru_bx': 1, 'lru_lambda': 1}
MATRICES = ['ffn_up', 'ffn_down', 'ple_proj', 'ple_gate_w', 'even_w_in', 'even_w_out', 'rwkv_w2', 'rwkv_a2',
            'rwkv_g2', 'odd_w_in', 'odd_w_out']
SMALL_SHARDED = [n for n in WEIGHTS if n in SHARD_AXIS and n not in MATRICES]
REPLICATED = [n for n in WEIGHTS if n not in SHARD_AXIS]


def _cparams(sem):
    return pltpu.CompilerParams(dimension_semantics=sem, vmem_limit_bytes=VMEM_LIMIT)


def _rup(n, m):
    return -(-n // m) * m


def _pick(n, cands):
    for c in cands:
        if n % c == 0:
            return c
    return n


_DIMS = {'nn': (((1,), (0,)), ((), ())), 'nt': (((1,), (1,)), ((), ())), 'tn': (((0,), (0,)), ((), ()))}


def _mm(a, b, mode, name):
    if mode == 'tn':
        K, M = a.shape
    else:
        M, K = a.shape
    N = b.shape[0] if mode == 'nt' else b.shape[1]
    tm = _pick(M, (1024, 512, 256, 128))
    tn = _pick(N, (1024, 512, 256, 128))
    room = MM_VMEM - 3 * tm * tn * 4
    per_k = 2 * (tm * a.dtype.itemsize + tn * b.dtype.itemsize)
    tk = max([t for t in range(LANES, K + 1, LANES) if K % t == 0 and t * per_k <= room] or [K])
    nk = K // tk

    def body(a_ref, b_ref, o_ref, acc_ref):
        k = pl.program_id(2)
        part = lax.dot_general(a_ref[...].astype(BF16), b_ref[...].astype(BF16), _DIMS[mode],
                               preferred_element_type=F32)

        @pl.when(k == 0)
        def _():
            acc_ref[...] = part

        @pl.when(k > 0)
        def _():
            acc_ref[...] += part

        @pl.when(k == nk - 1)
        def _():
            o_ref[...] = acc_ref[...]

    a_spec = pl.BlockSpec((tk, tm), lambda i, j, k: (k, i)) if mode == 'tn' else pl.BlockSpec((tm, tk), lambda i, j, k: (i, k))
    b_spec = pl.BlockSpec((tn, tk), lambda i, j, k: (j, k)) if mode == 'nt' else pl.BlockSpec((tk, tn), lambda i, j, k: (k, j))
    return pl.pallas_call(
        body, grid=(M // tm, N // tn, nk), in_specs=[a_spec, b_spec],
        out_specs=pl.BlockSpec((tm, tn), lambda i, j, k: (i, j)), out_shape=SDS((M, N), F32),
        scratch_shapes=[pltpu.VMEM((tm, tn), F32)], name=name,
        compiler_params=_cparams(("parallel", "parallel", "arbitrary")))(a, b)


def _matmul(x, w16, wz, name):
    @jax.custom_vjp
    def op(x, wz):
        return _mm(x.astype(BF16), w16, 'nn', name + "_f")

    def op_f(x, wz):
        x16 = x.astype(BF16)
        return _mm(x16, w16, 'nn', name + "_f"), x16

    def op_b(x16, g):
        g16 = g.astype(BF16)
        return _mm(g16, w16, 'nt', name + "_dx"), _mm(x16, g16, 'tn', name + "_dw")

    op.defvjp(op_f, op_b)
    return op(x, wz)


def _bdmm_call(a, b, mode, nb, name):
    T = a.shape[0]
    bd = a.shape[1] // nb
    tt = _pick(T, (512, 256, 128))
    nt = T // tt
    if mode == 'tn':
        def body(a_ref, b_ref, o_ref):
            part = lax.dot_general(a_ref[...].astype(BF16), b_ref[...].astype(BF16), _DIMS['tn'],
                                   preferred_element_type=F32)

            @pl.when(pl.program_id(1) == 0)
            def _():
                o_ref[0] = part

            @pl.when(pl.program_id(1) > 0)
            def _():
                o_ref[0] += part

        return pl.pallas_call(
            body, grid=(nb, nt),
            in_specs=[pl.BlockSpec((tt, bd), lambda n, t: (t, n)), pl.BlockSpec((tt, bd), lambda n, t: (t, n))],
            out_specs=pl.BlockSpec((1, bd, bd), lambda n, t: (n, 0, 0)), out_shape=SDS((nb, bd, bd), F32),
            name=name, compiler_params=_cparams(("parallel", "arbitrary")))(a, b)

    def body(a_ref, b_ref, o_ref):
        o_ref[...] = lax.dot_general(a_ref[...].astype(BF16), b_ref[0].astype(BF16), _DIMS[mode],
                                     preferred_element_type=F32)

    return pl.pallas_call(
        body, grid=(nb, nt),
        in_specs=[pl.BlockSpec((tt, bd), lambda n, t: (t, n)), pl.BlockSpec((1, bd, bd), lambda n, t: (n, 0, 0))],
        out_specs=pl.BlockSpec((tt, bd), lambda n, t: (t, n)), out_shape=SDS(a.shape, F32),
        name=name, compiler_params=_cparams(("parallel", "parallel")))(a, b)


def _bdmm(x, w, name):
    nb = w.shape[0]

    @jax.custom_vjp
    def op(x, w):
        return _bdmm_call(x, w, 'nn', nb, name + "_f")

    def op_f(x, w):
        return op(x, w), (x, w)

    def op_b(res, g):
        x, w = res
        return _bdmm_call(g, w, 'nt', nb, name + "_dx"), _bdmm_call(x, g, 'tn', nb, name + "_dw")

    op.defvjp(op_f, op_b)
    return op(x, w)


def _tile_op(name, fn, arrs, params, consts, by_rows, width=LANES):
    arrs, params, consts = tuple(arrs), tuple(params), tuple(consts)
    na, npar, nc = len(arrs), len(params), len(consts)
    T = arrs[0].shape[0]
    if by_rows:
        tile = _pick(T, (256, 128, 64, 32, 16, 8))
        grid = (T // tile,)
        arr_block = lambda a: (tile, a.shape[1])
        arr_spec = lambda a: pl.BlockSpec((tile, a.shape[1]), lambda i: (i, 0))
        par_block = lambda p: p.shape
        par_spec = lambda p: pl.BlockSpec(p.shape, lambda i: (0, 0))
    else:
        grid = (arrs[0].shape[1] // width,)
        arr_block = lambda a: (T, width)
        arr_spec = lambda a: pl.BlockSpec((T, width), lambda i: (0, i))
        par_block = lambda p: (p.shape[0], width)
        par_spec = lambda p: pl.BlockSpec((p.shape[0], width), lambda i: (0, i))
    const_spec = lambda c: pl.BlockSpec(c.shape, lambda i: (0,) * c.ndim)
    outs_sds = jax.eval_shape(fn, *[SDS(arr_block(a), F32) for a in arrs], *[SDS(par_block(p), F32) for p in params],
                              *[SDS(c.shape, c.dtype) for c in consts])
    out_widths = [o.shape[1] for o in outs_sds]
    nout = len(out_widths)
    if by_rows:
        out_shapes = [SDS((T, w), F32) for w in out_widths]
        out_specs = [pl.BlockSpec((tile, w), lambda i: (i, 0)) for w in out_widths]
    else:
        out_shapes = [SDS((T, grid[0] * w), F32) for w in out_widths]
        out_specs = [pl.BlockSpec((T, w), lambda i: (0, i)) for w in out_widths]

    def fwd_call(arrs, params):
        def body(*refs):
            outs = fn(*[r[...] for r in refs[:na + npar + nc]])
            for o_ref, o in zip(refs[na + npar + nc:], outs):
                o_ref[...] = o

        return pl.pallas_call(
            body, grid=grid, in_specs=[arr_spec(a) for a in arrs] + [par_spec(p) for p in params] + [const_spec(c) for c in consts],
            out_specs=out_specs, out_shape=out_shapes, name=name + "_f",
            compiler_params=_cparams(("parallel",)))(*arrs, *params, *consts)

    def bwd_call(arrs, params, cts):
        def body(*refs):
            ins = refs[:na + npar + nc + nout]
            outs = refs[na + npar + nc + nout:]
            av = [r[...] for r in ins[:na]]
            pv = [r[...] for r in ins[na:na + npar]]
            cv = [r[...] for r in ins[na + npar:na + npar + nc]]
            gv = [r[...] for r in ins[na + npar + nc:]]
            _, vjp = jax.vjp(lambda *t: fn(*t, *cv), *av, *pv)
            grads = vjp(tuple(gv))
            for o_ref, g in zip(outs[:na], grads[:na]):
                o_ref[...] = g
            if by_rows and npar:
                @pl.when(pl.program_id(0) == 0)
                def _():
                    for o_ref in outs[na:]:
                        o_ref[...] = jnp.zeros_like(o_ref)

                for o_ref, g in zip(outs[na:], grads[na:]):
                    o_ref[...] += g
            else:
                for o_ref, g in zip(outs[na:], grads[na:]):
                    o_ref[...] = g

        return pl.pallas_call(
            body, grid=grid,
            in_specs=[arr_spec(a) for a in arrs] + [par_spec(p) for p in params] + [const_spec(c) for c in consts] + out_specs,
            out_specs=[arr_spec(a) for a in arrs] + [par_spec(p) for p in params],
            out_shape=[SDS(a.shape, F32) for a in arrs] + [SDS(p.shape, F32) for p in params], name=name + "_b",
            compiler_params=_cparams(("arbitrary",) if by_rows else ("parallel",)))(*arrs, *params, *consts, *cts)

    @jax.custom_vjp
    def op(arrs, params):
        return tuple(fwd_call(arrs, params))

    def op_f(arrs, params):
        return op(arrs, params), (arrs, params)

    def op_b(res, cts):
        arrs, params = res
        g = bwd_call(arrs, params, cts)
        return tuple(g[:na]), tuple(g[na:])

    op.defvjp(op_f, op_b)
    return op(arrs, params)


def _rowwise(name, fn, arrs, params=(), consts=()):
    return _tile_op(name, fn, arrs, params, consts, True)


def _colwise(name, fn, arrs, params=(), width=LANES):
    return _tile_op(name, fn, arrs, params, (), False, width)


@functools.partial(jax.custom_vjp, nondiff_argnums=(1,))
def _shift(x, k):
    rows = lax.broadcasted_iota(jnp.int32, x.shape, 0)
    return jnp.where(rows >= k, pltpu.roll(x, k, 0), 0.0)


def _shift_f(x, k):
    return _shift(x, k), None


def _shift_b(k, _, g):
    n = g.shape[0]
    rows = lax.broadcasted_iota(jnp.int32, g.shape, 0)
    return (jnp.where(rows < n - k, pltpu.roll(g, n - k, 0), 0.0),)


_shift.defvjp(_shift_f, _shift_b)


def _causal_conv(x, w):
    K = w.shape[0]
    y = x * w[K - 1:K, :]
    for j in range(K - 1):
        y = y + _shift(x, K - 1 - j) * w[j:j + 1, :]
    return y


def _silu(x):
    return x * jax.nn.sigmoid(x)


def _softplus(x):
    return jnp.maximum(x, 0.0) + jnp.log1p(jnp.exp(-jnp.abs(x)))


def _split_cols(h, offs, widths):
    @jax.custom_vjp
    def op(h):
        return tuple(h[:, o:o + w] for o, w in zip(offs, widths))

    def op_f(h):
        return op(h), None

    def op_b(_, cts):
        parts, pos = [], 0
        T = cts[0].shape[0]
        for o, w, c in zip(offs, widths, cts):
            if o > pos:
                parts.append(jnp.zeros((T, o - pos), F32))
            parts.append(c)
            pos = o + w
        if pos < h.shape[1]:
            parts.append(jnp.zeros((T, h.shape[1] - pos), F32))
        return (jnp.concatenate(parts, axis=1),)

    op.defvjp(op_f, op_b)
    return op(h)


def _group_ones(width, group):
    g = jnp.arange(width) // group
    return (g[:, None] == g[None, :]).astype(F32)


def _layer_norm_rows(x, g, b, eps):
    mu = jnp.mean(x, axis=1, keepdims=True)
    var = jnp.mean(jnp.square(x - mu), axis=1, keepdims=True)
    return (x - mu) * lax.rsqrt(var + eps) * g + b


def _tri(L, strict=False):
    i = lax.broadcasted_iota(jnp.int32, (L, L), 0)
    j = lax.broadcasted_iota(jnp.int32, (L, L), 1)
    return (i > j) if strict else (i >= j)


def _cumsum_rows(x):
    H, L, _ = x.shape
    tri = jnp.broadcast_to(_tri(L).astype(F32)[None], (H, L, L))
    return jnp.einsum('hls,hsn->hln', tri, x, precision=HI)


def _col_to_row(c):
    L = c.shape[1]
    return jnp.sum(c * _tri_eye(L)[None], axis=1, keepdims=True)


def _row_to_col(r):
    N = r.shape[2]
    return jnp.sum(r * _tri_eye(N)[None], axis=2, keepdims=True)


def _scalar_col(t):
    return _row_to_col(t.reshape(t.shape[0], 1, t.shape[3]))


def _tri_eye(L):
    i = lax.broadcasted_iota(jnp.int32, (L, L), 0)
    j = lax.broadcasted_iota(jnp.int32, (L, L), 1)
    return (i == j).astype(F32)


def _unit_lower_inverse(n_strict):
    L = n_strict.shape[1]
    inv = _tri_eye(L)[None] + n_strict
    x = n_strict
    p = 2
    while p < L:
        x = jnp.einsum('hij,hjk->hik', x, x)
        inv = inv + jnp.einsum('hij,hjk->hik', inv, x)
        p *= 2
    return inv


def _rwkv_chunk(r, lw, k, v, a, b, h0):
    L = r.shape[1]
    mm = jnp.einsum
    cum = _cumsum_rows(lw)
    cum_l = jnp.sum(lw, axis=1, keepdims=True)
    e_neg = jnp.exp(-cum)
    rt, bt, kt, at = r * jnp.exp(cum), b * e_neg, k * e_neg, a * jnp.exp(cum - lw)
    to_end = jnp.exp(cum_l - cum)
    strict, incl = _tri(L, True)[None], _tri(L)[None]
    n = jnp.where(strict, mm('hld,hsd->hls', at, bt), 0.0)
    mk = jnp.where(strict, mm('hld,hsd->hls', at, kt), 0.0)
    u = mm('hls,hsv->hlv', _unit_lower_inverse(n), mm('hld,hdv->hlv', at, h0) + mm('hls,hsv->hlv', mk, v))
    y = (mm('hld,hdv->hlv', rt, h0) + mm('hls,hsv->hlv', jnp.where(incl, mm('hld,hsd->hls', rt, bt), 0.0), u)
         + mm('hls,hsv->hlv', jnp.where(incl, mm('hld,hsd->hls', rt, kt), 0.0), v))
    h1 = (_row_to_col(jnp.exp(cum_l)) * h0 + mm('hld,hlv->hdv', b * to_end, u) + mm('hld,hlv->hdv', k * to_end, v))
    return y, h1


def _gdn_chunk(q, k, v, beta, lg, h0):
    C, D = q.shape[1], q.shape[2]
    scale = D ** -0.5
    beta, lg = _scalar_col(beta), _scalar_col(lg)
    gc = _cumsum_rows(lg)
    gc_l = jnp.sum(lg, axis=1, keepdims=True)
    causal, strict = _tri(C)[None], _tri(C, True)[None]
    decay = jnp.exp(jnp.where(causal, gc - _col_to_row(gc), -jnp.inf))
    k_beta = k * beta
    m = jnp.where(strict, jnp.einsum('hcd,hsd->hcs', k_beta, k) * decay, 0.0)
    inv = _unit_lower_inverse(-m)
    e_gc = jnp.exp(gc)
    u = jnp.einsum('hcs,hsd->hcd', inv, v * beta)
    w = jnp.einsum('hcs,hsd->hcd', inv, k_beta * e_gc)
    attn = jnp.where(causal, jnp.einsum('hcd,hsd->hcs', q * scale, k) * decay, 0.0)
    v_new = u - jnp.einsum('hcd,hde->hce', w, h0)
    o = jnp.einsum('hcd,hde->hce', q * scale * e_gc, h0) + jnp.einsum('hcs,hse->hce', attn, v_new)
    h1 = h0 * jnp.exp(gc_l) + jnp.einsum('hcd,hce->hde', k * jnp.exp(gc_l - gc), v_new)
    return o, h1


def _ssd_chunk(xs, dt, aa, bm, cm, h0):
    H, L, _ = xs.shape
    dt, aa = _scalar_col(dt), _scalar_col(aa)
    x = xs * dt
    cs = _cumsum_rows(aa)
    cs_l = jnp.sum(aa, axis=1, keepdims=True)
    causal = _tri(L)[None]
    cb = jnp.einsum('gln,gsn->gls', cm, bm)
    wd = jnp.where(causal, cb * jnp.exp(jnp.where(causal, cs - _col_to_row(cs), -jnp.inf)), 0.0)
    cmb = jnp.broadcast_to(cm, (H,) + cm.shape[1:])
    bmb = jnp.broadcast_to(bm, (H,) + bm.shape[1:])
    y = jnp.einsum('hls,hsp->hlp', wd, x) + jnp.einsum('hln,hnp->hlp', cmb, h0) * jnp.exp(cs)
    h1 = jnp.exp(cs_l) * h0 + jnp.einsum('hln,hlp->hnp', bmb, x * jnp.exp(cs_l - cs))
    return y, h1


def _chunk_scan(name, fn, seqs, hb, L, state_shape, out_width):
    seqs = tuple(seqs)
    ns = len(seqs)
    H = max(s.shape[0] for s in seqs)
    T = max(s.shape[1] for s in seqs)
    nc, nh = T // L, H // hb
    lead = [hb if s.shape[0] == H else 1 for s in seqs]
    st_block = (hb,) + state_shape

    def seq_spec(s, l, imap):
        if s.ndim == 4:
            return pl.BlockSpec((l, 1, 1, L), lambda h, c: imap(h, c) + (0,))
        return pl.BlockSpec((l, L, s.shape[2]), imap)

    fmap = lambda h, c: (h, c, 0)
    rmap = lambda h, c: (h, nc - 1 - c, 0)

    def fwd_call(seqs):
        def body(*refs):
            y_ref, st_ref, carry = refs[ns], refs[ns + 1], refs[ns + 2]

            @pl.when(pl.program_id(1) == 0)
            def _():
                carry[...] = jnp.zeros_like(carry)

            h0 = carry[...]
            st_ref[0] = h0
            y, h1 = fn(*[r[...] for r in refs[:ns]], h0)
            y_ref[...] = y
            carry[...] = h1

        return pl.pallas_call(
            body, grid=(nh, nc), in_specs=[seq_spec(s, l, fmap) for s, l in zip(seqs, lead)],
            out_specs=[pl.BlockSpec((hb, L, out_width), fmap),
                       pl.BlockSpec((1,) + st_block, lambda h, c: (c, h) + (0,) * len(state_shape))],
            out_shape=[SDS((H, T, out_width), F32), SDS((nc, H) + state_shape, F32)],
            scratch_shapes=[pltpu.VMEM(st_block, F32)], name=name + "_f",
            compiler_params=_cparams(("parallel", "arbitrary")))(*seqs)

    def bwd_call(seqs, states, dy):
        def body(*refs):
            st_ref, dy_ref = refs[ns], refs[ns + 1]
            outs, carry = refs[ns + 2:2 * ns + 2], refs[2 * ns + 2]

            @pl.when(pl.program_id(1) == 0)
            def _():
                carry[...] = jnp.zeros_like(carry)

            _, vjp = jax.vjp(fn, *[r[...] for r in refs[:ns]], st_ref[0])
            grads = vjp((dy_ref[...], carry[...]))
            for o_ref, g in zip(outs, grads[:ns]):
                o_ref[...] = g
            carry[...] = grads[ns]

        return pl.pallas_call(
            body, grid=(nh, nc),
            in_specs=[seq_spec(s, l, rmap) for s, l in zip(seqs, lead)]
            + [pl.BlockSpec((1,) + st_block, lambda h, c: (nc - 1 - c, h) + (0,) * len(state_shape)),
               pl.BlockSpec((hb, L, out_width), rmap)],
            out_specs=[seq_spec(s, l, rmap) for s, l in zip(seqs, lead)],
            out_shape=[SDS(s.shape, F32) for s in seqs],
            scratch_shapes=[pltpu.VMEM(st_block, F32)], name=name + "_b",
            compiler_params=_cparams(("parallel", "arbitrary")))(*seqs, states, dy)

    @jax.custom_vjp
    def op(seqs):
        return fwd_call(seqs)[0]

    def op_f(seqs):
        y, states = fwd_call(seqs)
        return y, (seqs, states)

    def op_b(res, dy):
        seqs, states = res
        return (tuple(bwd_call(seqs, states, dy)),)

    op.defvjp(op_f, op_b)
    return op(seqs)


def _lru_scan_call(a, u, h, reverse, name):
    T, C = a.shape
    cw = _pick(C, (1024, 512, 256, 128))
    tt = _pick(T, (512, 256, 128, 64, 32, 16, 8))
    nt, ng = T // tt, tt // 8
    sub = lambda: lax.broadcasted_iota(jnp.int32, (8, cw), 0)
    first = lambda: pl.program_id(1) == 0

    def fwd_body(a_ref, u_ref, h_ref, carry_ref):
        @pl.when(first())
        def _():
            carry_ref[...] = jnp.zeros_like(carry_ref)

        def group(i, carry):
            r0 = pl.multiple_of(i * 8, 8)
            ab, ub = a_ref[pl.ds(r0, 8), :], u_ref[pl.ds(r0, 8), :]
            out = jnp.zeros((8, cw), F32)
            for j in range(8):
                carry = ab[j:j + 1, :] * carry + ub[j:j + 1, :]
                out = jnp.where(sub() == j, carry, out)
            h_ref[pl.ds(r0, 8), :] = out
            return carry

        carry_ref[...] = lax.fori_loop(0, ng, group, carry_ref[...])

    def bwd_body(a_ref, u_ref, h_ref, hp_ref, g_ref, da_ref, cg_ref, ca_ref):
        @pl.when(first())
        def _():
            cg_ref[...] = jnp.zeros_like(cg_ref)
            ca_ref[...] = jnp.zeros_like(ca_ref)

        h_before = jnp.where(pl.program_id(1) < nt - 1, hp_ref[7:8, :], 0.0)

        def group(i, carry):
            g_next, a_next = carry
            gi = ng - 1 - i
            r0 = pl.multiple_of(gi * 8, 8)
            rp = pl.multiple_of(jnp.maximum(gi - 1, 0) * 8, 8)
            ab, ub, hb = a_ref[pl.ds(r0, 8), :], u_ref[pl.ds(r0, 8), :], h_ref[pl.ds(r0, 8), :]
            h_last_prev = jnp.where(gi > 0, h_ref[pl.ds(rp, 8), :][7:8, :], h_before)
            g_out = jnp.zeros((8, cw), F32)
            da_out = jnp.zeros((8, cw), F32)
            for j in range(7, -1, -1):
                g_next = ub[j:j + 1, :] + a_next * g_next
                a_next = ab[j:j + 1, :]
                h_prev = hb[j - 1:j, :] if j > 0 else h_last_prev
                g_out = jnp.where(sub() == j, g_next, g_out)
                da_out = jnp.where(sub() == j, g_next * h_prev, da_out)
            g_ref[pl.ds(r0, 8), :] = g_out
            da_ref[pl.ds(r0, 8), :] = da_out
            return g_next, a_next

        cg_ref[...], ca_ref[...] = lax.fori_loop(0, ng, group, (cg_ref[...], ca_ref[...]))

    row = pltpu.VMEM((1, cw), F32)
    if not reverse:
        spec = pl.BlockSpec((tt, cw), lambda i, t: (t, i))
        return pl.pallas_call(fwd_body, grid=(C // cw, nt), in_specs=[spec, spec], out_specs=spec,
                              out_shape=SDS((T, C), F32), scratch_shapes=[row], name=name,
                              compiler_params=_cparams(("parallel", "arbitrary")))(a, u)
    spec = pl.BlockSpec((tt, cw), lambda i, t: (nt - 1 - t, i))
    before = pl.BlockSpec((8, cw), lambda i, t: (jnp.maximum((nt - 1 - t) * ng - 1, 0), i))
    return pl.pallas_call(bwd_body, grid=(C // cw, nt), in_specs=[spec, spec, spec, before], out_specs=[spec, spec],
                          out_shape=[SDS((T, C), F32), SDS((T, C), F32)], scratch_shapes=[row, row], name=name,
                          compiler_params=_cparams(("parallel", "arbitrary")))(a, u, h, h)


@jax.custom_vjp
def _lru_scan(a, u):
    return _lru_scan_call(a, u, None, False, "lru_scan_f")


def _lru_scan_f(a, u):
    h = _lru_scan(a, u)
    return h, (a, h)


def _lru_scan_b(res, dh):
    a, h = res
    g, da = _lru_scan_call(a, dh, h, True, "lru_scan_b")
    return da, g


_lru_scan.defvjp(_lru_scan_f, _lru_scan_b)


def _heads_major(x, nheads):
    T, W = x.shape
    return jnp.transpose(x.reshape(T, nheads, W // nheads), (1, 0, 2))


def _tokens_major(x):
    H, T, N = x.shape
    return jnp.transpose(x, (1, 0, 2)).reshape(T, H * N)


def _pad_cols(w, offs, widths, total):
    parts, pos, src = [], 0, 0
    for o, wd in zip(offs, widths):
        if o > pos:
            parts.append(jnp.zeros((w.shape[0], o - pos), w.dtype))
        parts.append(w[:, src:src + wd])
        src += wd
        pos = o + wd
    if pos < total:
        parts.append(jnp.zeros((w.shape[0], total - pos), w.dtype))
    return jnp.concatenate(parts, axis=1)


def _unpad_cols(w, offs, widths):
    return jnp.concatenate([w[:, o:o + wd] for o, wd in zip(offs, widths)], axis=1)


def _aligned_layout(widths):
    offs, pos = [], 0
    for w in widths:
        offs.append(pos)
        pos += _rup(w, LANES)
    return offs, _rup(pos, 512)


def _pad_lanes(v, n):
    return jnp.pad(v, ((0, 0), (0, n - v.shape[1])))


def _even_mixer(x, q, wz, li):
    T = x.shape[0]
    ah, an = q['rwkv_r_k'].shape
    aw = ah * an
    bh, bn = q['gdn_A_log'].shape[0], q['gdn_norm_g'].shape[0]
    bw = bh * bn
    lw_, la_, lg_ = q['rwkv_w2'].shape[0], q['rwkv_a2'].shape[0], q['rwkv_g2'].shape[0]
    widths = [aw, aw, aw, lw_, la_, lg_, bw, bw, bw, bw, bh, bh]
    offs, total = _aligned_layout(widths)
    pw = [_rup(w, LANES) for w in widths]
    hcols = _matmul(x, _pad_cols(q['even_w_in'], offs, widths, total), wz['even_w_in'], f"even_in{li}")
    a_w = offs[6]
    a_cols, bq, bk, bv, bz, beta_raw, alpha_raw = _split_cols(hcols, [0] + offs[6:], [a_w] + pw[6:])

    mu = _pad_cols(q['rwkv_mu'][None], offs[:6], widths[:6], a_w)
    (xs,) = _colwise(f"rwkv_shift{li}", lambda h, m: (h + (_shift(h, 1) - h) * m,), [a_cols], [mu])
    r, k, v, w_lo, a_lo, g_lo = _split_cols(xs, offs[:6], pw[:6])
    tw, sg = _rowwise(f"rwkv_lora_act{li}", lambda w, g: (jnp.tanh(w), jax.nn.sigmoid(g)), [w_lo, g_lo])
    pad_rows = lambda w, n: jnp.pad(w, ((0, n - w.shape[0]), (0, 0)))
    wl = _matmul(tw, pad_rows(q['rwkv_w2'], pw[3]), wz['rwkv_w2'], f"rwkv_w2{li}")
    al = _matmul(a_lo, pad_rows(q['rwkv_a2'], pw[4]), wz['rwkv_a2'], f"rwkv_a2{li}")
    g = _matmul(sg, pad_rows(q['rwkv_g2'], pw[5]), wz['rwkv_g2'], f"rwkv_g2{li}")
    ones_a = _group_ones(aw, an)

    def pre(k, wl, al, w0, a0, k_k, k_a, ones):
        lw = -jnp.exp(-_softplus(-(w0 + wl)) - 0.5)
        a = jax.nn.sigmoid(a0 + al)
        kk = k * k_k
        kk = kk * lax.rsqrt(jnp.dot(kk * kk, ones) + L2_EPS)
        return lw, k * (1.0 + (a - 1.0) * k_a), -kk, kk * a

    lw, k2, sa, sb = _rowwise(f"rwkv_pre{li}", pre, [k, wl, al],
                              [q['rwkv_w0'][None], q['rwkv_a0'][None], q['rwkv_k_k'][None], q['rwkv_k_a'][None]], [ones_a])
    hm = lambda t: _heads_major(t, ah)
    out = _chunk_scan(f"rwkv_scan{li}", _rwkv_chunk, [hm(r), hm(lw), hm(k2), hm(v), hm(sa), hm(sb)],
                      min(ah, SCAN_HEADS), min(RWKV_CHUNK, T), (an, an), an)
    out = _tokens_major(out)

    def post(out, r, k2, v, g, gn_g, gn_b, r_k, ones):
        mean = jnp.dot(out, ones) * (1.0 / an)
        cen = out - mean
        var = jnp.dot(cen * cen, ones) * (1.0 / an)
        normed = cen * lax.rsqrt(var + A_GN_EPS) * gn_g + gn_b
        bonus = jnp.dot(r * k2 * r_k, ones) * v
        return ((normed + bonus) * g,)

    flat = lambda t: t.reshape(1, -1)
    (ya,) = _rowwise(f"rwkv_post{li}", post, [out, r, k2, v, g],
                     [flat(q['rwkv_gn_g']), flat(q['rwkv_gn_b']), flat(q['rwkv_r_k'])], [ones_a])

    cw = q['gdn_conv_w']

    def conv_l2(x, w):
        y = _silu(_causal_conv(x, w))
        return (y * lax.rsqrt(jnp.sum(y * y, axis=1, keepdims=True) + L2_EPS),)

    if bn == LANES:
        (gq,) = _colwise(f"gdn_conv_q{li}", conv_l2, [bq], [cw[:, :bw]])
        (gk,) = _colwise(f"gdn_conv_k{li}", conv_l2, [bk], [cw[:, bw:2 * bw]])
    else:
        raise NotImplementedError("gated DeltaNet head width must equal the lane count")
    (gv,) = _colwise(f"gdn_conv_v{li}", lambda x, w: (_silu(_causal_conv(x, w)),), [bv], [cw[:, 2 * bw:]])

    def gates(beta_raw, alpha_raw, a_log, dt_bias):
        return jax.nn.sigmoid(beta_raw), -jnp.exp(a_log) * _softplus(alpha_raw + dt_bias)

    beta, lg = _rowwise(f"gdn_gates{li}", gates, [beta_raw, alpha_raw],
                        [_pad_lanes(q['gdn_A_log'][None], pw[10]), _pad_lanes(q['gdn_dt_bias'][None], pw[11])])
    gl = min(GDN_CHUNK, T)
    col = lambda t: jnp.transpose(t[:, :bh]).reshape(bh, T // gl, 1, gl)
    hmb = lambda t: _heads_major(t, bh)
    o = _chunk_scan(f"gdn_scan{li}", _gdn_chunk, [hmb(gq), hmb(gk), hmb(gv), col(beta), col(lg)],
                    min(bh, SCAN_HEADS), gl, (bn, bn), bn)
    o = _tokens_major(o)
    ones_b = _group_ones(bw, bn)

    def gdn_post(o, z, ng, ones):
        ms = jnp.dot(o * o, ones) * (1.0 / bn)
        return (o * lax.rsqrt(ms + RMS_EPS) * ng * _silu(z),)

    (yb,) = _rowwise(f"gdn_post{li}", gdn_post, [o, bz], [jnp.tile(q['gdn_norm_g'][None], (1, bh))], [ones_b])
    return _matmul(jnp.concatenate([ya, yb], axis=1), q['even_w_out'], wz['even_w_out'], f"even_out{li}")


def _odd_mixer(x, q, wz, li):
    T = x.shape[0]
    ch = q['mamba_dt_bias'].shape[0]
    cwid = q['mamba_norm_g'].shape[0]
    cp = cwid // ch
    xbc_w = q['mamba_conv_w'].shape[1]
    cn = (xbc_w - cwid) // (2 * C_GROUPS)
    dw = q['lru_lambda'].shape[0]
    widths = [cwid, xbc_w, ch, dw, dw]
    offs, total = _aligned_layout(widths)
    pw = [_rup(w, LANES) for w in widths]
    hcols = _matmul(x, _pad_cols(q['odd_w_in'], offs, widths, total), wz['odd_w_in'], f"odd_in{li}")
    z, xbc, dt_raw, y_br, x_br = _split_cols(hcols, offs, pw)

    (xbc_c,) = _colwise(f"mamba_conv{li}", lambda x, w, b: (_silu(_causal_conv(x, w) + b),), [xbc],
                        [q['mamba_conv_w'], q['mamba_conv_b'][None]])
    gn = C_GROUPS * cn
    xs, bm, cm = _split_cols(xbc_c, [0, cwid, cwid + gn], [cwid, gn, gn])

    def dts(dt_raw, dt_bias, a_log):
        dt = _softplus(dt_raw + dt_bias)
        return dt, dt * (-jnp.exp(a_log))

    dt, aa = _rowwise(f"mamba_dt{li}", dts, [dt_raw],
                      [_pad_lanes(q['mamba_dt_bias'][None], pw[2]), _pad_lanes(q['mamba_A_log'][None], pw[2])])
    sl = min(SSD_CHUNK, T)
    col = lambda t: jnp.transpose(t[:, :ch]).reshape(ch, T // sl, 1, sl)
    y = _chunk_scan(f"ssd_scan{li}", _ssd_chunk,
                    [_heads_major(xs, ch), col(dt), col(aa), _heads_major(bm, C_GROUPS), _heads_major(cm, C_GROUPS)],
                    ch // C_GROUPS, sl, (cn, cp), cp)
    y = _tokens_major(y)
    gsz = cwid // C_GROUPS

    def mamba_post(y, xs, z, d, ng):
        yy = (y + xs * d) * _silu(z)
        lane = lax.broadcasted_iota(jnp.int32, yy.shape, 1)
        ms = jnp.zeros_like(yy)
        for gi in range(C_GROUPS):
            sel = (lane >= gi * gsz) & (lane < (gi + 1) * gsz)
            ms = jnp.where(sel, jnp.sum(jnp.where(sel, yy * yy, 0.0), axis=1, keepdims=True) * (1.0 / gsz), ms)
        return (yy * lax.rsqrt(ms + RMS_EPS) * ng,)

    (yc,) = _rowwise(f"mamba_post{li}", mamba_post, [y, xs, z],
                     [jnp.repeat(q['mamba_D'], cp)[None], q['mamba_norm_g'][None]])

    (xc,) = _colwise(f"lru_conv{li}", lambda x, w, b: (_causal_conv(x, w) + b,), [x_br],
                     [q['lru_conv_w'], q['lru_conv_b'][None]])
    ra = _bdmm(xc, q['lru_wa'], f"lru_wa{li}")
    ia = _bdmm(xc, q['lru_wx'], f"lru_wx{li}")

    def lru_pre(ra, ia, xc, ba, bx, lam):
        r = jax.nn.sigmoid(ra + ba)
        i = jax.nn.sigmoid(ia + bx)
        log_a = LRU_C * r * (-_softplus(-lam))
        t = 2.0 * log_a
        series = t * (1.0 + t * (0.5 + t * (1.0 / 6.0 + t * (1.0 / 24.0 + t * (1.0 / 120.0 + t * (1.0 / 720.0))))))
        expm1 = jnp.where(t > -0.2, series, jnp.exp(t) - 1.0)
        return jnp.exp(log_a), jnp.sqrt(-expm1) * (i * xc)

    a, u = _rowwise(f"lru_pre{li}", lru_pre, [ra, ia, xc], [q['lru_ba'][None], q['lru_bx'][None], q['lru_lambda'][None]])
    h = _lru_scan(a, u)
    (yd,) = _rowwise(f"lru_post{li}", lambda h, y: (h * jax.nn.gelu(y),), [h, y_br])
    return _matmul(jnp.concatenate([yc, yd], axis=1), q['odd_w_out'], wz['odd_w_out'], f"odd_out{li}")


def _forward(x, wz, sp, p, w16, depth):
    alpha = (2.0 * depth) ** 0.25
    for i in range(depth):
        j = i // 2
        even = i % 2 == 0
        names = [n for n in WEIGHTS if n.startswith(('rwkv_', 'gdn_', 'even_') if even else ('mamba_', 'lru_', 'odd_'))]
        q = {n: (w16[n][j] if n in MATRICES else sp[n][j]) for n in names}
        wzl = {n: wz[f"{n}.{j}"] for n in names if n in MATRICES}
        y = (_even_mixer if even else _odd_mixer)(x, q, wzl, i)

        def ln_res(x, y, g, b):
            return (_layer_norm_rows(alpha * x + y, g, b, LN_EPS),)

        (h,) = _rowwise(f"ln1_{i}", ln_res, [x, y], [sp['ln1_g'][i][None], sp['ln1_b'][i][None]])
        dff = w16['ffn_up'].shape[2] // 2
        gate = _matmul(h, w16['ffn_up'][i][:, :dff], wz[f"ffn_up.{i}"], f"ffn_gate{i}")
        val = _matmul(h, w16['ffn_up'][i][:, dff:], wz[f"ffn_up_val.{i}"], f"ffn_val{i}")
        cw, cb = sp['ffn_conv_w'][i], sp['ffn_conv_b'][i][None]

        def ffn_act(gate, val, wg, wv, bg, bv):
            return (_silu(_causal_conv(gate, wg) + bg) * (_causal_conv(val, wv) + bv),)

        (act,) = _colwise(f"ffn_act{i}", ffn_act, [gate, val], [cw[:, :dff], cw[:, dff:], cb[:, :dff], cb[:, dff:]])
        f = _matmul(act, w16['ffn_down'][i], wz[f"ffn_down.{i}"], f"ffn_down{i}")
        (h2,) = _rowwise(f"ln2_{i}", ln_res, [h, f], [sp['ln2_g'][i][None], sp['ln2_b'][i][None]])
        e0 = _matmul(p[i], w16['ple_proj'][i], wz[f"ple_proj.{i}"], f"ple_proj{i}")
        gl = _matmul(h2, w16['ple_gate_w'][i], wz[f"ple_gate_w.{i}"], f"ple_gate{i}")

        def ple(h2, gl, e0, gb, ng):
            e = e0 * lax.rsqrt(jnp.mean(e0 * e0, axis=1, keepdims=True) + RMS_EPS) * ng
            return (h2 + jax.nn.sigmoid(gl + gb) * e,)

        (x,) = _rowwise(f"ple{i}", ple, [h2, gl, e0], [sp['ple_gate_b'][i][None], sp['ple_norm_g'][i][None]])
    return x


def _loss_head(y, target):
    T, D = y.shape
    tile = _pick(T, (256, 128, 64, 32, 16, 8))

    def body(y_ref, t_ref, dy_ref, l_ref):
        err = y_ref[...] - t_ref[...]
        dy_ref[...] = err * (1.0 / D)

        @pl.when(pl.program_id(0) == 0)
        def _():
            l_ref[...] = jnp.zeros_like(l_ref)

        l_ref[...] += jnp.sum(jnp.sum(err * err, axis=1, keepdims=True), axis=0, keepdims=True) * (0.5 / D) + jnp.zeros_like(l_ref)

    spec = pl.BlockSpec((tile, D), lambda i: (i, 0))
    dy, l = pl.pallas_call(body, grid=(T // tile,), in_specs=[spec, spec],
                           out_specs=[spec, pl.BlockSpec((8, LANES), lambda i: (0, 0))],
                           out_shape=[SDS((T, D), F32), SDS((8, LANES), F32)], name="loss_head",
                           compiler_params=_cparams(("arbitrary",)))(y, target)
    return l[0, 0], dy


def _my_index():
    return 4 * lax.axis_index("x") + 2 * lax.axis_index("y") + lax.axis_index("c")


def _hbm_specs(n):
    return [pl.BlockSpec(memory_space=pl.ANY)] * n


def _all_gather(blocks, name):
    blocks = tuple(blocks)
    n = len(blocks)
    half = [b.shape[0] // 2 for b in blocks]

    def body(*refs):
        ins, outs = refs[:n], refs[n:2 * n]
        send_sems, recv_sems, local_sems = refs[2 * n:]
        x, y, c = lax.axis_index("x"), lax.axis_index("y"), lax.axis_index("c")
        me, sibling, other = (x, y, c), (x, y, 1 - c), 1 - c
        xn, yn, dg = (1 - x, y), (x, 1 - y), (1 - x, 1 - y)

        def slot(i, px, py, pc, h=None):
            ref = outs[i].at[4 * px + 2 * py + pc]
            return ref if h is None else ref.at[pl.ds(h * half[i], half[i])]

        def copy(i, k, blk, to, h=None, src=None):
            dst = slot(i, *blk, h)
            return pltpu.make_async_remote_copy(
                src_ref=dst if src is None else src, dst_ref=dst, send_sem=send_sems.at[9 * i + k],
                recv_sem=recv_sems.at[9 * i + k], device_id=to, device_id_type=MESH)

        mine = [pltpu.make_async_copy(ins[i], slot(i, *me), local_sems.at[i]) for i in range(n)]
        sent = []
        for i in range(n):
            sent += [copy(i, 1, me, (*xn, c), src=ins[i]), copy(i, 2, me, (*yn, c), src=ins[i])]
        sent += [copy(i, 0, me, sibling, src=ins[i]) for i in range(n)]
        for cp in mine + sent:
            cp.start()

        def after(i, k_in, blk, h_in, forwards):
            copy(i, k_in, blk, me, h_in).wait_recv()
            for k_out, to, h_out in forwards:
                sent.append(copy(i, k_out, blk, to, h_out))
                sent[-1].start()

        for i in range(n):
            after(i, 1, (*xn, c), None, [(3, (*yn, c), 0), (5, sibling, None)])
        for i in range(n):
            after(i, 2, (*yn, c), None, [(4, (*xn, c), 1), (6, sibling, None)])
        for i in range(n):
            after(i, 3, (*dg, c), 0, [(7, sibling, 0)])
        for i in range(n):
            after(i, 4, (*dg, c), 1, [(8, sibling, 1)])
        for i in range(n):
            copy(i, 0, sibling, me).wait_recv()
            copy(i, 5, (*xn, other), me).wait_recv()
            copy(i, 6, (*yn, other), me).wait_recv()
            copy(i, 7, (*dg, other), me, 0).wait_recv()
            copy(i, 8, (*dg, other), me, 1).wait_recv()
        for cp in sent:
            cp.wait_send()
        for cp in mine:
            cp.wait()

    return pl.pallas_call(
        body, out_shape=[SDS((N_DEV,) + b.shape, b.dtype) for b in blocks], in_specs=_hbm_specs(n), out_specs=_hbm_specs(n),
        scratch_shapes=[pltpu.SemaphoreType.DMA((9 * n,)), pltpu.SemaphoreType.DMA((9 * n,)), pltpu.SemaphoreType.DMA((n,))],
        name=name)(*blocks)


def _sibling_exchange(parts, name):
    parts = tuple(parts)
    n = len(parts)

    def body(*refs):
        ins, outs = refs[:n], refs[n:2 * n]
        send_sems, recv_sems = refs[2 * n:]
        x, y, c = lax.axis_index("x"), lax.axis_index("y"), lax.axis_index("c")
        copies = [pltpu.make_async_remote_copy(
            src_ref=ins[i].at[q, 1 - c], dst_ref=outs[i].at[q], send_sem=send_sems.at[4 * i + q],
            recv_sem=recv_sems.at[4 * i + q], device_id=(x, y, 1 - c), device_id_type=MESH)
            for i in range(n) for q in range(4)]
        for cp in copies:
            cp.start()
        for cp in copies:
            cp.wait_recv()
        for cp in copies:
            cp.wait_send()

    return pl.pallas_call(
        body, out_shape=[SDS((4,) + p.shape[2:], p.dtype) for p in parts], in_specs=_hbm_specs(n), out_specs=_hbm_specs(n),
        scratch_shapes=[pltpu.SemaphoreType.DMA((4 * n,)), pltpu.SemaphoreType.DMA((4 * n,))], name=name)(*parts)


def _chip_exchange(sends, name):
    sends = tuple(sends)
    n = len(sends)

    def body(*refs):
        ins, outs = refs[:n], refs[n:2 * n]
        send_sems, recv_sems = refs[2 * n:]
        x, y, c = lax.axis_index("x"), lax.axis_index("y"), lax.axis_index("c")
        chips = [(1 - x, y), (x, 1 - y), (1 - x, 1 - y)]
        copies = [pltpu.make_async_remote_copy(
            src_ref=ins[i].at[k], dst_ref=outs[i].at[k], send_sem=send_sems.at[3 * i + k],
            recv_sem=recv_sems.at[3 * i + k], device_id=(*chip, c), device_id_type=MESH)
            for k, chip in enumerate(chips) for i in range(n)]
        for cp in copies:
            cp.start()
        for cp in copies:
            cp.wait_recv()
        for cp in copies:
            cp.wait_send()

    return pl.pallas_call(
        body, out_shape=[SDS(s.shape, s.dtype) for s in sends], in_specs=_hbm_specs(n), out_specs=_hbm_specs(n),
        scratch_shapes=[pltpu.SemaphoreType.DMA((3 * n,)), pltpu.SemaphoreType.DMA((3 * n,))], name=name)(*sends)


def _place_ids():
    x, y, c = lax.axis_index("x"), lax.axis_index("y"), lax.axis_index("c")
    return jnp.stack([c, 2 * x + y, 2 * (1 - x) + y, 2 * x + (1 - y), 2 * (1 - x) + (1 - y)]).astype(jnp.int32)


def _row_tile(rows, cols):
    best = None
    for t in range(16, rows + 1, 16):
        if rows % t == 0 and t * cols <= 256 * 1024:
            best = t
    return best or rows


def _chip_partials(part, recv_a, ids, dtype, name):
    _, _, R, C = part.shape
    tr = _row_tile(R, C)

    def body(ids_ref, p_ref, a_ref, o_ref):
        o_ref[...] = (p_ref[...] + a_ref[...]).astype(dtype)

    return pl.pallas_call(
        body, out_shape=SDS((3, R, C), dtype),
        grid_spec=pltpu.PrefetchScalarGridSpec(
            num_scalar_prefetch=1, grid=(3, R // tr),
            in_specs=[pl.BlockSpec((None, None, tr, C), lambda s, i, ids: (ids[2 + s], ids[0], i, 0)),
                      pl.BlockSpec((None, tr, C), lambda s, i, ids: (ids[2 + s], i, 0))],
            out_specs=pl.BlockSpec((None, tr, C), lambda s, i, ids: (s, i, 0))),
        name=name, compiler_params=_cparams(("parallel", "parallel")))(ids, part, recv_a)


def _adamw(part, recv_a, recv_c, ids, w, m, v, name):
    R, C = w.shape
    tr = _row_tile(R, C)
    c1 = 1.0 / (1.0 - ADAM_B1 ** ADAM_STEP)
    c2 = 1.0 / (1.0 - ADAM_B2 ** ADAM_STEP)

    def body(ids_ref, p_ref, a_ref, c_ref, w_ref, m_ref, v_ref, g_ref, d_ref, nm_ref, nv_ref):
        g = p_ref[...] + a_ref[...]
        for k in range(3):
            g = g + c_ref[k].astype(F32)
        nm = ADAM_B1 * m_ref[...] + (1.0 - ADAM_B1) * g
        nv = ADAM_B2 * v_ref[...] + (1.0 - ADAM_B2) * jnp.square(g)
        g_ref[...] = g
        nm_ref[...] = nm
        nv_ref[...] = nv
        d_ref[...] = -ADAM_LR * ((nm * c1) / (jnp.sqrt(nv * c2) + ADAM_EPS) + ADAM_WD * w_ref[...])

    spec = pl.BlockSpec((tr, C), lambda i, ids: (i, 0))
    return pl.pallas_call(
        body, out_shape=[SDS((R, C), F32)] * 4,
        grid_spec=pltpu.PrefetchScalarGridSpec(
            num_scalar_prefetch=1, grid=(R // tr,),
            in_specs=[pl.BlockSpec((None, None, tr, C), lambda i, ids: (ids[1], ids[0], i, 0)),
                      pl.BlockSpec((None, tr, C), lambda i, ids: (ids[1], i, 0)),
                      pl.BlockSpec((3, tr, C), lambda i, ids: (0, i, 0)), spec, spec, spec],
            out_specs=[spec] * 4),
        name=name, compiler_params=_cparams(("parallel",)))(ids, part, recv_a, recv_c, w, m, v)


def _to_flat(vecs, quantum):
    flat = jnp.concatenate([v.reshape(-1) for v in vecs])
    n = _rup(flat.shape[0], quantum * FLAT_COLS)
    return jnp.pad(flat, (0, n - flat.shape[0])).reshape(n // FLAT_COLS, FLAT_COLS)


def _gathered_to_full(g, names, blocks):
    flat = g.reshape(N_DEV, -1)
    out, pos = {}, 0
    for n in names:
        shp = blocks[n]
        size = math.prod(shp)
        out[n] = _blocks_to_full(flat[:, pos:pos + size].reshape((N_DEV,) + shp), SHARD_AXIS[n])
        pos += size
    return out


def _blocks_to_full(g, ax):
    shp = g.shape[1:]
    return jnp.moveaxis(g, 0, ax).reshape(shp[:ax] + (N_DEV * shp[ax],) + shp[ax + 1:])


def _full_to_blocks(g, ax, ndev=N_DEV):
    shp = g.shape
    t = g.reshape(shp[:ax] + (ndev, shp[ax] // ndev) + shp[ax + 1:])
    return jnp.moveaxis(t, ax, 0)


def kernel(x, p, ln1_g, ln1_b, ln2_g, ln2_b, ffn_up, ffn_conv_w, ffn_conv_b, ffn_down, ple_proj, ple_norm_g, ple_gate_w, ple_gate_b, even_w_in, even_w_out, rwkv_mu, rwkv_w0, rwkv_w2, rwkv_a0, rwkv_a2, rwkv_g2, rwkv_k_k, rwkv_k_a, rwkv_r_k, rwkv_gn_g, rwkv_gn_b, gdn_conv_w, gdn_A_log, gdn_dt_bias, gdn_norm_g, odd_w_in, odd_w_out, mamba_conv_w, mamba_conv_b, mamba_dt_bias, mamba_A_log, mamba_D, mamba_norm_g, lru_conv_w, lru_conv_b, lru_wa, lru_ba, lru_wx, lru_bx, lru_lambda, loss_target, m_ln1_g, m_ln1_b, m_ln2_g, m_ln2_b, m_ffn_up, m_ffn_conv_w, m_ffn_conv_b, m_ffn_down, m_ple_proj, m_ple_norm_g, m_ple_gate_w, m_ple_gate_b, m_even_w_in, m_even_w_out, m_rwkv_mu, m_rwkv_w0, m_rwkv_w2, m_rwkv_a0, m_rwkv_a2, m_rwkv_g2, m_rwkv_k_k, m_rwkv_k_a, m_rwkv_r_k, m_rwkv_gn_g, m_rwkv_gn_b, m_gdn_conv_w, m_gdn_A_log, m_gdn_dt_bias, m_gdn_norm_g, m_odd_w_in, m_odd_w_out, m_mamba_conv_w, m_mamba_conv_b, m_mamba_dt_bias, m_mamba_A_log, m_mamba_D, m_mamba_norm_g, m_lru_conv_w, m_lru_conv_b, m_lru_wa, m_lru_ba, m_lru_wx, m_lru_bx, m_lru_lambda, v_ln1_g, v_ln1_b, v_ln2_g, v_ln2_b, v_ffn_up, v_ffn_conv_w, v_ffn_conv_b, v_ffn_down, v_ple_proj, v_ple_norm_g, v_ple_gate_w, v_ple_gate_b, v_even_w_in, v_even_w_out, v_rwkv_mu, v_rwkv_w0, v_rwkv_w2, v_rwkv_a0, v_rwkv_a2, v_rwkv_g2, v_rwkv_k_k, v_rwkv_k_a, v_rwkv_r_k, v_rwkv_gn_g, v_rwkv_gn_b, v_gdn_conv_w, v_gdn_A_log, v_gdn_dt_bias, v_gdn_norm_g, v_odd_w_in, v_odd_w_out, v_mamba_conv_w, v_mamba_conv_b, v_mamba_dt_bias, v_mamba_A_log, v_mamba_D, v_mamba_norm_g, v_lru_conv_w, v_lru_conv_b, v_lru_wa, v_lru_ba, v_lru_wx, v_lru_bx, v_lru_lambda):
    args = locals()
    w = {n: args[n] for n in WEIGHTS}
    m = {n: args["m_" + n] for n in WEIGHTS}
    v = {n: args["v_" + n] for n in WEIGHTS}
    depth = ln1_g.shape[0]
    me = _my_index()
    blocks = {n: w[n].shape for n in WEIGHTS}

    as_rows = lambda t: t.reshape(-1, t.shape[-1])
    small = _to_flat([w[n] for n in SMALL_SHARDED], 16)
    gathered = _all_gather([as_rows(w[n].astype(BF16)) for n in MATRICES] + [small], "gather_params")
    w16 = {n: _blocks_to_full(g.reshape((N_DEV,) + blocks[n]), SHARD_AXIS[n]) for n, g in zip(MATRICES, gathered)}
    sp = _gathered_to_full(gathered[-1], SMALL_SHARDED, blocks)
    sp.update({n: w[n] for n in REPLICATED})

    lay = _matrix_layouts(w16, sp)
    wz = {k: jnp.zeros(shape, F32) for k, shape in lay['padded'].items()}
    y, vjp = jax.vjp(lambda x_, wz_, sp_: _forward(x_, wz_, sp_, p[:, 0], w16, depth), x[0], wz, sp)
    loss_local, dy = _loss_head(y, loss_target[0])
    dx, dwz, dsp = vjp(dy)
    loss = lax.psum(loss_local, ("x", "y", "c"))
    gfull = dict(dsp)
    gblocks = {}
    for n in MATRICES:
        layers = range(w16[n].shape[0])
        if n == 'ffn_up':
            halves = [jnp.stack([dwz[f"{k}.{j}"] for j in layers]) for k in ("ffn_up", "ffn_up_val")]
            gblocks[n] = jnp.concatenate([_full_to_blocks(h, SHARD_AXIS[n], N_DEV // 2) for h in halves], axis=0)
        else:
            gblocks[n] = _full_to_blocks(jnp.stack([lay['unpad'][n](dwz[f"{n}.{j}"]) for j in layers]), SHARD_AXIS[n])

    rep_flat = jnp.concatenate([gfull[n].reshape(-1) for n in REPLICATED])
    rep_n = rep_flat.shape[0]
    piece = _rup(rep_n, N_DEV * LANES) // N_DEV
    rep_pad = lambda t: jnp.pad(t, (0, N_DEV * piece - rep_n))
    small_parts = jnp.concatenate([_full_to_blocks(gfull[n], SHARD_AXIS[n]).reshape(N_DEV, -1) for n in SMALL_SHARDED]
                                  + [rep_pad(rep_flat).reshape(N_DEV, piece)], axis=1)
    n_flat = small_parts.shape[1]
    n_pad = _rup(n_flat, 16 * FLAT_COLS)
    small_parts = jnp.pad(small_parts, ((0, 0), (0, n_pad - n_flat)))

    def my_small(d):
        rep = rep_pad(jnp.concatenate([d[n].reshape(-1) for n in REPLICATED]))
        mine = lax.dynamic_slice(rep, (me * piece,), (piece,))
        flat = jnp.concatenate([d[n].reshape(-1) for n in SMALL_SHARDED] + [mine])
        return jnp.pad(flat, (0, n_pad - n_flat)).reshape(n_pad // FLAT_COLS, FLAT_COLS)

    by_chip = lambda t, cols: t.reshape(4, 2, -1, cols)
    parts = [by_chip(gblocks[n], blocks[n][-1]) for n in MATRICES]
    parts.append(by_chip(small_parts, FLAT_COLS))
    wire = [BF16] * len(MATRICES) + [F32]
    tags = MATRICES + ["small"]
    ids = _place_ids()
    recv_a = _sibling_exchange(parts, "reduce_sibling")
    recv_c = _chip_exchange([_chip_partials(pt, ra, ids, dt, f"chip_partials_{t}")
                             for pt, ra, dt, t in zip(parts, recv_a, wire, tags)], "reduce_chips")
    mine = [(as_rows(w[n]), as_rows(m[n]), as_rows(v[n])) for n in MATRICES] + [(my_small(w), my_small(m), my_small(v))]
    results = [_adamw(pt, ra, rc, ids, *wmv, f"adamw_{t}")
               for pt, ra, rc, wmv, t in zip(parts, recv_a, recv_c, mine, tags)]
    small_res = [r.reshape(-1) for r in results[-1]]
    rep_res = jnp.stack([r[n_flat - piece:n_flat] for r in small_res])
    rep_rows = _rup(4 * piece, 8 * FLAT_COLS) // FLAT_COLS
    rep_blk = jnp.pad(rep_res.reshape(-1), (0, rep_rows * FLAT_COLS - 4 * piece)).reshape(rep_rows, FLAT_COLS)
    (rep_all,) = _all_gather([rep_blk], "gather_replicated")
    rep_all = rep_all.reshape(N_DEV, -1)[:, :4 * piece]
    rep_all = jnp.transpose(rep_all.reshape(N_DEV, 4, piece), (1, 0, 2)).reshape(4, N_DEV * piece)

    outs = [{}, {}, {}, {}]
    for k in range(4):
        for n, res in zip(MATRICES, results):
            outs[k][n] = res[k].reshape(blocks[n])
        pos = 0
        for n in SMALL_SHARDED:
            size = math.prod(blocks[n])
            outs[k][n] = small_res[k][pos:pos + size].reshape(blocks[n])
            pos += size
        pos = 0
        for n in REPLICATED:
            size = math.prod(blocks[n])
            outs[k][n] = rep_all[k, pos:pos + size].reshape(blocks[n])
            pos += size
    return (loss, dx[None], *[outs[0][n] for n in WEIGHTS], *[outs[1][n] for n in WEIGHTS],
            *[outs[2][n] for n in WEIGHTS], *[outs[3][n] for n in WEIGHTS])


def _matrix_layouts(w16, sp):
    padded, unpad = {}, {}
    ident = lambda g: g
    for n in ('ffn_down', 'ple_proj', 'ple_gate_w', 'even_w_out', 'odd_w_out'):
        for j in range(w16[n].shape[0]):
            padded[f"{n}.{j}"] = w16[n].shape[1:]
        unpad[n] = ident
    for j in range(w16['ffn_up'].shape[0]):
        half = (w16['ffn_up'].shape[1], w16['ffn_up'].shape[2] // 2)
        padded[f"ffn_up.{j}"] = padded[f"ffn_up_val.{j}"] = half
    for n in ('rwkv_w2', 'rwkv_a2', 'rwkv_g2'):
        rows, cols = w16[n].shape[1:]
        for j in range(w16[n].shape[0]):
            padded[f"{n}.{j}"] = (_rup(rows, LANES), cols)
        unpad[n] = functools.partial(lambda g, rows: g[:rows], rows=rows)
    ah, an = sp['rwkv_r_k'].shape[1:]
    bh, bn = sp['gdn_A_log'].shape[1], sp['gdn_norm_g'].shape[1]
    ew = [ah * an] * 3 + [w16['rwkv_w2'].shape[1], w16['rwkv_a2'].shape[1], w16['rwkv_g2'].shape[1]] + [bh * bn] * 4 + [bh, bh]
    cwid, ch = sp['mamba_norm_g'].shape[1], sp['mamba_dt_bias'].shape[1]
    dw = sp['lru_lambda'].shape[1]
    ow = [cwid, sp['mamba_conv_w'].shape[2], ch, dw, dw]
    for n, widths in (('even_w_in', ew), ('odd_w_in', ow)):
        offs, total = _aligned_layout(widths)
        for j in range(w16[n].shape[0]):
            padded[f"{n}.{j}"] = (w16[n].shape[1], total)
        unpad[n] = functools.partial(_unpad_cols, offs=offs, widths=widths)
    return {'padded': padded, 'unpad': unpad}
```

```python
import functools
import math

import jax
import jax.numpy as jnp
from jax import lax
from jax.experimental import pallas as pl
from jax.experimental.pallas import tpu as pltpu

F32 = jnp.float32
BF16 = jnp.bfloat16
HI = lax.Precision.HIGHEST
SDS = jax.ShapeDtypeStruct
MESH = pl.DeviceIdType.MESH

LANES = 128
VMEM_LIMIT = 56 * 1024 * 1024
N_DEV = 8
FLAT_COLS = 1024
MM_VMEM = 40 * 1024 * 1024

LN_EPS = 1e-5
RMS_EPS = 1e-6
L2_EPS = 1e-6
A_GN_EPS = 64e-5
LRU_C = 8.0
C_GROUPS = 4
RWKV_CHUNK = 64
GDN_CHUNK = 64
SSD_CHUNK = 128
SCAN_HEADS = 16

ADAM_LR, ADAM_B1, ADAM_B2, ADAM_EPS, ADAM_WD, ADAM_STEP = 0.001, 0.9, 0.999, 1e-08, 0.01, 10

WEIGHTS = ['ln1_g', 'ln1_b', 'ln2_g', 'ln2_b', 'ffn_up', 'ffn_conv_w', 'ffn_conv_b', 'ffn_down', 'ple_proj',
           'ple_norm_g', 'ple_gate_w', 'ple_gate_b', 'even_w_in', 'even_w_out', 'rwkv_mu', 'rwkv_w0', 'rwkv_w2',
           'rwkv_a0', 'rwkv_a2', 'rwkv_g2', 'rwkv_k_k', 'rwkv_k_a', 'rwkv_r_k', 'rwkv_gn_g', 'rwkv_gn_b',
           'gdn_conv_w', 'gdn_A_log', 'gdn_dt_bias', 'gdn_norm_g', 'odd_w_in', 'odd_w_out', 'mamba_conv_w',
           'mamba_conv_b', 'mamba_dt_bias', 'mamba_A_log', 'mamba_D', 'mamba_norm_g', 'lru_conv_w', 'lru_conv_b',
           'lru_wa', 'lru_ba', 'lru_wx', 'lru_bx', 'lru_lambda']
SHARD_AXIS = {'ffn_up': 2, 'ffn_conv_w': 2, 'ffn_down': 1, 'ple_proj': 2, 'ple_gate_w': 1, 'even_w_in': 2,
              'even_w_out': 1, 'rwkv_w2': 2, 'rwkv_a2': 2, 'rwkv_g2': 2, 'gdn_conv_w': 2, 'odd_w_in': 2,
              'odd_w_out': 1, 'mamba_conv_w': 2, 'mamba_conv_b': 1, 'mamba_norm_g': 1, 'lru_conv_w': 2,
              'lru_conv_b': 1, 'lru_ba': 1, 'lru_bx': 1, 'lru_lambda': 1}
MATRICES = ['ffn_up', 'ffn_down', 'ple_proj', 'ple_gate_w', 'even_w_in', 'even_w_out', 'rwkv_w2', 'rwkv_a2',
            'rwkv_g2', 'odd_w_in', 'odd_w_out']
SMALL_SHARDED = [n for n in WEIGHTS if n in SHARD_AXIS and n not in MATRICES]
REPLICATED = [n for n in WEIGHTS if n not in SHARD_AXIS]


def _cparams(sem):
    return pltpu.CompilerParams(dimension_semantics=sem, vmem_limit_bytes=VMEM_LIMIT)


def _rup(n, m):
    return -(-n // m) * m


def _pick(n, cands):
    for c in cands:
        if n % c == 0:
            return c
    return n


_DIMS = {'nn': (((1,), (0,)), ((), ())), 'nt': (((1,), (1,)), ((), ())), 'tn': (((0,), (0,)), ((), ()))}


def _mm(a, b, mode, name):
    if mode == 'tn':
        K, M = a.shape
    else:
        M, K = a.shape
    N = b.shape[0] if mode == 'nt' else b.shape[1]
    tm = _pick(M, (1024, 512, 256, 128))
    tn = _pick(N, (1024, 512, 256, 128))
    room = MM_VMEM - 3 * tm * tn * 4
    per_k = 2 * (tm * a.dtype.itemsize + tn * b.dtype.itemsize)
    tk = max([t for t in range(LANES, K + 1, LANES) if K % t == 0 and t * per_k <= room] or [K])
    nk = K // tk

    def body(a_ref, b_ref, o_ref, acc_ref):
        k = pl.program_id(2)
        part = lax.dot_general(a_ref[...].astype(BF16), b_ref[...].astype(BF16), _DIMS[mode],
                               preferred_element_type=F32)

        @pl.when(k == 0)
        def _():
            acc_ref[...] = part

        @pl.when(k > 0)
        def _():
            acc_ref[...] += part

        @pl.when(k == nk - 1)
        def _():
            o_ref[...] = acc_ref[...]

    a_spec = pl.BlockSpec((tk, tm), lambda i, j, k: (k, i)) if mode == 'tn' else pl.BlockSpec((tm, tk), lambda i, j, k: (i, k))
    b_spec = pl.BlockSpec((tn, tk), lambda i, j, k: (j, k)) if mode == 'nt' else pl.BlockSpec((tk, tn), lambda i, j, k: (k, j))
    return pl.pallas_call(
        body, grid=(M // tm, N // tn, nk), in_specs=[a_spec, b_spec],
        out_specs=pl.BlockSpec((tm, tn), lambda i, j, k: (i, j)), out_shape=SDS((M, N), F32),
        scratch_shapes=[pltpu.VMEM((tm, tn), F32)], name=name,
        compiler_params=_cparams(("parallel", "parallel", "arbitrary")))(a, b)


def _matmul(x, w16, wz, name):
    @jax.custom_vjp
    def op(x, wz):
        return _mm(x.astype(BF16), w16, 'nn', name + "_f")

    def op_f(x, wz):
        x16 = x.astype(BF16)
        return _mm(x16, w16, 'nn', name + "_f"), x16

    def op_b(x16, g):
        g16 = g.astype(BF16)
        return _mm(g16, w16, 'nt', name + "_dx"), _mm(x16, g16, 'tn', name + "_dw")

    op.defvjp(op_f, op_b)
    return op(x, wz)


def _block_diag_pair(x, wa, wx, name):
    T = x.shape[0]
    nb, bd, _ = wa.shape
    dot = lambda a, b, mode: lax.dot_general(a.astype(BF16), b.astype(BF16), _DIMS[mode], preferred_element_type=F32)
    cols = pl.BlockSpec((T, bd), lambda n: (0, n))
    blk = pl.BlockSpec((1, bd, bd), lambda n: (n, 0, 0))

    def fwd_call(x, wa, wx):
        def body(x_ref, wa_ref, wx_ref, ra_ref, rx_ref):
            ra_ref[...] = dot(x_ref[...], wa_ref[0], 'nn')
            rx_ref[...] = dot(x_ref[...], wx_ref[0], 'nn')

        return pl.pallas_call(body, grid=(nb,), in_specs=[cols, blk, blk], out_specs=[cols, cols],
                              out_shape=[SDS(x.shape, F32)] * 2, name=name + "_f",
                              compiler_params=_cparams(("parallel",)))(x, wa, wx)

    def bwd_call(x, wa, wx, ga, gx):
        def body(x_ref, wa_ref, wx_ref, ga_ref, gx_ref, dx_ref, dwa_ref, dwx_ref):
            dx_ref[...] = dot(ga_ref[...], wa_ref[0], 'nt') + dot(gx_ref[...], wx_ref[0], 'nt')
            dwa_ref[0] = dot(x_ref[...], ga_ref[...], 'tn')
            dwx_ref[0] = dot(x_ref[...], gx_ref[...], 'tn')

        return pl.pallas_call(body, grid=(nb,), in_specs=[cols, blk, blk, cols, cols], out_specs=[cols, blk, blk],
                              out_shape=[SDS(x.shape, F32), SDS(wa.shape, F32), SDS(wx.shape, F32)], name=name + "_b",
                              compiler_params=_cparams(("parallel",)))(x, wa, wx, ga, gx)

    @jax.custom_vjp
    def op(x, wa, wx):
        return tuple(fwd_call(x, wa, wx))

    def op_f(x, wa, wx):
        return op(x, wa, wx), (x, wa, wx)

    def op_b(res, g):
        return tuple(bwd_call(*res, *g))

    op.defvjp(op_f, op_b)
    return op(x, wa, wx)


def _tile_op(name, fn, arrs, params, consts, by_rows, width=LANES):
    arrs, params, consts = tuple(arrs), tuple(params), tuple(consts)
    na, npar, nc = len(arrs), len(params), len(consts)
    T = arrs[0].shape[0]
    if by_rows:
        tile = _pick(T, (256, 128, 64, 32, 16, 8))
        grid = (T // tile,)
        arr_block = lambda a: (tile, a.shape[1])
        arr_spec = lambda a: pl.BlockSpec((tile, a.shape[1]), lambda i: (i, 0))
        par_block = lambda p: p.shape
        par_spec = lambda p: pl.BlockSpec(p.shape, lambda i: (0, 0))
    else:
        grid = (arrs[0].shape[1] // width,)
        arr_block = lambda a: (T, width)
        arr_spec = lambda a: pl.BlockSpec((T, width), lambda i: (0, i))
        par_block = lambda p: (p.shape[0], width)
        par_spec = lambda p: pl.BlockSpec((p.shape[0], width), lambda i: (0, i))
    const_spec = lambda c: pl.BlockSpec(c.shape, lambda i: (0,) * c.ndim)
    outs_sds = jax.eval_shape(fn, *[SDS(arr_block(a), F32) for a in arrs], *[SDS(par_block(p), F32) for p in params],
                              *[SDS(c.shape, c.dtype) for c in consts])
    out_widths = [o.shape[1] for o in outs_sds]
    nout = len(out_widths)
    if by_rows:
        out_shapes = [SDS((T, w), F32) for w in out_widths]
        out_specs = [pl.BlockSpec((tile, w), lambda i: (i, 0)) for w in out_widths]
    else:
        out_shapes = [SDS((T, grid[0] * w), F32) for w in out_widths]
        out_specs = [pl.BlockSpec((T, w), lambda i: (0, i)) for w in out_widths]

    def fwd_call(arrs, params):
        def body(*refs):
            outs = fn(*[r[...] for r in refs[:na + npar + nc]])
            for o_ref, o in zip(refs[na + npar + nc:], outs):
                o_ref[...] = o

        return pl.pallas_call(
            body, grid=grid, in_specs=[arr_spec(a) for a in arrs] + [par_spec(p) for p in params] + [const_spec(c) for c in consts],
            out_specs=out_specs, out_shape=out_shapes, name=name + "_f",
            compiler_params=_cparams(("parallel",)))(*arrs, *params, *consts)

    def bwd_call(arrs, params, cts):
        def body(*refs):
            ins = refs[:na + npar + nc + nout]
            outs = refs[na + npar + nc + nout:]
            av = [r[...] for r in ins[:na]]
            pv = [r[...] for r in ins[na:na + npar]]
            cv = [r[...] for r in ins[na + npar:na + npar + nc]]
            gv = [r[...] for r in ins[na + npar + nc:]]
            _, vjp = jax.vjp(lambda *t: fn(*t, *cv), *av, *pv)
            grads = vjp(tuple(gv))
            for o_ref, g in zip(outs[:na], grads[:na]):
                o_ref[...] = g
            if by_rows and npar:
                @pl.when(pl.program_id(0) == 0)
                def _():
                    for o_ref in outs[na:]:
                        o_ref[...] = jnp.zeros_like(o_ref)

                for o_ref, g in zip(outs[na:], grads[na:]):
                    o_ref[...] += g
            else:
                for o_ref, g in zip(outs[na:], grads[na:]):
                    o_ref[...] = g

        return pl.pallas_call(
            body, grid=grid,
            in_specs=[arr_spec(a) for a in arrs] + [par_spec(p) for p in params] + [const_spec(c) for c in consts] + out_specs,
            out_specs=[arr_spec(a) for a in arrs] + [par_spec(p) for p in params],
            out_shape=[SDS(a.shape, F32) for a in arrs] + [SDS(p.shape, F32) for p in params], name=name + "_b",
            compiler_params=_cparams(("arbitrary",) if by_rows else ("parallel",)))(*arrs, *params, *consts, *cts)

    @jax.custom_vjp
    def op(arrs, params):
        return tuple(fwd_call(arrs, params))

    def op_f(arrs, params):
        return op(arrs, params), (arrs, params)

    def op_b(res, cts):
        arrs, params = res
        g = bwd_call(arrs, params, cts)
        return tuple(g[:na]), tuple(g[na:])

    op.defvjp(op_f, op_b)
    return op(arrs, params)


def _rowwise(name, fn, arrs, params=(), consts=()):
    return _tile_op(name, fn, arrs, params, consts, True)


def _colwise(name, fn, arrs, params=(), width=LANES):
    return _tile_op(name, fn, arrs, params, (), False, width)


@functools.partial(jax.custom_vjp, nondiff_argnums=(1,))
def _shift(x, k):
    rows = lax.broadcasted_iota(jnp.int32, x.shape, 0)
    return jnp.where(rows >= k, pltpu.roll(x, k, 0), 0.0)


def _shift_f(x, k):
    return _shift(x, k), None


def _shift_b(k, _, g):
    n = g.shape[0]
    rows = lax.broadcasted_iota(jnp.int32, g.shape, 0)
    return (jnp.where(rows < n - k, pltpu.roll(g, n - k, 0), 0.0),)


_shift.defvjp(_shift_f, _shift_b)


def _causal_conv(x, w):
    K = w.shape[0]
    y = x * w[K - 1:K, :]
    for j in range(K - 1):
        y = y + _shift(x, K - 1 - j) * w[j:j + 1, :]
    return y


def _silu(x):
    return x * jax.nn.sigmoid(x)


def _softplus(x):
    return jnp.maximum(x, 0.0) + jnp.log1p(jnp.exp(-jnp.abs(x)))


def _split_cols(h, offs, widths):
    @jax.custom_vjp
    def op(h):
        return tuple(h[:, o:o + w] for o, w in zip(offs, widths))

    def op_f(h):
        return op(h), None

    def op_b(_, cts):
        parts, pos = [], 0
        T = cts[0].shape[0]
        for o, w, c in zip(offs, widths, cts):
            if o > pos:
                parts.append(jnp.zeros((T, o - pos), F32))
            parts.append(c)
            pos = o + w
        if pos < h.shape[1]:
            parts.append(jnp.zeros((T, h.shape[1] - pos), F32))
        return (jnp.concatenate(parts, axis=1),)

    op.defvjp(op_f, op_b)
    return op(h)


def _group_ones(width, group):
    g = jnp.arange(width) // group
    return (g[:, None] == g[None, :]).astype(F32)


def _layer_norm_rows(x, g, b, eps):
    mu = jnp.mean(x, axis=1, keepdims=True)
    var = jnp.mean(jnp.square(x - mu), axis=1, keepdims=True)
    return (x - mu) * lax.rsqrt(var + eps) * g + b


def _tri(L, strict=False):
    i = lax.broadcasted_iota(jnp.int32, (L, L), 0)
    j = lax.broadcasted_iota(jnp.int32, (L, L), 1)
    return (i > j) if strict else (i >= j)


def _cumsum_rows(x):
    H, L, _ = x.shape
    tri = jnp.broadcast_to(_tri(L).astype(F32)[None], (H, L, L))
    return jnp.einsum('hls,hsn->hln', tri, x, precision=HI)


def _col_to_row(c):
    L = c.shape[1]
    return jnp.sum(c * _tri_eye(L)[None], axis=1, keepdims=True)


def _row_to_col(r):
    N = r.shape[2]
    return jnp.sum(r * _tri_eye(N)[None], axis=2, keepdims=True)


def _scalar_col(t):
    return _row_to_col(t.reshape(t.shape[0], 1, t.shape[3]))


def _tri_eye(L):
    i = lax.broadcasted_iota(jnp.int32, (L, L), 0)
    j = lax.broadcasted_iota(jnp.int32, (L, L), 1)
    return (i == j).astype(F32)


def _unit_lower_inverse(n_strict):
    L = n_strict.shape[1]
    inv = _tri_eye(L)[None] + n_strict
    x = n_strict
    p = 2
    while p < L:
        x = jnp.einsum('hij,hjk->hik', x, x)
        inv = inv + jnp.einsum('hij,hjk->hik', inv, x)
        p *= 2
    return inv


def _rwkv_chunk(r, lw, k, v, a, b, h0):
    L = r.shape[1]
    mm = jnp.einsum
    cum = _cumsum_rows(lw)
    cum_l = jnp.sum(lw, axis=1, keepdims=True)
    e_neg = jnp.exp(-cum)
    rt, bt, kt, at = r * jnp.exp(cum), b * e_neg, k * e_neg, a * jnp.exp(cum - lw)
    to_end = jnp.exp(cum_l - cum)
    strict, incl = _tri(L, True)[None], _tri(L)[None]
    n = jnp.where(strict, mm('hld,hsd->hls', at, bt), 0.0)
    mk = jnp.where(strict, mm('hld,hsd->hls', at, kt), 0.0)
    u = mm('hls,hsv->hlv', _unit_lower_inverse(n), mm('hld,hdv->hlv', at, h0) + mm('hls,hsv->hlv', mk, v))
    y = (mm('hld,hdv->hlv', rt, h0) + mm('hls,hsv->hlv', jnp.where(incl, mm('hld,hsd->hls', rt, bt), 0.0), u)
         + mm('hls,hsv->hlv', jnp.where(incl, mm('hld,hsd->hls', rt, kt), 0.0), v))
    h1 = (_row_to_col(jnp.exp(cum_l)) * h0 + mm('hld,hlv->hdv', b * to_end, u) + mm('hld,hlv->hdv', k * to_end, v))
    return y, h1


def _gdn_chunk(q, k, v, beta, lg, h0):
    C, D = q.shape[1], q.shape[2]
    scale = D ** -0.5
    beta, lg = _scalar_col(beta), _scalar_col(lg)
    gc = _cumsum_rows(lg)
    gc_l = jnp.sum(lg, axis=1, keepdims=True)
    causal, strict = _tri(C)[None], _tri(C, True)[None]
    decay = jnp.exp(jnp.where(causal, gc - _col_to_row(gc), -jnp.inf))
    k_beta = k * beta
    m = jnp.where(strict, jnp.einsum('hcd,hsd->hcs', k_beta, k) * decay, 0.0)
    inv = _unit_lower_inverse(-m)
    e_gc = jnp.exp(gc)
    u = jnp.einsum('hcs,hsd->hcd', inv, v * beta)
    w = jnp.einsum('hcs,hsd->hcd', inv, k_beta * e_gc)
    attn = jnp.where(causal, jnp.einsum('hcd,hsd->hcs', q * scale, k) * decay, 0.0)
    v_new = u - jnp.einsum('hcd,hde->hce', w, h0)
    o = jnp.einsum('hcd,hde->hce', q * scale * e_gc, h0) + jnp.einsum('hcs,hse->hce', attn, v_new)
    h1 = h0 * jnp.exp(gc_l) + jnp.einsum('hcd,hce->hde', k * jnp.exp(gc_l - gc), v_new)
    return o, h1


def _ssd_chunk(xs, dt, aa, bm, cm, h0):
    H, L, _ = xs.shape
    dt, aa = _scalar_col(dt), _scalar_col(aa)
    x = xs * dt
    cs = _cumsum_rows(aa)
    cs_l = jnp.sum(aa, axis=1, keepdims=True)
    causal = _tri(L)[None]
    cb = jnp.einsum('gln,gsn->gls', cm, bm)
    wd = jnp.where(causal, cb * jnp.exp(jnp.where(causal, cs - _col_to_row(cs), -jnp.inf)), 0.0)
    cmb = jnp.broadcast_to(cm, (H,) + cm.shape[1:])
    bmb = jnp.broadcast_to(bm, (H,) + bm.shape[1:])
    y = jnp.einsum('hls,hsp->hlp', wd, x) + jnp.einsum('hln,hnp->hlp', cmb, h0) * jnp.exp(cs)
    h1 = jnp.exp(cs_l) * h0 + jnp.einsum('hln,hlp->hnp', bmb, x * jnp.exp(cs_l - cs))
    return y, h1


def _chunk_scan(name, fn, seqs, hb, L, state_shape, out_width):
    seqs = tuple(seqs)
    ns = len(seqs)
    H = max(s.shape[0] for s in seqs)
    T = max(s.shape[1] for s in seqs)
    nc, nh = T // L, H // hb
    lead = [hb if s.shape[0] == H else 1 for s in seqs]
    st_block = (hb,) + state_shape

    def seq_spec(s, l, imap):
        if s.ndim == 4:
            return pl.BlockSpec((l, 1, 1, L), lambda h, c: imap(h, c) + (0,))
        return pl.BlockSpec((l, L, s.shape[2]), imap)

    fmap = lambda h, c: (h, c, 0)
    rmap = lambda h, c: (h, nc - 1 - c, 0)

    def fwd_call(seqs):
        def body(*refs):
            y_ref, st_ref, carry = refs[ns], refs[ns + 1], refs[ns + 2]

            @pl.when(pl.program_id(1) == 0)
            def _():
                carry[...] = jnp.zeros_like(carry)

            h0 = carry[...]
            st_ref[0] = h0
            y, h1 = fn(*[r[...] for r in refs[:ns]], h0)
            y_ref[...] = y
            carry[...] = h1

        return pl.pallas_call(
            body, grid=(nh, nc), in_specs=[seq_spec(s, l, fmap) for s, l in zip(seqs, lead)],
            out_specs=[pl.BlockSpec((hb, L, out_width), fmap),
                       pl.BlockSpec((1,) + st_block, lambda h, c: (c, h) + (0,) * len(state_shape))],
            out_shape=[SDS((H, T, out_width), F32), SDS((nc, H) + state_shape, F32)],
            scratch_shapes=[pltpu.VMEM(st_block, F32)], name=name + "_f",
            compiler_params=_cparams(("parallel", "arbitrary")))(*seqs)

    def bwd_call(seqs, states, dy):
        def body(*refs):
            st_ref, dy_ref = refs[ns], refs[ns + 1]
            outs, carry = refs[ns + 2:2 * ns + 2], refs[2 * ns + 2]

            @pl.when(pl.program_id(1) == 0)
            def _():
                carry[...] = jnp.zeros_like(carry)

            _, vjp = jax.vjp(fn, *[r[...] for r in refs[:ns]], st_ref[0])
            grads = vjp((dy_ref[...], carry[...]))
            for o_ref, g in zip(outs, grads[:ns]):
                o_ref[...] = g
            carry[...] = grads[ns]

        return pl.pallas_call(
            body, grid=(nh, nc),
            in_specs=[seq_spec(s, l, rmap) for s, l in zip(seqs, lead)]
            + [pl.BlockSpec((1,) + st_block, lambda h, c: (nc - 1 - c, h) + (0,) * len(state_shape)),
               pl.BlockSpec((hb, L, out_width), rmap)],
            out_specs=[seq_spec(s, l, rmap) for s, l in zip(seqs, lead)],
            out_shape=[SDS(s.shape, F32) for s in seqs],
            scratch_shapes=[pltpu.VMEM(st_block, F32)], name=name + "_b",
            compiler_params=_cparams(("parallel", "arbitrary")))(*seqs, states, dy)

    @jax.custom_vjp
    def op(seqs):
        return fwd_call(seqs)[0]

    def op_f(seqs):
        y, states = fwd_call(seqs)
        return y, (seqs, states)

    def op_b(res, dy):
        seqs, states = res
        return (tuple(bwd_call(seqs, states, dy)),)

    op.defvjp(op_f, op_b)
    return op(seqs)


def _lru_scan_call(a, u, h, reverse, name):
    T, C = a.shape
    cw = _pick(C, (1024, 512, 256, 128))
    tt = _pick(T, (512, 256, 128, 64, 32, 16, 8))
    nt, ng = T // tt, tt // 8
    sub = lambda: lax.broadcasted_iota(jnp.int32, (8, cw), 0)
    first = lambda: pl.program_id(1) == 0

    def fwd_body(a_ref, u_ref, h_ref, carry_ref):
        @pl.when(first())
        def _():
            carry_ref[...] = jnp.zeros_like(carry_ref)

        def group(i, carry):
            r0 = pl.multiple_of(i * 8, 8)
            ab, ub = a_ref[pl.ds(r0, 8), :], u_ref[pl.ds(r0, 8), :]
            out = jnp.zeros((8, cw), F32)
            for j in range(8):
                carry = ab[j:j + 1, :] * carry + ub[j:j + 1, :]
                out = jnp.where(sub() == j, carry, out)
            h_ref[pl.ds(r0, 8), :] = out
            return carry

        carry_ref[...] = lax.fori_loop(0, ng, group, carry_ref[...])

    def bwd_body(a_ref, u_ref, h_ref, hp_ref, g_ref, da_ref, cg_ref, ca_ref):
        @pl.when(first())
        def _():
            cg_ref[...] = jnp.zeros_like(cg_ref)
            ca_ref[...] = jnp.zeros_like(ca_ref)

        h_before = jnp.where(pl.program_id(1) < nt - 1, hp_ref[7:8, :], 0.0)

        def group(i, carry):
            g_next, a_next = carry
            gi = ng - 1 - i
            r0 = pl.multiple_of(gi * 8, 8)
            rp = pl.multiple_of(jnp.maximum(gi - 1, 0) * 8, 8)
            ab, ub, hb = a_ref[pl.ds(r0, 8), :], u_ref[pl.ds(r0, 8), :], h_ref[pl.ds(r0, 8), :]
            h_last_prev = jnp.where(gi > 0, h_ref[pl.ds(rp, 8), :][7:8, :], h_before)
            g_out = jnp.zeros((8, cw), F32)
            da_out = jnp.zeros((8, cw), F32)
            for j in range(7, -1, -1):
                g_next = ub[j:j + 1, :] + a_next * g_next
                a_next = ab[j:j + 1, :]
                h_prev = hb[j - 1:j, :] if j > 0 else h_last_prev
                g_out = jnp.where(sub() == j, g_next, g_out)
                da_out = jnp.where(sub() == j, g_next * h_prev, da_out)
            g_ref[pl.ds(r0, 8), :] = g_out
            da_ref[pl.ds(r0, 8), :] = da_out
            return g_next, a_next

        cg_ref[...], ca_ref[...] = lax.fori_loop(0, ng, group, (cg_ref[...], ca_ref[...]))

    row = pltpu.VMEM((1, cw), F32)
    if not reverse:
        spec = pl.BlockSpec((tt, cw), lambda i, t: (t, i))
        return pl.pallas_call(fwd_body, grid=(C // cw, nt), in_specs=[spec, spec], out_specs=spec,
                              out_shape=SDS((T, C), F32), scratch_shapes=[row], name=name,
                              compiler_params=_cparams(("parallel", "arbitrary")))(a, u)
    spec = pl.BlockSpec((tt, cw), lambda i, t: (nt - 1 - t, i))
    before = pl.BlockSpec((8, cw), lambda i, t: (jnp.maximum((nt - 1 - t) * ng - 1, 0), i))
    return pl.pallas_call(bwd_body, grid=(C // cw, nt), in_specs=[spec, spec, spec, before], out_specs=[spec, spec],
                          out_shape=[SDS((T, C), F32), SDS((T, C), F32)], scratch_shapes=[row, row], name=name,
                          compiler_params=_cparams(("parallel", "arbitrary")))(a, u, h, h)


@jax.custom_vjp
def _lru_scan(a, u):
    return _lru_scan_call(a, u, None, False, "lru_scan_f")


def _lru_scan_f(a, u):
    h = _lru_scan(a, u)
    return h, (a, h)


def _lru_scan_b(res, dh):
    a, h = res
    g, da = _lru_scan_call(a, dh, h, True, "lru_scan_b")
    return da, g


_lru_scan.defvjp(_lru_scan_f, _lru_scan_b)


def _heads_major(x, nheads):
    T, W = x.shape
    return jnp.transpose(x.reshape(T, nheads, W // nheads), (1, 0, 2))


def _tokens_major(x):
    H, T, N = x.shape
    return jnp.transpose(x, (1, 0, 2)).reshape(T, H * N)


def _pad_cols(w, offs, widths, total):
    parts, pos, src = [], 0, 0
    for o, wd in zip(offs, widths):
        if o > pos:
            parts.append(jnp.zeros((w.shape[0], o - pos), w.dtype))
        parts.append(w[:, src:src + wd])
        src += wd
        pos = o + wd
    if pos < total:
        parts.append(jnp.zeros((w.shape[0], total - pos), w.dtype))
    return jnp.concatenate(parts, axis=1)


def _unpad_cols(w, offs, widths):
    return jnp.concatenate([w[:, o:o + wd] for o, wd in zip(offs, widths)], axis=1)


def _aligned_layout(widths):
    offs, pos = [], 0
    for w in widths:
        offs.append(pos)
        pos += _rup(w, LANES)
    return offs, _rup(pos, 512)


def _pad_lanes(v, n):
    return jnp.pad(v, ((0, 0), (0, n - v.shape[1])))


def _even_mixer(x, q, wz, li):
    T = x.shape[0]
    ah, an = q['rwkv_r_k'].shape
    aw = ah * an
    bh, bn = q['gdn_A_log'].shape[0], q['gdn_norm_g'].shape[0]
    bw = bh * bn
    lw_, la_, lg_ = q['rwkv_w2'].shape[0], q['rwkv_a2'].shape[0], q['rwkv_g2'].shape[0]
    widths = [aw, aw, aw, lw_, la_, lg_, bw, bw, bw, bw, bh, bh]
    offs, total = _aligned_layout(widths)
    pw = [_rup(w, LANES) for w in widths]
    hcols = _matmul(x, _pad_cols(q['even_w_in'], offs, widths, total), wz['even_w_in'], f"even_in{li}")
    a_w = offs[6]
    a_cols, bq, bk, bv, bz, beta_raw, alpha_raw = _split_cols(hcols, [0] + offs[6:], [a_w] + pw[6:])

    mu = _pad_cols(q['rwkv_mu'][None], offs[:6], widths[:6], a_w)
    (xs,) = _colwise(f"rwkv_shift{li}", lambda h, m: (h + (_shift(h, 1) - h) * m,), [a_cols], [mu])
    r, k, v, w_lo, a_lo, g_lo = _split_cols(xs, offs[:6], pw[:6])
    tw, sg = _rowwise(f"rwkv_lora_act{li}", lambda w, g: (jnp.tanh(w), jax.nn.sigmoid(g)), [w_lo, g_lo])
    pad_rows = lambda w, n: jnp.pad(w, ((0, n - w.shape[0]), (0, 0)))
    wl = _matmul(tw, pad_rows(q['rwkv_w2'], pw[3]), wz['rwkv_w2'], f"rwkv_w2{li}")
    al = _matmul(a_lo, pad_rows(q['rwkv_a2'], pw[4]), wz['rwkv_a2'], f"rwkv_a2{li}")
    g = _matmul(sg, pad_rows(q['rwkv_g2'], pw[5]), wz['rwkv_g2'], f"rwkv_g2{li}")
    ones_a = _group_ones(aw, an)

    def pre(k, wl, al, w0, a0, k_k, k_a, ones):
        lw = -jnp.exp(-_softplus(-(w0 + wl)) - 0.5)
        a = jax.nn.sigmoid(a0 + al)
        kk = k * k_k
        kk = kk * lax.rsqrt(jnp.dot(kk * kk, ones) + L2_EPS)
        return lw, k * (1.0 + (a - 1.0) * k_a), -kk, kk * a

    lw, k2, sa, sb = _rowwise(f"rwkv_pre{li}", pre, [k, wl, al],
                              [q['rwkv_w0'][None], q['rwkv_a0'][None], q['rwkv_k_k'][None], q['rwkv_k_a'][None]], [ones_a])
    hm = lambda t: _heads_major(t, ah)
    out = _chunk_scan(f"rwkv_scan{li}", _rwkv_chunk, [hm(r), hm(lw), hm(k2), hm(v), hm(sa), hm(sb)],
                      min(ah, SCAN_HEADS), min(RWKV_CHUNK, T), (an, an), an)
    out = _tokens_major(out)

    def post(out, r, k2, v, g, gn_g, gn_b, r_k, ones):
        mean = jnp.dot(out, ones) * (1.0 / an)
        cen = out - mean
        var = jnp.dot(cen * cen, ones) * (1.0 / an)
        normed = cen * lax.rsqrt(var + A_GN_EPS) * gn_g + gn_b
        bonus = jnp.dot(r * k2 * r_k, ones) * v
        return ((normed + bonus) * g,)

    flat = lambda t: t.reshape(1, -1)
    (ya,) = _rowwise(f"rwkv_post{li}", post, [out, r, k2, v, g],
                     [flat(q['rwkv_gn_g']), flat(q['rwkv_gn_b']), flat(q['rwkv_r_k'])], [ones_a])

    cw = q['gdn_conv_w']

    def conv_l2(x, w):
        y = _silu(_causal_conv(x, w))
        return (y * lax.rsqrt(jnp.sum(y * y, axis=1, keepdims=True) + L2_EPS),)

    if bn == LANES:
        (gq,) = _colwise(f"gdn_conv_q{li}", conv_l2, [bq], [cw[:, :bw]])
        (gk,) = _colwise(f"gdn_conv_k{li}", conv_l2, [bk], [cw[:, bw:2 * bw]])
    else:
        raise NotImplementedError("gated DeltaNet head width must equal the lane count")
    (gv,) = _colwise(f"gdn_conv_v{li}", lambda x, w: (_silu(_causal_conv(x, w)),), [bv], [cw[:, 2 * bw:]])

    def gates(beta_raw, alpha_raw, a_log, dt_bias):
        return jax.nn.sigmoid(beta_raw), -jnp.exp(a_log) * _softplus(alpha_raw + dt_bias)

    beta, lg = _rowwise(f"gdn_gates{li}", gates, [beta_raw, alpha_raw],
                        [_pad_lanes(q['gdn_A_log'][None], pw[10]), _pad_lanes(q['gdn_dt_bias'][None], pw[11])])
    gl = min(GDN_CHUNK, T)
    col = lambda t: jnp.transpose(t[:, :bh]).reshape(bh, T // gl, 1, gl)
    hmb = lambda t: _heads_major(t, bh)
    o = _chunk_scan(f"gdn_scan{li}", _gdn_chunk, [hmb(gq), hmb(gk), hmb(gv), col(beta), col(lg)],
                    min(bh, SCAN_HEADS), gl, (bn, bn), bn)
    o = _tokens_major(o)
    ones_b = _group_ones(bw, bn)

    def gdn_post(o, z, ng, ones):
        ms = jnp.dot(o * o, ones) * (1.0 / bn)
        return (o * lax.rsqrt(ms + RMS_EPS) * ng * _silu(z),)

    (yb,) = _rowwise(f"gdn_post{li}", gdn_post, [o, bz], [jnp.tile(q['gdn_norm_g'][None], (1, bh))], [ones_b])
    return _matmul(jnp.concatenate([ya, yb], axis=1), q['even_w_out'], wz['even_w_out'], f"even_out{li}")


def _odd_mixer(x, q, wz, li):
    T = x.shape[0]
    ch = q['mamba_dt_bias'].shape[0]
    cwid = q['mamba_norm_g'].shape[0]
    cp = cwid // ch
    xbc_w = q['mamba_conv_w'].shape[1]
    cn = (xbc_w - cwid) // (2 * C_GROUPS)
    dw = q['lru_lambda'].shape[0]
    widths = [cwid, xbc_w, ch, dw, dw]
    offs, total = _aligned_layout(widths)
    pw = [_rup(w, LANES) for w in widths]
    hcols = _matmul(x, _pad_cols(q['odd_w_in'], offs, widths, total), wz['odd_w_in'], f"odd_in{li}")
    z, xbc, dt_raw, y_br, x_br = _split_cols(hcols, offs, pw)

    (xbc_c,) = _colwise(f"mamba_conv{li}", lambda x, w, b: (_silu(_causal_conv(x, w) + b),), [xbc],
                        [q['mamba_conv_w'], q['mamba_conv_b'][None]])
    gn = C_GROUPS * cn
    xs, bm, cm = _split_cols(xbc_c, [0, cwid, cwid + gn], [cwid, gn, gn])

    def dts(dt_raw, dt_bias, a_log):
        dt = _softplus(dt_raw + dt_bias)
        return dt, dt * (-jnp.exp(a_log))

    dt, aa = _rowwise(f"mamba_dt{li}", dts, [dt_raw],
                      [_pad_lanes(q['mamba_dt_bias'][None], pw[2]), _pad_lanes(q['mamba_A_log'][None], pw[2])])
    sl = min(SSD_CHUNK, T)
    col = lambda t: jnp.transpose(t[:, :ch]).reshape(ch, T // sl, 1, sl)
    y = _chunk_scan(f"ssd_scan{li}", _ssd_chunk,
                    [_heads_major(xs, ch), col(dt), col(aa), _heads_major(bm, C_GROUPS), _heads_major(cm, C_GROUPS)],
                    ch // C_GROUPS, sl, (cn, cp), cp)
    y = _tokens_major(y)
    gsz = cwid // C_GROUPS

    def mamba_post(y, xs, z, d, ng):
        yy = (y + xs * d) * _silu(z)
        lane = lax.broadcasted_iota(jnp.int32, yy.shape, 1)
        ms = jnp.zeros_like(yy)
        for gi in range(C_GROUPS):
            sel = (lane >= gi * gsz) & (lane < (gi + 1) * gsz)
            ms = jnp.where(sel, jnp.sum(jnp.where(sel, yy * yy, 0.0), axis=1, keepdims=True) * (1.0 / gsz), ms)
        return (yy * lax.rsqrt(ms + RMS_EPS) * ng,)

    (yc,) = _rowwise(f"mamba_post{li}", mamba_post, [y, xs, z],
                     [jnp.repeat(q['mamba_D'], cp)[None], q['mamba_norm_g'][None]])

    (xc,) = _colwise(f"lru_conv{li}", lambda x, w, b: (_causal_conv(x, w) + b,), [x_br],
                     [q['lru_conv_w'], q['lru_conv_b'][None]])
    ra, ia = _block_diag_pair(xc, q['lru_wa'], q['lru_wx'], f"lru_gates{li}")

    def lru_pre(ra, ia, xc, ba, bx, lam):
        r = jax.nn.sigmoid(ra + ba)
        i = jax.nn.sigmoid(ia + bx)
        log_a = LRU_C * r * (-_softplus(-lam))
        t = 2.0 * log_a
        series = t * (1.0 + t * (0.5 + t * (1.0 / 6.0 + t * (1.0 / 24.0 + t * (1.0 / 120.0 + t * (1.0 / 720.0))))))
        expm1 = jnp.where(t > -0.2, series, jnp.exp(t) - 1.0)
        return jnp.exp(log_a), jnp.sqrt(-expm1) * (i * xc)

    a, u = _rowwise(f"lru_pre{li}", lru_pre, [ra, ia, xc], [q['lru_ba'][None], q['lru_bx'][None], q['lru_lambda'][None]])
    h = _lru_scan(a, u)
    (yd,) = _rowwise(f"lru_post{li}", lambda h, y: (h * jax.nn.gelu(y),), [h, y_br])
    return _matmul(jnp.concatenate([yc, yd], axis=1), q['odd_w_out'], wz['odd_w_out'], f"odd_out{li}")


def _forward(x, wz, sp, p, w16, depth):
    alpha = (2.0 * depth) ** 0.25
    for i in range(depth):
        j = i // 2
        even = i % 2 == 0
        names = [n for n in WEIGHTS if n.startswith(('rwkv_', 'gdn_', 'even_') if even else ('mamba_', 'lru_', 'odd_'))]
        q = {n: (w16[n][j] if n in MATRICES else sp[n][j]) for n in names}
        wzl = {n: wz[f"{n}.{j}"] for n in names if n in MATRICES}
        y = (_even_mixer if even else _odd_mixer)(x, q, wzl, i)

        def ln_res(x, y, g, b):
            return (_layer_norm_rows(alpha * x + y, g, b, LN_EPS),)

        (h,) = _rowwise(f"ln1_{i}", ln_res, [x, y], [sp['ln1_g'][i][None], sp['ln1_b'][i][None]])
        dff = w16['ffn_up'].shape[2] // 2
        gate = _matmul(h, w16['ffn_up'][i][:, :dff], wz[f"ffn_up.{i}"], f"ffn_gate{i}")
        val = _matmul(h, w16['ffn_up'][i][:, dff:], wz[f"ffn_up_val.{i}"], f"ffn_val{i}")
        cw, cb = sp['ffn_conv_w'][i], sp['ffn_conv_b'][i][None]

        def ffn_act(gate, val, wg, wv, bg, bv):
            return (_silu(_causal_conv(gate, wg) + bg) * (_causal_conv(val, wv) + bv),)

        (act,) = _colwise(f"ffn_act{i}", ffn_act, [gate, val], [cw[:, :dff], cw[:, dff:], cb[:, :dff], cb[:, dff:]])
        f = _matmul(act, w16['ffn_down'][i], wz[f"ffn_down.{i}"], f"ffn_down{i}")
        (h2,) = _rowwise(f"ln2_{i}", ln_res, [h, f], [sp['ln2_g'][i][None], sp['ln2_b'][i][None]])
        e0 = _matmul(p[i], w16['ple_proj'][i], wz[f"ple_proj.{i}"], f"ple_proj{i}")
        gl = _matmul(h2, w16['ple_gate_w'][i], wz[f"ple_gate_w.{i}"], f"ple_gate{i}")

        def ple(h2, gl, e0, gb, ng):
            e = e0 * lax.rsqrt(jnp.mean(e0 * e0, axis=1, keepdims=True) + RMS_EPS) * ng
            return (h2 + jax.nn.sigmoid(gl + gb) * e,)

        (x,) = _rowwise(f"ple{i}", ple, [h2, gl, e0], [sp['ple_gate_b'][i][None], sp['ple_norm_g'][i][None]])
    return x


def _loss_head(y, target):
    T, D = y.shape
    tile = _pick(T, (256, 128, 64, 32, 16, 8))

    def body(y_ref, t_ref, dy_ref, l_ref):
        err = y_ref[...] - t_ref[...]
        dy_ref[...] = err * (1.0 / D)

        @pl.when(pl.program_id(0) == 0)
        def _():
            l_ref[...] = jnp.zeros_like(l_ref)

        l_ref[...] += jnp.sum(jnp.sum(err * err, axis=1, keepdims=True), axis=0, keepdims=True) * (0.5 / D) + jnp.zeros_like(l_ref)

    spec = pl.BlockSpec((tile, D), lambda i: (i, 0))
    dy, l = pl.pallas_call(body, grid=(T // tile,), in_specs=[spec, spec],
                           out_specs=[spec, pl.BlockSpec((8, LANES), lambda i: (0, 0))],
                           out_shape=[SDS((T, D), F32), SDS((8, LANES), F32)], name="loss_head",
                           compiler_params=_cparams(("arbitrary",)))(y, target)
    return l[0, 0], dy


def _my_index():
    return 4 * lax.axis_index("x") + 2 * lax.axis_index("y") + lax.axis_index("c")


def _hbm_specs(n):
    return [pl.BlockSpec(memory_space=pl.ANY)] * n


def _all_gather(blocks, name):
    blocks = tuple(blocks)
    n = len(blocks)
    half = [b.shape[0] // 2 for b in blocks]

    def body(*refs):
        ins, outs = refs[:n], refs[n:2 * n]
        send_sems, recv_sems, local_sems = refs[2 * n:]
        x, y, c = lax.axis_index("x"), lax.axis_index("y"), lax.axis_index("c")
        me, sibling, other = (x, y, c), (x, y, 1 - c), 1 - c
        xn, yn, dg = (1 - x, y), (x, 1 - y), (1 - x, 1 - y)

        def slot(i, px, py, pc, h=None):
            ref = outs[i].at[4 * px + 2 * py + pc]
            return ref if h is None else ref.at[pl.ds(h * half[i], half[i])]

        def copy(i, k, blk, to, h=None, src=None):
            dst = slot(i, *blk, h)
            return pltpu.make_async_remote_copy(
                src_ref=dst if src is None else src, dst_ref=dst, send_sem=send_sems.at[9 * i + k],
                recv_sem=recv_sems.at[9 * i + k], device_id=to, device_id_type=MESH)

        mine = [pltpu.make_async_copy(ins[i], slot(i, *me), local_sems.at[i]) for i in range(n)]
        sent = []
        for i in range(n):
            sent += [copy(i, 1, me, (*xn, c), src=ins[i]), copy(i, 2, me, (*yn, c), src=ins[i])]
        sent += [copy(i, 0, me, sibling, src=ins[i]) for i in range(n)]
        for cp in mine + sent:
            cp.start()

        def after(i, k_in, blk, h_in, forwards):
            copy(i, k_in, blk, me, h_in).wait_recv()
            for k_out, to, h_out in forwards:
                sent.append(copy(i, k_out, blk, to, h_out))
                sent[-1].start()

        for i in range(n):
            after(i, 1, (*xn, c), None, [(3, (*yn, c), 0), (5, sibling, None)])
        for i in range(n):
            after(i, 2, (*yn, c), None, [(4, (*xn, c), 1), (6, sibling, None)])
        for i in range(n):
            after(i, 3, (*dg, c), 0, [(7, sibling, 0)])
        for i in range(n):
            after(i, 4, (*dg, c), 1, [(8, sibling, 1)])
        for i in range(n):
            copy(i, 0, sibling, me).wait_recv()
            copy(i, 5, (*xn, other), me).wait_recv()
            copy(i, 6, (*yn, other), me).wait_recv()
            copy(i, 7, (*dg, other), me, 0).wait_recv()
            copy(i, 8, (*dg, other), me, 1).wait_recv()
        for cp in sent:
            cp.wait_send()
        for cp in mine:
            cp.wait()

    return pl.pallas_call(
        body, out_shape=[SDS((N_DEV,) + b.shape, b.dtype) for b in blocks], in_specs=_hbm_specs(n), out_specs=_hbm_specs(n),
        scratch_shapes=[pltpu.SemaphoreType.DMA((9 * n,)), pltpu.SemaphoreType.DMA((9 * n,)), pltpu.SemaphoreType.DMA((n,))],
        name=name)(*blocks)


def _sibling_exchange(parts, name):
    parts = tuple(parts)
    n = len(parts)

    def body(*refs):
        ins, outs = refs[:n], refs[n:2 * n]
        send_sems, recv_sems = refs[2 * n:]
        x, y, c = lax.axis_index("x"), lax.axis_index("y"), lax.axis_index("c")
        copies = [pltpu.make_async_remote_copy(
            src_ref=ins[i].at[q, 1 - c], dst_ref=outs[i].at[q], send_sem=send_sems.at[4 * i + q],
            recv_sem=recv_sems.at[4 * i + q], device_id=(x, y, 1 - c), device_id_type=MESH)
            for i in range(n) for q in range(4)]
        for cp in copies:
            cp.start()
        for cp in copies:
            cp.wait_recv()
        for cp in copies:
            cp.wait_send()

    return pl.pallas_call(
        body, out_shape=[SDS((4,) + p.shape[2:], p.dtype) for p in parts], in_specs=_hbm_specs(n), out_specs=_hbm_specs(n),
        scratch_shapes=[pltpu.SemaphoreType.DMA((4 * n,)), pltpu.SemaphoreType.DMA((4 * n,))], name=name)(*parts)


def _chip_exchange(sends, name):
    sends = tuple(sends)
    n = len(sends)

    def body(*refs):
        ins, outs = refs[:n], refs[n:2 * n]
        send_sems, recv_sems = refs[2 * n:]
        x, y, c = lax.axis_index("x"), lax.axis_index("y"), lax.axis_index("c")
        chips = [(1 - x, y), (x, 1 - y), (1 - x, 1 - y)]
        copies = [pltpu.make_async_remote_copy(
            src_ref=ins[i].at[k], dst_ref=outs[i].at[k], send_sem=send_sems.at[3 * i + k],
            recv_sem=recv_sems.at[3 * i + k], device_id=(*chip, c), device_id_type=MESH)
            for k, chip in enumerate(chips) for i in range(n)]
        for cp in copies:
            cp.start()
        for cp in copies:
            cp.wait_recv()
        for cp in copies:
            cp.wait_send()

    return pl.pallas_call(
        body, out_shape=[SDS(s.shape, s.dtype) for s in sends], in_specs=_hbm_specs(n), out_specs=_hbm_specs(n),
        scratch_shapes=[pltpu.SemaphoreType.DMA((3 * n,)), pltpu.SemaphoreType.DMA((3 * n,))], name=name)(*sends)


def _place_ids():
    x, y, c = lax.axis_index("x"), lax.axis_index("y"), lax.axis_index("c")
    return jnp.stack([c, 2 * x + y, 2 * (1 - x) + y, 2 * x + (1 - y), 2 * (1 - x) + (1 - y)]).astype(jnp.int32)


def _row_tile(rows, cols):
    best = None
    for t in range(16, rows + 1, 16):
        if rows % t == 0 and t * cols <= 256 * 1024:
            best = t
    return best or rows


def _chip_partials(part, recv_a, ids, dtype, name):
    _, _, R, C = part.shape
    tr = _row_tile(R, C)

    def body(ids_ref, p_ref, a_ref, o_ref):
        o_ref[...] = (p_ref[...] + a_ref[...]).astype(dtype)

    return pl.pallas_call(
        body, out_shape=SDS((3, R, C), dtype),
        grid_spec=pltpu.PrefetchScalarGridSpec(
            num_scalar_prefetch=1, grid=(3, R // tr),
            in_specs=[pl.BlockSpec((None, None, tr, C), lambda s, i, ids: (ids[2 + s], ids[0], i, 0)),
                      pl.BlockSpec((None, tr, C), lambda s, i, ids: (ids[2 + s], i, 0))],
            out_specs=pl.BlockSpec((None, tr, C), lambda s, i, ids: (s, i, 0))),
        name=name, compiler_params=_cparams(("parallel", "parallel")))(ids, part, recv_a)


def _adamw(part, recv_a, recv_c, ids, w, m, v, name):
    R, C = w.shape
    tr = _row_tile(R, C)
    c1 = 1.0 / (1.0 - ADAM_B1 ** ADAM_STEP)
    c2 = 1.0 / (1.0 - ADAM_B2 ** ADAM_STEP)

    def body(ids_ref, p_ref, a_ref, c_ref, w_ref, m_ref, v_ref, g_ref, d_ref, nm_ref, nv_ref):
        g = p_ref[...] + a_ref[...]
        for k in range(3):
            g = g + c_ref[k].astype(F32)
        nm = ADAM_B1 * m_ref[...] + (1.0 - ADAM_B1) * g
        nv = ADAM_B2 * v_ref[...] + (1.0 - ADAM_B2) * jnp.square(g)
        g_ref[...] = g
        nm_ref[...] = nm
        nv_ref[...] = nv
        d_ref[...] = -ADAM_LR * ((nm * c1) / (jnp.sqrt(nv * c2) + ADAM_EPS) + ADAM_WD * w_ref[...])

    spec = pl.BlockSpec((tr, C), lambda i, ids: (i, 0))
    return pl.pallas_call(
        body, out_shape=[SDS((R, C), F32)] * 4,
        grid_spec=pltpu.PrefetchScalarGridSpec(
            num_scalar_prefetch=1, grid=(R // tr,),
            in_specs=[pl.BlockSpec((None, None, tr, C), lambda i, ids: (ids[1], ids[0], i, 0)),
                      pl.BlockSpec((None, tr, C), lambda i, ids: (ids[1], i, 0)),
                      pl.BlockSpec((3, tr, C), lambda i, ids: (0, i, 0)), spec, spec, spec],
            out_specs=[spec] * 4),
        name=name, compiler_params=_cparams(("parallel",)))(ids, part, recv_a, recv_c, w, m, v)


def _to_flat(vecs, quantum):
    flat = jnp.concatenate([v.reshape(-1) for v in vecs])
    n = _rup(flat.shape[0], quantum * FLAT_COLS)
    return jnp.pad(flat, (0, n - flat.shape[0])).reshape(n // FLAT_COLS, FLAT_COLS)


def _gathered_to_full(g, names, blocks):
    flat = g.reshape(N_DEV, -1)
    out, pos = {}, 0
    for n in names:
        shp = blocks[n]
        size = math.prod(shp)
        out[n] = _blocks_to_full(flat[:, pos:pos + size].reshape((N_DEV,) + shp), SHARD_AXIS[n])
        pos += size
    return out


def _blocks_to_full(g, ax):
    shp = g.shape[1:]
    return jnp.moveaxis(g, 0, ax).reshape(shp[:ax] + (N_DEV * shp[ax],) + shp[ax + 1:])


def _full_to_blocks(g, ax, ndev=N_DEV):
    shp = g.shape
    t = g.reshape(shp[:ax] + (ndev, shp[ax] // ndev) + shp[ax + 1:])
    return jnp.moveaxis(t, ax, 0)


def kernel(x, p, ln1_g, ln1_b, ln2_g, ln2_b, ffn_up, ffn_conv_w, ffn_conv_b, ffn_down, ple_proj, ple_norm_g, ple_gate_w, ple_gate_b, even_w_in, even_w_out, rwkv_mu, rwkv_w0, rwkv_w2, rwkv_a0, rwkv_a2, rwkv_g2, rwkv_k_k, rwkv_k_a, rwkv_r_k, rwkv_gn_g, rwkv_gn_b, gdn_conv_w, gdn_A_log, gdn_dt_bias, gdn_norm_g, odd_w_in, odd_w_out, mamba_conv_w, mamba_conv_b, mamba_dt_bias, mamba_A_log, mamba_D, mamba_norm_g, lru_conv_w, lru_conv_b, lru_wa, lru_ba, lru_wx, lru_bx, lru_lambda, loss_target, m_ln1_g, m_ln1_b, m_ln2_g, m_ln2_b, m_ffn_up, m_ffn_conv_w, m_ffn_conv_b, m_ffn_down, m_ple_proj, m_ple_norm_g, m_ple_gate_w, m_ple_gate_b, m_even_w_in, m_even_w_out, m_rwkv_mu, m_rwkv_w0, m_rwkv_w2, m_rwkv_a0, m_rwkv_a2, m_rwkv_g2, m_rwkv_k_k, m_rwkv_k_a, m_rwkv_r_k, m_rwkv_gn_g, m_rwkv_gn_b, m_gdn_conv_w, m_gdn_A_log, m_gdn_dt_bias, m_gdn_norm_g, m_odd_w_in, m_odd_w_out, m_mamba_conv_w, m_mamba_conv_b, m_mamba_dt_bias, m_mamba_A_log, m_mamba_D, m_mamba_norm_g, m_lru_conv_w, m_lru_conv_b, m_lru_wa, m_lru_ba, m_lru_wx, m_lru_bx, m_lru_lambda, v_ln1_g, v_ln1_b, v_ln2_g, v_ln2_b, v_ffn_up, v_ffn_conv_w, v_ffn_conv_b, v_ffn_down, v_ple_proj, v_ple_norm_g, v_ple_gate_w, v_ple_gate_b, v_even_w_in, v_even_w_out, v_rwkv_mu, v_rwkv_w0, v_rwkv_w2, v_rwkv_a0, v_rwkv_a2, v_rwkv_g2, v_rwkv_k_k, v_rwkv_k_a, v_rwkv_r_k, v_rwkv_gn_g, v_rwkv_gn_b, v_gdn_conv_w, v_gdn_A_log, v_gdn_dt_bias, v_gdn_norm_g, v_odd_w_in, v_odd_w_out, v_mamba_conv_w, v_mamba_conv_b, v_mamba_dt_bias, v_mamba_A_log, v_mamba_D, v_mamba_norm_g, v_lru_conv_w, v_lru_conv_b, v_lru_wa, v_lru_ba, v_lru_wx, v_lru_bx, v_lru_lambda):
    args = locals()
    w = {n: args[n] for n in WEIGHTS}
    m = {n: args["m_" + n] for n in WEIGHTS}
    v = {n: args["v_" + n] for n in WEIGHTS}
    depth = ln1_g.shape[0]
    me = _my_index()
    blocks = {n: w[n].shape for n in WEIGHTS}

    as_rows = lambda t: t.reshape(-1, t.shape[-1])
    small = _to_flat([w[n] for n in SMALL_SHARDED], 16)
    gathered = _all_gather([as_rows(w[n].astype(BF16)) for n in MATRICES] + [small], "gather_params")
    w16 = {n: _blocks_to_full(g.reshape((N_DEV,) + blocks[n]), SHARD_AXIS[n]) for n, g in zip(MATRICES, gathered)}
    sp = _gathered_to_full(gathered[-1], SMALL_SHARDED, blocks)
    sp.update({n: w[n] for n in REPLICATED})

    lay = _matrix_layouts(w16, sp)
    wz = {k: jnp.zeros(shape, F32) for k, shape in lay['padded'].items()}
    y, vjp = jax.vjp(lambda x_, wz_, sp_: _forward(x_, wz_, sp_, p[:, 0], w16, depth), x[0], wz, sp)
    loss_local, dy = _loss_head(y, loss_target[0])
    dx, dwz, dsp = vjp(dy)
    loss = lax.psum(loss_local, ("x", "y", "c"))
    gfull = dict(dsp)
    gblocks = {}
    for n in MATRICES:
        layers = range(w16[n].shape[0])
        if n == 'ffn_up':
            halves = [jnp.stack([dwz[f"{k}.{j}"] for j in layers]) for k in ("ffn_up", "ffn_up_val")]
            gblocks[n] = jnp.concatenate([_full_to_blocks(h, SHARD_AXIS[n], N_DEV // 2) for h in halves], axis=0)
        else:
            gblocks[n] = _full_to_blocks(jnp.stack([lay['unpad'][n](dwz[f"{n}.{j}"]) for j in layers]), SHARD_AXIS[n])

    rep_flat = jnp.concatenate([gfull[n].reshape(-1) for n in REPLICATED])
    rep_n = rep_flat.shape[0]
    piece = _rup(rep_n, N_DEV * LANES) // N_DEV
    rep_pad = lambda t: jnp.pad(t, (0, N_DEV * piece - rep_n))
    small_parts = jnp.concatenate([_full_to_blocks(gfull[n], SHARD_AXIS[n]).reshape(N_DEV, -1) for n in SMALL_SHARDED]
                                  + [rep_pad(rep_flat).reshape(N_DEV, piece)], axis=1)
    n_flat = small_parts.shape[1]
    n_pad = _rup(n_flat, 16 * FLAT_COLS)
    small_parts = jnp.pad(small_parts, ((0, 0), (0, n_pad - n_flat)))

    def my_small(d):
        rep = rep_pad(jnp.concatenate([d[n].reshape(-1) for n in REPLICATED]))
        mine = lax.dynamic_slice(rep, (me * piece,), (piece,))
        flat = jnp.concatenate([d[n].reshape(-1) for n in SMALL_SHARDED] + [mine])
        return jnp.pad(flat, (0, n_pad - n_flat)).reshape(n_pad // FLAT_COLS, FLAT_COLS)

    by_chip = lambda t, cols: t.reshape(4, 2, -1, cols)
    parts = [by_chip(gblocks[n], blocks[n][-1]) for n in MATRICES]
    parts.append(by_chip(small_parts, FLAT_COLS))
    wire = [BF16] * len(MATRICES) + [F32]
    tags = MATRICES + ["small"]
    ids = _place_ids()
    recv_a = _sibling_exchange(parts, "reduce_sibling")
    recv_c = _chip_exchange([_chip_partials(pt, ra, ids, dt, f"chip_partials_{t}")
                             for pt, ra, dt, t in zip(parts, recv_a, wire, tags)], "reduce_chips")
    mine = [(as_rows(w[n]), as_rows(m[n]), as_rows(v[n])) for n in MATRICES] + [(my_small(w), my_small(m), my_small(v))]
    results = [_adamw(pt, ra, rc, ids, *wmv, f"adamw_{t}")
               for pt, ra, rc, wmv, t in zip(parts, recv_a, recv_c, mine, tags)]
    small_res = [r.reshape(-1) for r in results[-1]]
    rep_res = jnp.stack([r[n_flat - piece:n_flat] for r in small_res])
    rep_rows = _rup(4 * piece, 8 * FLAT_COLS) // FLAT_COLS
    rep_blk = jnp.pad(rep_res.reshape(-1), (0, rep_rows * FLAT_COLS - 4 * piece)).reshape(rep_rows, FLAT_COLS)
    (rep_all,) = _all_gather([rep_blk], "gather_replicated")
    rep_all = rep_all.reshape(N_DEV, -1)[:, :4 * piece]
    rep_all = jnp.transpose(rep_all.reshape(N_DEV, 4, piece), (1, 0, 2)).reshape(4, N_DEV * piece)

    outs = [{}, {}, {}, {}]
    for k in range(4):
        for n, res in zip(MATRICES, results):
            outs[k][n] = res[k].reshape(blocks[n])
        pos = 0
        for n in SMALL_SHARDED:
            size = math.prod(blocks[n])
            outs[k][n] = small_res[k][pos:pos + size].reshape(blocks[n])
            pos += size
        pos = 0
        for n in REPLICATED:
            size = math.prod(blocks[n])
            outs[k][n] = rep_all[k, pos:pos + size].reshape(blocks[n])
            pos += size
    return (loss, dx[None], *[outs[0][n] for n in WEIGHTS], *[outs[1][n] for n in WEIGHTS],
            *[outs[2][n] for n in WEIGHTS], *[outs[3][n] for n in WEIGHTS])


def _matrix_layouts(w16, sp):
    padded, unpad = {}, {}
    ident = lambda g: g
    for n in ('ffn_down', 'ple_proj', 'ple_gate_w', 'even_w_out', 'odd_w_out'):
        for j in range(w16[n].shape[0]):
            padded[f"{n}.{j}"] = w16[n].shape[1:]
        unpad[n] = ident
    for j in range(w16['ffn_up'].shape[0]):
        half = (w16['ffn_up'].shape[1], w16['ffn_up'].shape[2] // 2)
        padded[f"ffn_up.{j}"] = padded[f"ffn_up_val.{j}"] = half
    for n in ('rwkv_w2', 'rwkv_a2', 'rwkv_g2'):
        rows, cols = w16[n].shape[1:]
        for j in range(w16[n].shape[0]):
            padded[f"{n}.{j}"] = (_rup(rows, LANES), cols)
        unpad[n] = functools.partial(lambda g, rows: g[:rows], rows=rows)
    ah, an = sp['rwkv_r_k'].shape[1:]
    bh, bn = sp['gdn_A_log'].shape[1], sp['gdn_norm_g'].shape[1]
    ew = [ah * an] * 3 + [w16['rwkv_w2'].shape[1], w16['rwkv_a2'].shape[1], w16['rwkv_g2'].shape[1]] + [bh * bn] * 4 + [bh, bh]
    cwid, ch = sp['mamba_norm_g'].shape[1], sp['mamba_dt_bias'].shape[1]
    dw = sp['lru_lambda'].shape[1]
    ow = [cwid, sp['mamba_conv_w'].shape[2], ch, dw, dw]
    for n, widths in (('even_w_in', ew), ('odd_w_in', ow)):
        offs, total = _aligned_layout(widths)
        for j in range(w16[n].shape[0]):
            padded[f"{n}.{j}"] = (w16[n].shape[1], total)
        unpad[n] = functools.partial(_unpad_cols, offs=offs, widths=widths)
    return {'padded': padded, 'unpad': unpad}
```

```python
import functools
import math

import jax
import jax.numpy as jnp
from jax import lax
from jax.experimental import pallas as pl
from jax.experimental.pallas import tpu as pltpu

F32 = jnp.float32
BF16 = jnp.bfloat16
HI = lax.Precision.HIGHEST
SDS = jax.ShapeDtypeStruct
MESH = pl.DeviceIdType.MESH

LANES = 128
VMEM_LIMIT = 56 * 1024 * 1024
N_DEV = 8
FLAT_COLS = 1024
MM_VMEM = 40 * 1024 * 1024

LN_EPS = 1e-5
RMS_EPS = 1e-6
L2_EPS = 1e-6
A_GN_EPS = 64e-5
LRU_C = 8.0
C_GROUPS = 4
RWKV_CHUNK = 64
GDN_CHUNK = 64
SSD_CHUNK = 128
SCAN_HEADS = 16

ADAM_LR, ADAM_B1, ADAM_B2, ADAM_EPS, ADAM_WD, ADAM_STEP = 0.001, 0.9, 0.999, 1e-08, 0.01, 10

WEIGHTS = ['ln1_g', 'ln1_b', 'ln2_g', 'ln2_b', 'ffn_up', 'ffn_conv_w', 'ffn_conv_b', 'ffn_down', 'ple_proj',
           'ple_norm_g', 'ple_gate_w', 'ple_gate_b', 'even_w_in', 'even_w_out', 'rwkv_mu', 'rwkv_w0', 'rwkv_w2',
           'rwkv_a0', 'rwkv_a2', 'rwkv_g2', 'rwkv_k_k', 'rwkv_k_a', 'rwkv_r_k', 'rwkv_gn_g', 'rwkv_gn_b',
           'gdn_conv_w', 'gdn_A_log', 'gdn_dt_bias', 'gdn_norm_g', 'odd_w_in', 'odd_w_out', 'mamba_conv_w',
           'mamba_conv_b', 'mamba_dt_bias', 'mamba_A_log', 'mamba_D', 'mamba_norm_g', 'lru_conv_w', 'lru_conv_b',
           'lru_wa', 'lru_ba', 'lru_wx', 'lru_bx', 'lru_lambda']
SHARD_AXIS = {'ffn_up': 2, 'ffn_conv_w': 2, 'ffn_down': 1, 'ple_proj': 2, 'ple_gate_w': 1, 'even_w_in': 2,
              'even_w_out': 1, 'rwkv_w2': 2, 'rwkv_a2': 2, 'rwkv_g2': 2, 'gdn_conv_w': 2, 'odd_w_in': 2,
              'odd_w_out': 1, 'mamba_conv_w': 2, 'mamba_conv_b': 1, 'mamba_norm_g': 1, 'lru_conv_w': 2,
              'lru_conv_b': 1, 'lru_ba': 1, 'lru_bx': 1, 'lru_lambda': 1}
MATRICES = ['ffn_up', 'ffn_down', 'ple_proj', 'ple_gate_w', 'even_w_in', 'even_w_out', 'rwkv_w2', 'rwkv_a2',
            'rwkv_g2', 'odd_w_in', 'odd_w_out']
SMALL_SHARDED = [n for n in WEIGHTS if n in SHARD_AXIS and n not in MATRICES]
REPLICATED = [n for n in WEIGHTS if n not in SHARD_AXIS]


def _cparams(sem):
    return pltpu.CompilerParams(dimension_semantics=sem, vmem_limit_bytes=VMEM_LIMIT)


def _rup(n, m):
    return -(-n // m) * m


def _pick(n, cands):
    for c in cands:
        if n % c == 0:
            return c
    return n


_DIMS = {'nn': (((1,), (0,)), ((), ())), 'nt': (((1,), (1,)), ((), ())), 'tn': (((0,), (0,)), ((), ()))}


def _mm(a, b, mode, name):
    if mode == 'tn':
        K, M = a.shape
    else:
        M, K = a.shape
    N = b.shape[0] if mode == 'nt' else b.shape[1]
    tm = _pick(M, (1024, 512, 256, 128))
    tn = _pick(N, (1024, 512, 256, 128))
    room = MM_VMEM - 3 * tm * tn * 4
    per_k = 2 * (tm * a.dtype.itemsize + tn * b.dtype.itemsize)
    tk = max([t for t in range(LANES, K + 1, LANES) if K % t == 0 and t * per_k <= room] or [K])
    nk = K // tk

    def body(a_ref, b_ref, o_ref, acc_ref):
        k = pl.program_id(2)
        part = lax.dot_general(a_ref[...].astype(BF16), b_ref[...].astype(BF16), _DIMS[mode],
                               preferred_element_type=F32)

        @pl.when(k == 0)
        def _():
            acc_ref[...] = part

        @pl.when(k > 0)
        def _():
            acc_ref[...] += part

        @pl.when(k == nk - 1)
        def _():
            o_ref[...] = acc_ref[...]

    a_spec = pl.BlockSpec((tk, tm), lambda i, j, k: (k, i)) if mode == 'tn' else pl.BlockSpec((tm, tk), lambda i, j, k: (i, k))
    b_spec = pl.BlockSpec((tn, tk), lambda i, j, k: (j, k)) if mode == 'nt' else pl.BlockSpec((tk, tn), lambda i, j, k: (k, j))
    return pl.pallas_call(
        body, grid=(M // tm, N // tn, nk), in_specs=[a_spec, b_spec],
        out_specs=pl.BlockSpec((tm, tn), lambda i, j, k: (i, j)), out_shape=SDS((M, N), F32),
        scratch_shapes=[pltpu.VMEM((tm, tn), F32)], name=name,
        compiler_params=_cparams(("parallel", "parallel", "arbitrary")))(a, b)


def _matmul(x, w16, wz, name):
    @jax.custom_vjp
    def op(x, wz):
        return _mm(x.astype(BF16), w16, 'nn', name + "_f")

    def op_f(x, wz):
        x16 = x.astype(BF16)
        return _mm(x16, w16, 'nn', name + "_f"), x16

    def op_b(x16, g):
        g16 = g.astype(BF16)
        return _mm(g16, w16, 'nt', name + "_dx"), _mm(x16, g16, 'tn', name + "_dw")

    op.defvjp(op_f, op_b)
    return op(x, wz)


def _block_diag_pair(x, wa, wx, name):
    T = x.shape[0]
    nb, bd, _ = wa.shape
    dot = lambda a, b, mode: lax.dot_general(a.astype(BF16), b.astype(BF16), _DIMS[mode], preferred_element_type=F32)
    cols = pl.BlockSpec((T, bd), lambda n: (0, n))
    blk = pl.BlockSpec((1, bd, bd), lambda n: (n, 0, 0))

    def fwd_call(x, wa, wx):
        def body(x_ref, wa_ref, wx_ref, ra_ref, rx_ref):
            ra_ref[...] = dot(x_ref[...], wa_ref[0], 'nn')
            rx_ref[...] = dot(x_ref[...], wx_ref[0], 'nn')

        return pl.pallas_call(body, grid=(nb,), in_specs=[cols, blk, blk], out_specs=[cols, cols],
                              out_shape=[SDS(x.shape, F32)] * 2, name=name + "_f",
                              compiler_params=_cparams(("parallel",)))(x, wa, wx)

    def bwd_call(x, wa, wx, ga, gx):
        def body(x_ref, wa_ref, wx_ref, ga_ref, gx_ref, dx_ref, dwa_ref, dwx_ref):
            dx_ref[...] = dot(ga_ref[...], wa_ref[0], 'nt') + dot(gx_ref[...], wx_ref[0], 'nt')
            dwa_ref[0] = dot(x_ref[...], ga_ref[...], 'tn')
            dwx_ref[0] = dot(x_ref[...], gx_ref[...], 'tn')

        return pl.pallas_call(body, grid=(nb,), in_specs=[cols, blk, blk, cols, cols], out_specs=[cols, blk, blk],
                              out_shape=[SDS(x.shape, F32), SDS(wa.shape, F32), SDS(wx.shape, F32)], name=name + "_b",
                              compiler_params=_cparams(("parallel",)))(x, wa, wx, ga, gx)

    @jax.custom_vjp
    def op(x, wa, wx):
        return tuple(fwd_call(x, wa, wx))

    def op_f(x, wa, wx):
        return op(x, wa, wx), (x, wa, wx)

    def op_b(res, g):
        return tuple(bwd_call(*res, *g))

    op.defvjp(op_f, op_b)
    return op(x, wa, wx)


def _tile_op(name, fn, arrs, params, consts, by_rows, width=LANES):
    arrs, params, consts = tuple(arrs), tuple(params), tuple(consts)
    na, npar, nc = len(arrs), len(params), len(consts)
    T = arrs[0].shape[0]
    if by_rows:
        tile = _pick(T, (256, 128, 64, 32, 16, 8))
        grid = (T // tile,)
        arr_block = lambda a: (tile, a.shape[1])
        arr_spec = lambda a: pl.BlockSpec((tile, a.shape[1]), lambda i: (i, 0))
        par_block = lambda p: p.shape
        par_spec = lambda p: pl.BlockSpec(p.shape, lambda i: (0, 0))
    else:
        grid = (arrs[0].shape[1] // width,)
        arr_block = lambda a: (T, width)
        arr_spec = lambda a: pl.BlockSpec((T, width), lambda i: (0, i))
        par_block = lambda p: (p.shape[0], width)
        par_spec = lambda p: pl.BlockSpec((p.shape[0], width), lambda i: (0, i))
    const_spec = lambda c: pl.BlockSpec(c.shape, lambda i: (0,) * c.ndim)
    outs_sds = jax.eval_shape(fn, *[SDS(arr_block(a), F32) for a in arrs], *[SDS(par_block(p), F32) for p in params],
                              *[SDS(c.shape, c.dtype) for c in consts])
    out_widths = [o.shape[1] for o in outs_sds]
    nout = len(out_widths)
    if by_rows:
        out_shapes = [SDS((T, w), F32) for w in out_widths]
        out_specs = [pl.BlockSpec((tile, w), lambda i: (i, 0)) for w in out_widths]
    else:
        out_shapes = [SDS((T, grid[0] * w), F32) for w in out_widths]
        out_specs = [pl.BlockSpec((T, w), lambda i: (0, i)) for w in out_widths]

    def fwd_call(arrs, params):
        def body(*refs):
            outs = fn(*[r[...] for r in refs[:na + npar + nc]])
            for o_ref, o in zip(refs[na + npar + nc:], outs):
                o_ref[...] = o

        return pl.pallas_call(
            body, grid=grid, in_specs=[arr_spec(a) for a in arrs] + [par_spec(p) for p in params] + [const_spec(c) for c in consts],
            out_specs=out_specs, out_shape=out_shapes, name=name + "_f",
            compiler_params=_cparams(("parallel",)))(*arrs, *params, *consts)

    def bwd_call(arrs, params, cts):
        def body(*refs):
            ins = refs[:na + npar + nc + nout]
            outs = refs[na + npar + nc + nout:]
            av = [r[...] for r in ins[:na]]
            pv = [r[...] for r in ins[na:na + npar]]
            cv = [r[...] for r in ins[na + npar:na + npar + nc]]
            gv = [r[...] for r in ins[na + npar + nc:]]
            _, vjp = jax.vjp(lambda *t: fn(*t, *cv), *av, *pv)
            grads = vjp(tuple(gv))
            for o_ref, g in zip(outs[:na], grads[:na]):
                o_ref[...] = g
            if by_rows and npar:
                @pl.when(pl.program_id(0) == 0)
                def _():
                    for o_ref in outs[na:]:
                        o_ref[...] = jnp.zeros_like(o_ref)

                for o_ref, g in zip(outs[na:], grads[na:]):
                    o_ref[...] += g
            else:
                for o_ref, g in zip(outs[na:], grads[na:]):
                    o_ref[...] = g

        return pl.pallas_call(
            body, grid=grid,
            in_specs=[arr_spec(a) for a in arrs] + [par_spec(p) for p in params] + [const_spec(c) for c in consts] + out_specs,
            out_specs=[arr_spec(a) for a in arrs] + [par_spec(p) for p in params],
            out_shape=[SDS(a.shape, F32) for a in arrs] + [SDS(p.shape, F32) for p in params], name=name + "_b",
            compiler_params=_cparams(("arbitrary",) if by_rows else ("parallel",)))(*arrs, *params, *consts, *cts)

    @jax.custom_vjp
    def op(arrs, params):
        return tuple(fwd_call(arrs, params))

    def op_f(arrs, params):
        return op(arrs, params), (arrs, params)

    def op_b(res, cts):
        arrs, params = res
        g = bwd_call(arrs, params, cts)
        return tuple(g[:na]), tuple(g[na:])

    op.defvjp(op_f, op_b)
    return op(arrs, params)


def _rowwise(name, fn, arrs, params=(), consts=()):
    return _tile_op(name, fn, arrs, params, consts, True)


def _colwise(name, fn, arrs, params=(), width=LANES):
    return _tile_op(name, fn, arrs, params, (), False, width)


@functools.partial(jax.custom_vjp, nondiff_argnums=(1,))
def _shift(x, k):
    rows = lax.broadcasted_iota(jnp.int32, x.shape, 0)
    return jnp.where(rows >= k, pltpu.roll(x, k, 0), 0.0)


def _shift_f(x, k):
    return _shift(x, k), None


def _shift_b(k, _, g):
    n = g.shape[0]
    rows = lax.broadcasted_iota(jnp.int32, g.shape, 0)
    return (jnp.where(rows < n - k, pltpu.roll(g, n - k, 0), 0.0),)


_shift.defvjp(_shift_f, _shift_b)


def _causal_conv(x, w):
    K = w.shape[0]
    y = x * w[K - 1:K, :]
    for j in range(K - 1):
        y = y + _shift(x, K - 1 - j) * w[j:j + 1, :]
    return y


def _silu(x):
    return x * jax.nn.sigmoid(x)


def _softplus(x):
    return jnp.maximum(x, 0.0) + jnp.log1p(jnp.exp(-jnp.abs(x)))


def _split_cols(h, offs, widths):
    @jax.custom_vjp
    def op(h):
        return tuple(h[:, o:o + w] for o, w in zip(offs, widths))

    def op_f(h):
        return op(h), None

    def op_b(_, cts):
        parts, pos = [], 0
        T = cts[0].shape[0]
        for o, w, c in zip(offs, widths, cts):
            if o > pos:
                parts.append(jnp.zeros((T, o - pos), F32))
            parts.append(c)
            pos = o + w
        if pos < h.shape[1]:
            parts.append(jnp.zeros((T, h.shape[1] - pos), F32))
        return (jnp.concatenate(parts, axis=1),)

    op.defvjp(op_f, op_b)
    return op(h)


def _group_ones(width, group):
    g = jnp.arange(width) // group
    return (g[:, None] == g[None, :]).astype(F32)


def _layer_norm_rows(x, g, b, eps):
    mu = jnp.mean(x, axis=1, keepdims=True)
    var = jnp.mean(jnp.square(x - mu), axis=1, keepdims=True)
    return (x - mu) * lax.rsqrt(var + eps) * g + b


def _tri(L, strict=False):
    i = lax.broadcasted_iota(jnp.int32, (L, L), 0)
    j = lax.broadcasted_iota(jnp.int32, (L, L), 1)
    return (i > j) if strict else (i >= j)


def _cumsum_rows(x):
    H, L, _ = x.shape
    tri = jnp.broadcast_to(_tri(L).astype(F32)[None], (H, L, L))
    return jnp.einsum('hls,hsn->hln', tri, x, precision=HI)


def _col_to_row(c):
    L = c.shape[1]
    return jnp.sum(c * _tri_eye(L)[None], axis=1, keepdims=True)


def _row_to_col(r):
    N = r.shape[2]
    return jnp.sum(r * _tri_eye(N)[None], axis=2, keepdims=True)


def _scalar_col(t):
    return _row_to_col(t.reshape(t.shape[0], 1, t.shape[3]))


def _tri_eye(L):
    i = lax.broadcasted_iota(jnp.int32, (L, L), 0)
    j = lax.broadcasted_iota(jnp.int32, (L, L), 1)
    return (i == j).astype(F32)


def _unit_lower_inverse(n_strict):
    L = n_strict.shape[1]
    inv = _tri_eye(L)[None] + n_strict
    x = n_strict
    p = 2
    while p < L:
        x = jnp.einsum('hij,hjk->hik', x, x)
        inv = inv + jnp.einsum('hij,hjk->hik', inv, x)
        p *= 2
    return inv


def _rwkv_chunk(r, lw, k, v, a, b, h0):
    L = r.shape[1]
    mm = jnp.einsum
    cum = _cumsum_rows(lw)
    cum_l = jnp.sum(lw, axis=1, keepdims=True)
    e_neg = jnp.exp(-cum)
    rt, bt, kt, at = r * jnp.exp(cum), b * e_neg, k * e_neg, a * jnp.exp(cum - lw)
    to_end = jnp.exp(cum_l - cum)
    strict, incl = _tri(L, True)[None], _tri(L)[None]
    n = jnp.where(strict, mm('hld,hsd->hls', at, bt), 0.0)
    mk = jnp.where(strict, mm('hld,hsd->hls', at, kt), 0.0)
    u = mm('hls,hsv->hlv', _unit_lower_inverse(n), mm('hld,hdv->hlv', at, h0) + mm('hls,hsv->hlv', mk, v))
    y = (mm('hld,hdv->hlv', rt, h0) + mm('hls,hsv->hlv', jnp.where(incl, mm('hld,hsd->hls', rt, bt), 0.0), u)
         + mm('hls,hsv->hlv', jnp.where(incl, mm('hld,hsd->hls', rt, kt), 0.0), v))
    h1 = (_row_to_col(jnp.exp(cum_l)) * h0 + mm('hld,hlv->hdv', b * to_end, u) + mm('hld,hlv->hdv', k * to_end, v))
    return y, h1


def _gdn_chunk(q, k, v, beta, lg, h0):
    C, D = q.shape[1], q.shape[2]
    scale = D ** -0.5
    beta, lg = _scalar_col(beta), _scalar_col(lg)
    gc = _cumsum_rows(lg)
    gc_l = jnp.sum(lg, axis=1, keepdims=True)
    causal, strict = _tri(C)[None], _tri(C, True)[None]
    decay = jnp.exp(jnp.where(causal, gc - _col_to_row(gc), -jnp.inf))
    k_beta = k * beta
    m = jnp.where(strict, jnp.einsum('hcd,hsd->hcs', k_beta, k) * decay, 0.0)
    inv = _unit_lower_inverse(-m)
    e_gc = jnp.exp(gc)
    u = jnp.einsum('hcs,hsd->hcd', inv, v * beta)
    w = jnp.einsum('hcs,hsd->hcd', inv, k_beta * e_gc)
    attn = jnp.where(causal, jnp.einsum('hcd,hsd->hcs', q * scale, k) * decay, 0.0)
    v_new = u - jnp.einsum('hcd,hde->hce', w, h0)
    o = jnp.einsum('hcd,hde->hce', q * scale * e_gc, h0) + jnp.einsum('hcs,hse->hce', attn, v_new)
    h1 = h0 * jnp.exp(gc_l) + jnp.einsum('hcd,hce->hde', k * jnp.exp(gc_l - gc), v_new)
    return o, h1


def _ssd_chunk(xs, dt, aa, bm, cm, h0):
    H, L, _ = xs.shape
    dt, aa = _scalar_col(dt), _scalar_col(aa)
    x = xs * dt
    cs = _cumsum_rows(aa)
    cs_l = jnp.sum(aa, axis=1, keepdims=True)
    causal = _tri(L)[None]
    cb = jnp.einsum('gln,gsn->gls', cm, bm)
    wd = jnp.where(causal, cb * jnp.exp(jnp.where(causal, cs - _col_to_row(cs), -jnp.inf)), 0.0)
    cmb = jnp.broadcast_to(cm, (H,) + cm.shape[1:])
    bmb = jnp.broadcast_to(bm, (H,) + bm.shape[1:])
    y = jnp.einsum('hls,hsp->hlp', wd, x) + jnp.einsum('hln,hnp->hlp', cmb, h0) * jnp.exp(cs)
    h1 = jnp.exp(cs_l) * h0 + jnp.einsum('hln,hlp->hnp', bmb, x * jnp.exp(cs_l - cs))
    return y, h1


def _chunk_scan(name, fn, seqs, hb, L, state_shape, out_width):
    seqs = tuple(seqs)
    ns = len(seqs)
    H = max(s.shape[0] for s in seqs)
    T = max(s.shape[1] for s in seqs)
    nc, nh = T // L, H // hb
    lead = [hb if s.shape[0] == H else 1 for s in seqs]
    st_block = (hb,) + state_shape

    def seq_spec(s, l, imap):
        if s.ndim == 4:
            return pl.BlockSpec((l, 1, 1, L), lambda h, c: imap(h, c) + (0,))
        return pl.BlockSpec((l, L, s.shape[2]), imap)

    fmap = lambda h, c: (h, c, 0)
    rmap = lambda h, c: (h, nc - 1 - c, 0)

    def fwd_call(seqs):
        def body(*refs):
            y_ref, st_ref, carry = refs[ns], refs[ns + 1], refs[ns + 2]

            @pl.when(pl.program_id(1) == 0)
            def _():
                carry[...] = jnp.zeros_like(carry)

            h0 = carry[...]
            st_ref[0] = h0
            y, h1 = fn(*[r[...] for r in refs[:ns]], h0)
            y_ref[...] = y
            carry[...] = h1

        return pl.pallas_call(
            body, grid=(nh, nc), in_specs=[seq_spec(s, l, fmap) for s, l in zip(seqs, lead)],
            out_specs=[pl.BlockSpec((hb, L, out_width), fmap),
                       pl.BlockSpec((1,) + st_block, lambda h, c: (c, h) + (0,) * len(state_shape))],
            out_shape=[SDS((H, T, out_width), F32), SDS((nc, H) + state_shape, F32)],
            scratch_shapes=[pltpu.VMEM(st_block, F32)], name=name + "_f",
            compiler_params=_cparams(("parallel", "arbitrary")))(*seqs)

    def bwd_call(seqs, states, dy):
        def body(*refs):
            st_ref, dy_ref = refs[ns], refs[ns + 1]
            outs, carry = refs[ns + 2:2 * ns + 2], refs[2 * ns + 2]

            @pl.when(pl.program_id(1) == 0)
            def _():
                carry[...] = jnp.zeros_like(carry)

            _, vjp = jax.vjp(fn, *[r[...] for r in refs[:ns]], st_ref[0])
            grads = vjp((dy_ref[...], carry[...]))
            for o_ref, g in zip(outs, grads[:ns]):
                o_ref[...] = g
            carry[...] = grads[ns]

        return pl.pallas_call(
            body, grid=(nh, nc),
            in_specs=[seq_spec(s, l, rmap) for s, l in zip(seqs, lead)]
            + [pl.BlockSpec((1,) + st_block, lambda h, c: (nc - 1 - c, h) + (0,) * len(state_shape)),
               pl.BlockSpec((hb, L, out_width), rmap)],
            out_specs=[seq_spec(s, l, rmap) for s, l in zip(seqs, lead)],
            out_shape=[SDS(s.shape, F32) for s in seqs],
            scratch_shapes=[pltpu.VMEM(st_block, F32)], name=name + "_b",
            compiler_params=_cparams(("parallel", "arbitrary")))(*seqs, states, dy)

    @jax.custom_vjp
    def op(seqs):
        return fwd_call(seqs)[0]

    def op_f(seqs):
        y, states = fwd_call(seqs)
        return y, (seqs, states)

    def op_b(res, dy):
        seqs, states = res
        return (tuple(bwd_call(seqs, states, dy)),)

    op.defvjp(op_f, op_b)
    return op(seqs)


def _lru_scan_call(a, u, h, reverse, name):
    T, C = a.shape
    cw = _pick(C, (1024, 512, 256, 128))
    tt = _pick(T, (512, 256, 128, 64, 32, 16, 8))
    nt, ng = T // tt, tt // 8
    sub = lambda: lax.broadcasted_iota(jnp.int32, (8, cw), 0)
    first = lambda: pl.program_id(1) == 0

    def fwd_body(a_ref, u_ref, h_ref, carry_ref):
        @pl.when(first())
        def _():
            carry_ref[...] = jnp.zeros_like(carry_ref)

        def group(i, carry):
            r0 = pl.multiple_of(i * 8, 8)
            ab, ub = a_ref[pl.ds(r0, 8), :], u_ref[pl.ds(r0, 8), :]
            out = jnp.zeros((8, cw), F32)
            for j in range(8):
                carry = ab[j:j + 1, :] * carry + ub[j:j + 1, :]
                out = jnp.where(sub() == j, carry, out)
            h_ref[pl.ds(r0, 8), :] = out
            return carry

        carry_ref[...] = lax.fori_loop(0, ng, group, carry_ref[...])

    def bwd_body(a_ref, u_ref, h_ref, hp_ref, g_ref, da_ref, cg_ref, ca_ref):
        @pl.when(first())
        def _():
            cg_ref[...] = jnp.zeros_like(cg_ref)
            ca_ref[...] = jnp.zeros_like(ca_ref)

        h_before = jnp.where(pl.program_id(1) < nt - 1, hp_ref[7:8, :], 0.0)

        def group(i, carry):
            g_next, a_next = carry
            gi = ng - 1 - i
            r0 = pl.multiple_of(gi * 8, 8)
            rp = pl.multiple_of(jnp.maximum(gi - 1, 0) * 8, 8)
            ab, ub, hb = a_ref[pl.ds(r0, 8), :], u_ref[pl.ds(r0, 8), :], h_ref[pl.ds(r0, 8), :]
            h_last_prev = jnp.where(gi > 0, h_ref[pl.ds(rp, 8), :][7:8, :], h_before)
            g_out = jnp.zeros((8, cw), F32)
            da_out = jnp.zeros((8, cw), F32)
            for j in range(7, -1, -1):
                g_next = ub[j:j + 1, :] + a_next * g_next
                a_next = ab[j:j + 1, :]
                h_prev = hb[j - 1:j, :] if j > 0 else h_last_prev
                g_out = jnp.where(sub() == j, g_next, g_out)
                da_out = jnp.where(sub() == j, g_next * h_prev, da_out)
            g_ref[pl.ds(r0, 8), :] = g_out
            da_ref[pl.ds(r0, 8), :] = da_out
            return g_next, a_next

        cg_ref[...], ca_ref[...] = lax.fori_loop(0, ng, group, (cg_ref[...], ca_ref[...]))

    row = pltpu.VMEM((1, cw), F32)
    if not reverse:
        spec = pl.BlockSpec((tt, cw), lambda i, t: (t, i))
        return pl.pallas_call(fwd_body, grid=(C // cw, nt), in_specs=[spec, spec], out_specs=spec,
                              out_shape=SDS((T, C), F32), scratch_shapes=[row], name=name,
                              compiler_params=_cparams(("parallel", "arbitrary")))(a, u)
    spec = pl.BlockSpec((tt, cw), lambda i, t: (nt - 1 - t, i))
    before = pl.BlockSpec((8, cw), lambda i, t: (jnp.maximum((nt - 1 - t) * ng - 1, 0), i))
    return pl.pallas_call(bwd_body, grid=(C // cw, nt), in_specs=[spec, spec, spec, before], out_specs=[spec, spec],
                          out_shape=[SDS((T, C), F32), SDS((T, C), F32)], scratch_shapes=[row, row], name=name,
                          compiler_params=_cparams(("parallel", "arbitrary")))(a, u, h, h)


@jax.custom_vjp
def _lru_scan(a, u):
    return _lru_scan_call(a, u, None, False, "lru_scan_f")


def _lru_scan_f(a, u):
    h = _lru_scan(a, u)
    return h, (a, h)


def _lru_scan_b(res, dh):
    a, h = res
    g, da = _lru_scan_call(a, dh, h, True, "lru_scan_b")
    return da, g


_lru_scan.defvjp(_lru_scan_f, _lru_scan_b)


def _heads_major(x, nheads):
    T, W = x.shape
    return jnp.transpose(x.reshape(T, nheads, W // nheads), (1, 0, 2))


def _tokens_major(x):
    H, T, N = x.shape
    return jnp.transpose(x, (1, 0, 2)).reshape(T, H * N)


def _pad_cols(w, offs, widths, total):
    parts, pos, src = [], 0, 0
    for o, wd in zip(offs, widths):
        if o > pos:
            parts.append(jnp.zeros((w.shape[0], o - pos), w.dtype))
        parts.append(w[:, src:src + wd])
        src += wd
        pos = o + wd
    if pos < total:
        parts.append(jnp.zeros((w.shape[0], total - pos), w.dtype))
    return jnp.concatenate(parts, axis=1)


def _unpad_cols(w, offs, widths):
    return jnp.concatenate([w[:, o:o + wd] for o, wd in zip(offs, widths)], axis=1)


def _aligned_layout(widths):
    offs, pos = [], 0
    for w in widths:
        offs.append(pos)
        pos += _rup(w, LANES)
    return offs, _rup(pos, 512)


def _pad_lanes(v, n):
    return jnp.pad(v, ((0, 0), (0, n - v.shape[1])))


def _even_mixer(x, q, wz, li):
    T = x.shape[0]
    ah, an = q['rwkv_r_k'].shape
    aw = ah * an
    bh, bn = q['gdn_A_log'].shape[0], q['gdn_norm_g'].shape[0]
    bw = bh * bn
    lw_, la_, lg_ = q['rwkv_w2'].shape[0], q['rwkv_a2'].shape[0], q['rwkv_g2'].shape[0]
    widths = [aw, aw, aw, lw_, la_, lg_, bw, bw, bw, bw, bh, bh]
    offs, total = _aligned_layout(widths)
    pw = [_rup(w, LANES) for w in widths]
    hcols = _matmul(x, _pad_cols(q['even_w_in'], offs, widths, total), wz['even_w_in'], f"even_in{li}")
    a_w = offs[6]
    a_cols, bq, bk, bv, bz, beta_raw, alpha_raw = _split_cols(hcols, [0] + offs[6:], [a_w] + pw[6:])

    mu = _pad_cols(q['rwkv_mu'][None], offs[:6], widths[:6], a_w)
    (xs,) = _colwise(f"rwkv_shift{li}", lambda h, m: (h + (_shift(h, 1) - h) * m,), [a_cols], [mu])
    r, k, v, w_lo, a_lo, g_lo = _split_cols(xs, offs[:6], pw[:6])
    tw, sg = _rowwise(f"rwkv_lora_act{li}", lambda w, g: (jnp.tanh(w), jax.nn.sigmoid(g)), [w_lo, g_lo])
    pad_rows = lambda w, n: jnp.pad(w, ((0, n - w.shape[0]), (0, 0)))
    wl = _matmul(tw, pad_rows(q['rwkv_w2'], pw[3]), wz['rwkv_w2'], f"rwkv_w2{li}")
    al = _matmul(a_lo, pad_rows(q['rwkv_a2'], pw[4]), wz['rwkv_a2'], f"rwkv_a2{li}")
    g = _matmul(sg, pad_rows(q['rwkv_g2'], pw[5]), wz['rwkv_g2'], f"rwkv_g2{li}")
    ones_a = _group_ones(aw, an)

    def pre(k, wl, al, w0, a0, k_k, k_a, ones):
        lw = -jnp.exp(-_softplus(-(w0 + wl)) - 0.5)
        a = jax.nn.sigmoid(a0 + al)
        kk = k * k_k
        kk = kk * lax.rsqrt(jnp.dot(kk * kk, ones) + L2_EPS)
        return lw, k * (1.0 + (a - 1.0) * k_a), -kk, kk * a

    lw, k2, sa, sb = _rowwise(f"rwkv_pre{li}", pre, [k, wl, al],
                              [q['rwkv_w0'][None], q['rwkv_a0'][None], q['rwkv_k_k'][None], q['rwkv_k_a'][None]], [ones_a])
    hm = lambda t: _heads_major(t, ah)
    out = _chunk_scan(f"rwkv_scan{li}", _rwkv_chunk, [hm(r), hm(lw), hm(k2), hm(v), hm(sa), hm(sb)],
                      min(ah, SCAN_HEADS), min(RWKV_CHUNK, T), (an, an), an)
    out = _tokens_major(out)

    def post(out, r, k2, v, g, gn_g, gn_b, r_k, ones):
        mean = jnp.dot(out, ones) * (1.0 / an)
        cen = out - mean
        var = jnp.dot(cen * cen, ones) * (1.0 / an)
        normed = cen * lax.rsqrt(var + A_GN_EPS) * gn_g + gn_b
        bonus = jnp.dot(r * k2 * r_k, ones) * v
        return ((normed + bonus) * g,)

    flat = lambda t: t.reshape(1, -1)
    (ya,) = _rowwise(f"rwkv_post{li}", post, [out, r, k2, v, g],
                     [flat(q['rwkv_gn_g']), flat(q['rwkv_gn_b']), flat(q['rwkv_r_k'])], [ones_a])

    cw = q['gdn_conv_w']

    def conv_l2(x, w):
        y = _silu(_causal_conv(x, w))
        return (y * lax.rsqrt(jnp.sum(y * y, axis=1, keepdims=True) + L2_EPS),)

    if bn == LANES:
        (gq,) = _colwise(f"gdn_conv_q{li}", conv_l2, [bq], [cw[:, :bw]])
        (gk,) = _colwise(f"gdn_conv_k{li}", conv_l2, [bk], [cw[:, bw:2 * bw]])
    else:
        raise NotImplementedError("gated DeltaNet head width must equal the lane count")
    (gv,) = _colwise(f"gdn_conv_v{li}", lambda x, w: (_silu(_causal_conv(x, w)),), [bv], [cw[:, 2 * bw:]])

    def gates(beta_raw, alpha_raw, a_log, dt_bias):
        return jax.nn.sigmoid(beta_raw), -jnp.exp(a_log) * _softplus(alpha_raw + dt_bias)

    beta, lg = _rowwise(f"gdn_gates{li}", gates, [beta_raw, alpha_raw],
                        [_pad_lanes(q['gdn_A_log'][None], pw[10]), _pad_lanes(q['gdn_dt_bias'][None], pw[11])])
    gl = min(GDN_CHUNK, T)
    col = lambda t: jnp.transpose(t[:, :bh]).reshape(bh, T // gl, 1, gl)
    hmb = lambda t: _heads_major(t, bh)
    o = _chunk_scan(f"gdn_scan{li}", _gdn_chunk, [hmb(gq), hmb(gk), hmb(gv), col(beta), col(lg)],
                    min(bh, SCAN_HEADS), gl, (bn, bn), bn)
    o = _tokens_major(o)
    ones_b = _group_ones(bw, bn)

    def gdn_post(o, z, ng, ones):
        ms = jnp.dot(o * o, ones) * (1.0 / bn)
        return (o * lax.rsqrt(ms + RMS_EPS) * ng * _silu(z),)

    (yb,) = _rowwise(f"gdn_post{li}", gdn_post, [o, bz], [jnp.tile(q['gdn_norm_g'][None], (1, bh))], [ones_b])
    return _matmul(jnp.concatenate([ya, yb], axis=1), q['even_w_out'], wz['even_w_out'], f"even_out{li}")


def _odd_mixer(x, q, wz, li):
    T = x.shape[0]
    ch = q['mamba_dt_bias'].shape[0]
    cwid = q['mamba_norm_g'].shape[0]
    cp = cwid // ch
    xbc_w = q['mamba_conv_w'].shape[1]
    cn = (xbc_w - cwid) // (2 * C_GROUPS)
    dw = q['lru_lambda'].shape[0]
    widths = [cwid, xbc_w, ch, dw, dw]
    offs, total = _aligned_layout(widths)
    pw = [_rup(w, LANES) for w in widths]
    hcols = _matmul(x, _pad_cols(q['odd_w_in'], offs, widths, total), wz['odd_w_in'], f"odd_in{li}")
    z, xbc, dt_raw, y_br, x_br = _split_cols(hcols, offs, pw)

    (xbc_c,) = _colwise(f"mamba_conv{li}", lambda x, w, b: (_silu(_causal_conv(x, w) + b),), [xbc],
                        [q['mamba_conv_w'], q['mamba_conv_b'][None]])
    gn = C_GROUPS * cn
    xs, bm, cm = _split_cols(xbc_c, [0, cwid, cwid + gn], [cwid, gn, gn])

    def dts(dt_raw, dt_bias, a_log):
        dt = _softplus(dt_raw + dt_bias)
        return dt, dt * (-jnp.exp(a_log))

    dt, aa = _rowwise(f"mamba_dt{li}", dts, [dt_raw],
                      [_pad_lanes(q['mamba_dt_bias'][None], pw[2]), _pad_lanes(q['mamba_A_log'][None], pw[2])])
    sl = min(SSD_CHUNK, T)
    col = lambda t: jnp.transpose(t[:, :ch]).reshape(ch, T // sl, 1, sl)
    y = _chunk_scan(f"ssd_scan{li}", _ssd_chunk,
                    [_heads_major(xs, ch), col(dt), col(aa), _heads_major(bm, C_GROUPS), _heads_major(cm, C_GROUPS)],
                    ch // C_GROUPS, sl, (cn, cp), cp)
    y = _tokens_major(y)
    gsz = cwid // C_GROUPS

    def mamba_post(y, xs, z, d, ng):
        yy = (y + xs * d) * _silu(z)
        lane = lax.broadcasted_iota(jnp.int32, yy.shape, 1)
        ms = jnp.zeros_like(yy)
        for gi in range(C_GROUPS):
            sel = (lane >= gi * gsz) & (lane < (gi + 1) * gsz)
            ms = jnp.where(sel, jnp.sum(jnp.where(sel, yy * yy, 0.0), axis=1, keepdims=True) * (1.0 / gsz), ms)
        return (yy * lax.rsqrt(ms + RMS_EPS) * ng,)

    (yc,) = _rowwise(f"mamba_post{li}", mamba_post, [y, xs, z],
                     [jnp.repeat(q['mamba_D'], cp)[None], q['mamba_norm_g'][None]])

    (xc,) = _colwise(f"lru_conv{li}", lambda x, w, b: (_causal_conv(x, w) + b,), [x_br],
                     [q['lru_conv_w'], q['lru_conv_b'][None]])
    ra, ia = _block_diag_pair(xc, q['lru_wa'], q['lru_wx'], f"lru_gates{li}")

    def lru_pre(ra, ia, xc, ba, bx, lam):
        r = jax.nn.sigmoid(ra + ba)
        i = jax.nn.sigmoid(ia + bx)
        log_a = LRU_C * r * (-_softplus(-lam))
        t = 2.0 * log_a
        series = t * (1.0 + t * (0.5 + t * (1.0 / 6.0 + t * (1.0 / 24.0 + t * (1.0 / 120.0 + t * (1.0 / 720.0))))))
        expm1 = jnp.where(t > -0.2, series, jnp.exp(t) - 1.0)
        return jnp.exp(log_a), jnp.sqrt(-expm1) * (i * xc)

    a, u = _rowwise(f"lru_pre{li}", lru_pre, [ra, ia, xc], [q['lru_ba'][None], q['lru_bx'][None], q['lru_lambda'][None]])
    h = _lru_scan(a, u)
    (yd,) = _rowwise(f"lru_post{li}", lambda h, y: (h * jax.nn.gelu(y),), [h, y_br])
    return _matmul(jnp.concatenate([yc, yd], axis=1), q['odd_w_out'], wz['odd_w_out'], f"odd_out{li}")


def _forward(x, wz, sp, p, w16, depth):
    alpha = (2.0 * depth) ** 0.25
    for i in range(depth):
        j = i // 2
        even = i % 2 == 0
        names = [n for n in WEIGHTS if n.startswith(('rwkv_', 'gdn_', 'even_') if even else ('mamba_', 'lru_', 'odd_'))]
        q = {n: (w16[n][j] if n in MATRICES else sp[n][j]) for n in names}
        wzl = {n: wz[f"{n}.{j}"] for n in names if n in MATRICES}
        y = (_even_mixer if even else _odd_mixer)(x, q, wzl, i)

        def ln_res(x, y, g, b):
            return (_layer_norm_rows(alpha * x + y, g, b, LN_EPS),)

        (h,) = _rowwise(f"ln1_{i}", ln_res, [x, y], [sp['ln1_g'][i][None], sp['ln1_b'][i][None]])
        dff = w16['ffn_up'].shape[2] // 2
        gate = _matmul(h, w16['ffn_up'][i][:, :dff], wz[f"ffn_up.{i}"], f"ffn_gate{i}")
        val = _matmul(h, w16['ffn_up'][i][:, dff:], wz[f"ffn_up_val.{i}"], f"ffn_val{i}")
        cw, cb = sp['ffn_conv_w'][i], sp['ffn_conv_b'][i][None]

        def ffn_act(gate, val, wg, wv, bg, bv):
            return (_silu(_causal_conv(gate, wg) + bg) * (_causal_conv(val, wv) + bv),)

        (act,) = _colwise(f"ffn_act{i}", ffn_act, [gate, val], [cw[:, :dff], cw[:, dff:], cb[:, :dff], cb[:, dff:]])
        f = _matmul(act, w16['ffn_down'][i], wz[f"ffn_down.{i}"], f"ffn_down{i}")
        (h2,) = _rowwise(f"ln2_{i}", ln_res, [h, f], [sp['ln2_g'][i][None], sp['ln2_b'][i][None]])
        e0 = _matmul(p[i], w16['ple_proj'][i], wz[f"ple_proj.{i}"], f"ple_proj{i}")
        gl = _matmul(h2, w16['ple_gate_w'][i], wz[f"ple_gate_w.{i}"], f"ple_gate{i}")

        def ple(h2, gl, e0, gb, ng):
            e = e0 * lax.rsqrt(jnp.mean(e0 * e0, axis=1, keepdims=True) + RMS_EPS) * ng
            return (h2 + jax.nn.sigmoid(gl + gb) * e,)

        (x,) = _rowwise(f"ple{i}", ple, [h2, gl, e0], [sp['ple_gate_b'][i][None], sp['ple_norm_g'][i][None]])
    return x


def _loss_head(y, target):
    T, D = y.shape
    tile = _pick(T, (256, 128, 64, 32, 16, 8))

    def body(y_ref, t_ref, dy_ref, l_ref):
        err = y_ref[...] - t_ref[...]
        dy_ref[...] = err * (1.0 / D)

        @pl.when(pl.program_id(0) == 0)
        def _():
            l_ref[...] = jnp.zeros_like(l_ref)

        l_ref[...] += jnp.sum(jnp.sum(err * err, axis=1, keepdims=True), axis=0, keepdims=True) * (0.5 / D) + jnp.zeros_like(l_ref)

    spec = pl.BlockSpec((tile, D), lambda i: (i, 0))
    dy, l = pl.pallas_call(body, grid=(T // tile,), in_specs=[spec, spec],
                           out_specs=[spec, pl.BlockSpec((8, LANES), lambda i: (0, 0))],
                           out_shape=[SDS((T, D), F32), SDS((8, LANES), F32)], name="loss_head",
                           compiler_params=_cparams(("arbitrary",)))(y, target)
    return l[0, 0], dy


def _my_index():
    return 4 * lax.axis_index("x") + 2 * lax.axis_index("y") + lax.axis_index("c")


def _hbm_specs(n):
    return [pl.BlockSpec(memory_space=pl.ANY)] * n


def _all_gather(blocks, name):
    blocks = tuple(blocks)
    n = len(blocks)
    half = [b.shape[0] // 2 for b in blocks]

    def body(*refs):
        ins, outs = refs[:n], refs[n:2 * n]
        send_sems, recv_sems, local_sems = refs[2 * n:]
        x, y, c = lax.axis_index("x"), lax.axis_index("y"), lax.axis_index("c")
        me, sibling, other = (x, y, c), (x, y, 1 - c), 1 - c
        xn, yn, dg = (1 - x, y), (x, 1 - y), (1 - x, 1 - y)

        def slot(i, px, py, pc, h=None):
            ref = outs[i].at[4 * px + 2 * py + pc]
            return ref if h is None else ref.at[pl.ds(h * half[i], half[i])]

        def copy(i, k, blk, to, h=None, src=None):
            dst = slot(i, *blk, h)
            return pltpu.make_async_remote_copy(
                src_ref=dst if src is None else src, dst_ref=dst, send_sem=send_sems.at[9 * i + k],
                recv_sem=recv_sems.at[9 * i + k], device_id=to, device_id_type=MESH)

        mine = [pltpu.make_async_copy(ins[i], slot(i, *me), local_sems.at[i]) for i in range(n)]
        sent = []
        for i in range(n):
            sent += [copy(i, 1, me, (*xn, c), src=ins[i]), copy(i, 2, me, (*yn, c), src=ins[i])]
        sent += [copy(i, 0, me, sibling, src=ins[i]) for i in range(n)]
        for cp in mine + sent:
            cp.start()

        def after(i, k_in, blk, h_in, forwards):
            copy(i, k_in, blk, me, h_in).wait_recv()
            for k_out, to, h_out in forwards:
                sent.append(copy(i, k_out, blk, to, h_out))
                sent[-1].start()

        for i in range(n):
            after(i, 1, (*xn, c), None, [(3, (*yn, c), 0), (5, sibling, None)])
        for i in range(n):
            after(i, 2, (*yn, c), None, [(4, (*xn, c), 1), (6, sibling, None)])
        for i in range(n):
            after(i, 3, (*dg, c), 0, [(7, sibling, 0)])
        for i in range(n):
            after(i, 4, (*dg, c), 1, [(8, sibling, 1)])
        for i in range(n):
            copy(i, 0, sibling, me).wait_recv()
            copy(i, 5, (*xn, other), me).wait_recv()
            copy(i, 6, (*yn, other), me).wait_recv()
            copy(i, 7, (*dg, other), me, 0).wait_recv()
            copy(i, 8, (*dg, other), me, 1).wait_recv()
        for cp in sent:
            cp.wait_send()
        for cp in mine:
            cp.wait()

    return pl.pallas_call(
        body, out_shape=[SDS((N_DEV,) + b.shape, b.dtype) for b in blocks], in_specs=_hbm_specs(n), out_specs=_hbm_specs(n),
        scratch_shapes=[pltpu.SemaphoreType.DMA((9 * n,)), pltpu.SemaphoreType.DMA((9 * n,)), pltpu.SemaphoreType.DMA((n,))],
        name=name)(*blocks)


def _sibling_exchange(parts, name):
    parts = tuple(parts)
    n = len(parts)

    def body(*refs):
        ins, outs = refs[:n], refs[n:2 * n]
        send_sems, recv_sems = refs[2 * n:]
        x, y, c = lax.axis_index("x"), lax.axis_index("y"), lax.axis_index("c")
        copies = [pltpu.make_async_remote_copy(
            src_ref=ins[i].at[q, 1 - c], dst_ref=outs[i].at[q], send_sem=send_sems.at[4 * i + q],
            recv_sem=recv_sems.at[4 * i + q], device_id=(x, y, 1 - c), device_id_type=MESH)
            for i in range(n) for q in range(4)]
        for cp in copies:
            cp.start()
        for cp in copies:
            cp.wait_recv()
        for cp in copies:
            cp.wait_send()

    return pl.pallas_call(
        body, out_shape=[SDS((4,) + p.shape[2:], p.dtype) for p in parts], in_specs=_hbm_specs(n), out_specs=_hbm_specs(n),
        scratch_shapes=[pltpu.SemaphoreType.DMA((4 * n,)), pltpu.SemaphoreType.DMA((4 * n,))], name=name)(*parts)


def _neighbour_exchange(bufs, k, name):
    bufs = tuple(bufs)
    n = len(bufs)

    def body(*refs):
        ins, outs = refs[:n], refs[n:2 * n]
        send_sems, recv_sems = refs[2 * n:]
        x, y, c = lax.axis_index("x"), lax.axis_index("y"), lax.axis_index("c")
        copies = [pltpu.make_async_remote_copy(
            src_ref=ins[i].at[pl.ds(d * k, k)], dst_ref=outs[i].at[pl.ds(d * k, k)], send_sem=send_sems.at[2 * i + d],
            recv_sem=recv_sems.at[2 * i + d], device_id=to, device_id_type=MESH)
            for d, to in enumerate([(1 - x, y, c), (x, 1 - y, c)]) for i in range(n)]
        for cp in copies:
            cp.start()
        for cp in copies:
            cp.wait_recv()
        for cp in copies:
            cp.wait_send()

    return pl.pallas_call(
        body, out_shape=[SDS((2 * k,) + b.shape[1:], b.dtype) for b in bufs], in_specs=_hbm_specs(n), out_specs=_hbm_specs(n),
        scratch_shapes=[pltpu.SemaphoreType.DMA((2 * n,)), pltpu.SemaphoreType.DMA((2 * n,))], name=name)(*bufs)


_FIRST_HOP_SLOT = ((0, 5), (4, 2), (1, 3))


def _place_ids():
    x, y, c = lax.axis_index("x"), lax.axis_index("y"), lax.axis_index("c")
    place = [c, 2 * x + y, 2 * (1 - x) + y, 2 * x + (1 - y), 2 * (1 - x) + (1 - y)]
    return jnp.stack(place + [jnp.int32(s) for pair in _FIRST_HOP_SLOT for s in pair]).astype(jnp.int32)


def _row_tile(rows, cols):
    best = None
    for t in range(16, rows + 1, 16):
        if rows % t == 0 and t * cols <= 256 * 1024:
            best = t
    return best or rows


def _chip_partials(part, recv_a, ids, dtype, name):
    _, _, R, C = part.shape
    tr = _row_tile(R // 2, C)
    nt = R // 2 // tr

    def body(ids_ref, p_ref, a_ref, o_ref):
        o_ref[...] = (p_ref[...] + a_ref[...]).astype(dtype)

    return pl.pallas_call(
        body, out_shape=SDS((6, R // 2, C), dtype),
        grid_spec=pltpu.PrefetchScalarGridSpec(
            num_scalar_prefetch=1, grid=(3, 2, nt),
            in_specs=[pl.BlockSpec((None, None, tr, C), lambda s, h, i, ids: (ids[2 + s], ids[0], h * nt + i, 0)),
                      pl.BlockSpec((None, tr, C), lambda s, h, i, ids: (ids[2 + s], h * nt + i, 0))],
            out_specs=pl.BlockSpec((None, tr, C), lambda s, h, i, ids: (ids[5 + 2 * s + h], i, 0))),
        name=name, compiler_params=_cparams(("parallel", "parallel", "parallel")))(ids, part, recv_a)


def _second_hop(first, recv1, name):
    _, R2, C = first.shape
    tr = _row_tile(R2, C)

    def body(f_ref, r_ref, o_ref):
        o_ref[...] = (f_ref[...].astype(F32) + r_ref[...].astype(F32)).astype(o_ref.dtype)

    return pl.pallas_call(
        body, out_shape=SDS((2, R2, C), first.dtype), grid=(2, R2 // tr),
        in_specs=[pl.BlockSpec((None, tr, C), lambda d, i: (5 - d, i, 0)),
                  pl.BlockSpec((None, tr, C), lambda d, i: (3 - 2 * d, i, 0))],
        out_specs=pl.BlockSpec((None, tr, C), lambda d, i: (d, i, 0)),
        name=name, compiler_params=_cparams(("parallel", "parallel")))(first, recv1)


def _adamw(part, recv_a, recv1, recv2, ids, w, m, v, name):
    R, C = w.shape
    tr = _row_tile(R // 2, C)
    nt = R // 2 // tr
    c1 = 1.0 / (1.0 - ADAM_B1 ** ADAM_STEP)
    c2 = 1.0 / (1.0 - ADAM_B2 ** ADAM_STEP)

    def body(ids_ref, p_ref, a_ref, x1_ref, y1_ref, x2_ref, y2_ref, w_ref, m_ref, v_ref, g_ref, d_ref, nm_ref, nv_ref):
        first = pl.program_id(0) == 0
        from_x = jnp.where(first, x1_ref[...], x2_ref[...]).astype(F32)
        from_y = jnp.where(first, y2_ref[...], y1_ref[...]).astype(F32)
        g = p_ref[...] + a_ref[...] + from_x + from_y
        nm = ADAM_B1 * m_ref[...] + (1.0 - ADAM_B1) * g
        nv = ADAM_B2 * v_ref[...] + (1.0 - ADAM_B2) * jnp.square(g)
        g_ref[...] = g
        nm_ref[...] = nm
        nv_ref[...] = nv
        d_ref[...] = -ADAM_LR * ((nm * c1) / (jnp.sqrt(nv * c2) + ADAM_EPS) + ADAM_WD * w_ref[...])

    spec = pl.BlockSpec((tr, C), lambda h, i, ids: (h * nt + i, 0))
    half = lambda slot: pl.BlockSpec((None, tr, C), lambda h, i, ids: (slot, i, 0))
    return pl.pallas_call(
        body, out_shape=[SDS((R, C), F32)] * 4,
        grid_spec=pltpu.PrefetchScalarGridSpec(
            num_scalar_prefetch=1, grid=(2, nt),
            in_specs=[pl.BlockSpec((None, None, tr, C), lambda h, i, ids: (ids[1], ids[0], h * nt + i, 0)),
                      pl.BlockSpec((None, tr, C), lambda h, i, ids: (ids[1], h * nt + i, 0)),
                      half(0), half(2), half(0), half(1), spec, spec, spec],
            out_specs=[spec] * 4),
        name=name, compiler_params=_cparams(("parallel", "parallel")))(ids, part, recv_a, recv1, recv1, recv2, recv2, w, m, v)


def _to_flat(vecs, quantum):
    flat = jnp.concatenate([v.reshape(-1) for v in vecs])
    n = _rup(flat.shape[0], quantum * FLAT_COLS)
    return jnp.pad(flat, (0, n - flat.shape[0])).reshape(n // FLAT_COLS, FLAT_COLS)


def _gathered_to_full(g, names, blocks):
    flat = g.reshape(N_DEV, -1)
    out, pos = {}, 0
    for n in names:
        shp = blocks[n]
        size = math.prod(shp)
        out[n] = _blocks_to_full(flat[:, pos:pos + size].reshape((N_DEV,) + shp), SHARD_AXIS[n])
        pos += size
    return out


def _blocks_to_full(g, ax):
    shp = g.shape[1:]
    return jnp.moveaxis(g, 0, ax).reshape(shp[:ax] + (N_DEV * shp[ax],) + shp[ax + 1:])


def _full_to_blocks(g, ax, ndev=N_DEV):
    shp = g.shape
    t = g.reshape(shp[:ax] + (ndev, shp[ax] // ndev) + shp[ax + 1:])
    return jnp.moveaxis(t, ax, 0)


def kernel(x, p, ln1_g, ln1_b, ln2_g, ln2_b, ffn_up, ffn_conv_w, ffn_conv_b, ffn_down, ple_proj, ple_norm_g, ple_gate_w, ple_gate_b, even_w_in, even_w_out, rwkv_mu, rwkv_w0, rwkv_w2, rwkv_a0, rwkv_a2, rwkv_g2, rwkv_k_k, rwkv_k_a, rwkv_r_k, rwkv_gn_g, rwkv_gn_b, gdn_conv_w, gdn_A_log, gdn_dt_bias, gdn_norm_g, odd_w_in, odd_w_out, mamba_conv_w, mamba_conv_b, mamba_dt_bias, mamba_A_log, mamba_D, mamba_norm_g, lru_conv_w, lru_conv_b, lru_wa, lru_ba, lru_wx, lru_bx, lru_lambda, loss_target, m_ln1_g, m_ln1_b, m_ln2_g, m_ln2_b, m_ffn_up, m_ffn_conv_w, m_ffn_conv_b, m_ffn_down, m_ple_proj, m_ple_norm_g, m_ple_gate_w, m_ple_gate_b, m_even_w_in, m_even_w_out, m_rwkv_mu, m_rwkv_w0, m_rwkv_w2, m_rwkv_a0, m_rwkv_a2, m_rwkv_g2, m_rwkv_k_k, m_rwkv_k_a, m_rwkv_r_k, m_rwkv_gn_g, m_rwkv_gn_b, m_gdn_conv_w, m_gdn_A_log, m_gdn_dt_bias, m_gdn_norm_g, m_odd_w_in, m_odd_w_out, m_mamba_conv_w, m_mamba_conv_b, m_mamba_dt_bias, m_mamba_A_log, m_mamba_D, m_mamba_norm_g, m_lru_conv_w, m_lru_conv_b, m_lru_wa, m_lru_ba, m_lru_wx, m_lru_bx, m_lru_lambda, v_ln1_g, v_ln1_b, v_ln2_g, v_ln2_b, v_ffn_up, v_ffn_conv_w, v_ffn_conv_b, v_ffn_down, v_ple_proj, v_ple_norm_g, v_ple_gate_w, v_ple_gate_b, v_even_w_in, v_even_w_out, v_rwkv_mu, v_rwkv_w0, v_rwkv_w2, v_rwkv_a0, v_rwkv_a2, v_rwkv_g2, v_rwkv_k_k, v_rwkv_k_a, v_rwkv_r_k, v_rwkv_gn_g, v_rwkv_gn_b, v_gdn_conv_w, v_gdn_A_log, v_gdn_dt_bias, v_gdn_norm_g, v_odd_w_in, v_odd_w_out, v_mamba_conv_w, v_mamba_conv_b, v_mamba_dt_bias, v_mamba_A_log, v_mamba_D, v_mamba_norm_g, v_lru_conv_w, v_lru_conv_b, v_lru_wa, v_lru_ba, v_lru_wx, v_lru_bx, v_lru_lambda):
    args = locals()
    w = {n: args[n] for n in WEIGHTS}
    m = {n: args["m_" + n] for n in WEIGHTS}
    v = {n: args["v_" + n] for n in WEIGHTS}
    depth = ln1_g.shape[0]
    me = _my_index()
    blocks = {n: w[n].shape for n in WEIGHTS}

    as_rows = lambda t: t.reshape(-1, t.shape[-1])
    small = _to_flat([w[n] for n in SMALL_SHARDED], 16)
    gathered = _all_gather([as_rows(w[n].astype(BF16)) for n in MATRICES] + [small], "gather_params")
    w16 = {n: _blocks_to_full(g.reshape((N_DEV,) + blocks[n]), SHARD_AXIS[n]) for n, g in zip(MATRICES, gathered)}
    sp = _gathered_to_full(gathered[-1], SMALL_SHARDED, blocks)
    sp.update({n: w[n] for n in REPLICATED})

    lay = _matrix_layouts(w16, sp)
    wz = {k: jnp.zeros(shape, F32) for k, shape in lay['padded'].items()}
    y, vjp = jax.vjp(lambda x_, wz_, sp_: _forward(x_, wz_, sp_, p[:, 0], w16, depth), x[0], wz, sp)
    loss_local, dy = _loss_head(y, loss_target[0])
    dx, dwz, dsp = vjp(dy)
    loss = lax.psum(loss_local, ("x", "y", "c"))
    gfull = dict(dsp)
    gblocks = {}
    for n in MATRICES:
        layers = range(w16[n].shape[0])
        ax = SHARD_AXIS[n]
        if n == 'ffn_up':
            gblocks[n] = jnp.concatenate(
                [jnp.concatenate([_full_to_blocks(dwz[f"{k}.{j}"][None], ax, N_DEV // 2) for j in layers], axis=1)
                 for k in ("ffn_up", "ffn_up_val")], axis=0)
        else:
            gblocks[n] = jnp.concatenate(
                [_full_to_blocks(lay['unpad'][n](dwz[f"{n}.{j}"])[None], ax) for j in layers], axis=1)

    rep_flat = jnp.concatenate([gfull[n].reshape(-1) for n in REPLICATED])
    rep_n = rep_flat.shape[0]
    piece = _rup(rep_n, N_DEV * LANES) // N_DEV
    rep_pad = lambda t: jnp.pad(t, (0, N_DEV * piece - rep_n))
    small_parts = jnp.concatenate([_full_to_blocks(gfull[n], SHARD_AXIS[n]).reshape(N_DEV, -1) for n in SMALL_SHARDED]
                                  + [rep_pad(rep_flat).reshape(N_DEV, piece)], axis=1)
    n_flat = small_parts.shape[1]
    n_pad = _rup(n_flat, 16 * FLAT_COLS)
    small_parts = jnp.pad(small_parts, ((0, 0), (0, n_pad - n_flat)))

    def my_small(d):
        rep = rep_pad(jnp.concatenate([d[n].reshape(-1) for n in REPLICATED]))
        mine = lax.dynamic_slice(rep, (me * piece,), (piece,))
        flat = jnp.concatenate([d[n].reshape(-1) for n in SMALL_SHARDED] + [mine])
        return jnp.pad(flat, (0, n_pad - n_flat)).reshape(n_pad // FLAT_COLS, FLAT_COLS)

    by_chip = lambda t, cols: t.reshape(4, 2, -1, cols)
    parts = [by_chip(gblocks[n], blocks[n][-1]) for n in MATRICES]
    parts.append(by_chip(small_parts, FLAT_COLS))
    wire = [BF16] * len(MATRICES) + [F32]
    tags = MATRICES + ["small"]
    ids = _place_ids()
    recv_a = _sibling_exchange(parts, "reduce_sibling")
    first = [_chip_partials(pt, ra, ids, dt, f"chip_partials_{t}") for pt, ra, dt, t in zip(parts, recv_a, wire, tags)]
    recv1 = _neighbour_exchange(first, 2, "reduce_first_hop")
    recv2 = _neighbour_exchange([_second_hop(f, r1, f"second_hop_{t}") for f, r1, t in zip(first, recv1, tags)],
                                1, "reduce_second_hop")
    mine = [(as_rows(w[n]), as_rows(m[n]), as_rows(v[n])) for n in MATRICES] + [(my_small(w), my_small(m), my_small(v))]
    results = [_adamw(pt, ra, r1, r2, ids, *wmv, f"adamw_{t}")
               for pt, ra, r1, r2, wmv, t in zip(parts, recv_a, recv1, recv2, mine, tags)]
    small_res = [r.reshape(-1) for r in results[-1]]
    rep_res = jnp.stack([r[n_flat - piece:n_flat] for r in small_res])
    rep_rows = _rup(4 * piece, 8 * FLAT_COLS) // FLAT_COLS
    rep_blk = jnp.pad(rep_res.reshape(-1), (0, rep_rows * FLAT_COLS - 4 * piece)).reshape(rep_rows, FLAT_COLS)
    (rep_all,) = _all_gather([rep_blk], "gather_replicated")
    rep_all = rep_all.reshape(N_DEV, -1)[:, :4 * piece]
    rep_all = jnp.transpose(rep_all.reshape(N_DEV, 4, piece), (1, 0, 2)).reshape(4, N_DEV * piece)

    outs = [{}, {}, {}, {}]
    for k in range(4):
        for n, res in zip(MATRICES, results):
            outs[k][n] = res[k].reshape(blocks[n])
        pos = 0
        for n in SMALL_SHARDED:
            size = math.prod(blocks[n])
            outs[k][n] = small_res[k][pos:pos + size].reshape(blocks[n])
            pos += size
        pos = 0
        for n in REPLICATED:
            size = math.prod(blocks[n])
            outs[k][n] = rep_all[k, pos:pos + size].reshape(blocks[n])
            pos += size
    return (loss, dx[None], *[outs[0][n] for n in WEIGHTS], *[outs[1][n] for n in WEIGHTS],
            *[outs[2][n] for n in WEIGHTS], *[outs[3][n] for n in WEIGHTS])


def _matrix_layouts(w16, sp):
    padded, unpad = {}, {}
    ident = lambda g: g
    for n in ('ffn_down', 'ple_proj', 'ple_gate_w', 'even_w_out', 'odd_w_out'):
        for j in range(w16[n].shape[0]):
            padded[f"{n}.{j}"] = w16[n].shape[1:]
        unpad[n] = ident
    for j in range(w16['ffn_up'].shape[0]):
        half = (w16['ffn_up'].shape[1], w16['ffn_up'].shape[2] // 2)
        padded[f"ffn_up.{j}"] = padded[f"ffn_up_val.{j}"] = half
    for n in ('rwkv_w2', 'rwkv_a2', 'rwkv_g2'):
        rows, cols = w16[n].shape[1:]
        for j in range(w16[n].shape[0]):
            padded[f"{n}.{j}"] = (_rup(rows, LANES), cols)
        unpad[n] = functools.partial(lambda g, rows: g[:rows], rows=rows)
    ah, an = sp['rwkv_r_k'].shape[1:]
    bh, bn = sp['gdn_A_log'].shape[1], sp['gdn_norm_g'].shape[1]
    ew = [ah * an] * 3 + [w16['rwkv_w2'].shape[1], w16['rwkv_a2'].shape[1], w16['rwkv_g2'].shape[1]] + [bh * bn] * 4 + [bh, bh]
    cwid, ch = sp['mamba_norm_g'].shape[1], sp['mamba_dt_bias'].shape[1]
    dw = sp['lru_lambda'].shape[1]
    ow = [cwid, sp['mamba_conv_w'].shape[2], ch, dw, dw]
    for n, widths in (('even_w_in', ew), ('odd_w_in', ow)):
        offs, total = _aligned_layout(widths)
        for j in range(w16[n].shape[0]):
            padded[f"{n}.{j}"] = (w16[n].shape[1], total)
        unpad[n] = functools.partial(_unpad_cols, offs=offs, widths=widths)
    return {'padded': padded, 'unpad': unpad}
```

```python
import functools
import math

import jax
import jax.numpy as jnp
from jax import lax
from jax.experimental import pallas as pl
from jax.experimental.pallas import tpu as pltpu

F32 = jnp.float32
BF16 = jnp.bfloat16
HI = lax.Precision.HIGHEST
SDS = jax.ShapeDtypeStruct
MESH = pl.DeviceIdType.MESH

LANES = 128
VMEM_LIMIT = 56 * 1024 * 1024
N_DEV = 8
FLAT_COLS = 1024
MM_VMEM = 40 * 1024 * 1024

LN_EPS = 1e-5
RMS_EPS = 1e-6
L2_EPS = 1e-6
A_GN_EPS = 64e-5
LRU_C = 8.0
C_GROUPS = 4
RWKV_CHUNK = 64
GDN_CHUNK = 64
SSD_CHUNK = 64
SCAN_HEADS = 16

ADAM_LR, ADAM_B1, ADAM_B2, ADAM_EPS, ADAM_WD, ADAM_STEP = 0.001, 0.9, 0.999, 1e-08, 0.01, 10

WEIGHTS = ['ln1_g', 'ln1_b', 'ln2_g', 'ln2_b', 'ffn_up', 'ffn_conv_w', 'ffn_conv_b', 'ffn_down', 'ple_proj',
           'ple_norm_g', 'ple_gate_w', 'ple_gate_b', 'even_w_in', 'even_w_out', 'rwkv_mu', 'rwkv_w0', 'rwkv_w2',
           'rwkv_a0', 'rwkv_a2', 'rwkv_g2', 'rwkv_k_k', 'rwkv_k_a', 'rwkv_r_k', 'rwkv_gn_g', 'rwkv_gn_b',
           'gdn_conv_w', 'gdn_A_log', 'gdn_dt_bias', 'gdn_norm_g', 'odd_w_in', 'odd_w_out', 'mamba_conv_w',
           'mamba_conv_b', 'mamba_dt_bias', 'mamba_A_log', 'mamba_D', 'mamba_norm_g', 'lru_conv_w', 'lru_conv_b',
           'lru_wa', 'lru_ba', 'lru_wx', 'lru_bx', 'lru_lambda']
SHARD_AXIS = {'ffn_up': 2, 'ffn_conv_w': 2, 'ffn_down': 1, 'ple_proj': 2, 'ple_gate_w': 1, 'even_w_in': 2,
              'even_w_out': 1, 'rwkv_w2': 2, 'rwkv_a2': 2, 'rwkv_g2': 2, 'gdn_conv_w': 2, 'odd_w_in': 2,
              'odd_w_out': 1, 'mamba_conv_w': 2, 'mamba_conv_b': 1, 'mamba_norm_g': 1, 'lru_conv_w': 2,
              'lru_conv_b': 1, 'lru_ba': 1, 'lru_bx': 1, 'lru_lambda': 1}
MATRICES = ['ffn_up', 'ffn_down', 'ple_proj', 'ple_gate_w', 'even_w_in', 'even_w_out', 'rwkv_w2', 'rwkv_a2',
            'rwkv_g2', 'odd_w_in', 'odd_w_out']
SMALL_SHARDED = [n for n in WEIGHTS if n in SHARD_AXIS and n not in MATRICES]
REPLICATED = [n for n in WEIGHTS if n not in SHARD_AXIS]


def _cparams(sem):
    return pltpu.CompilerParams(dimension_semantics=sem, vmem_limit_bytes=VMEM_LIMIT)


def _rup(n, m):
    return -(-n // m) * m


def _pick(n, cands):
    for c in cands:
        if n % c == 0:
            return c
    return n


_DIMS = {'nn': (((1,), (0,)), ((), ())), 'nt': (((1,), (1,)), ((), ())), 'tn': (((0,), (0,)), ((), ()))}


def _mm(a, b, mode, name):
    if mode == 'tn':
        K, M = a.shape
    else:
        M, K = a.shape
    N = b.shape[0] if mode == 'nt' else b.shape[1]
    tm = _pick(M, (1024, 512, 256, 128))
    tn = _pick(N, (1024, 512, 256, 128))
    room = MM_VMEM - 3 * tm * tn * 4
    per_k = 2 * (tm * a.dtype.itemsize + tn * b.dtype.itemsize)
    tk = max([t for t in range(LANES, K + 1, LANES) if K % t == 0 and t * per_k <= room] or [K])
    nk = K // tk

    def body(a_ref, b_ref, o_ref, acc_ref):
        k = pl.program_id(2)
        part = lax.dot_general(a_ref[...].astype(BF16), b_ref[...].astype(BF16), _DIMS[mode],
                               preferred_element_type=F32)

        @pl.when(k == 0)
        def _():
            acc_ref[...] = part

        @pl.when(k > 0)
        def _():
            acc_ref[...] += part

        @pl.when(k == nk - 1)
        def _():
            o_ref[...] = acc_ref[...]

    a_spec = pl.BlockSpec((tk, tm), lambda i, j, k: (k, i)) if mode == 'tn' else pl.BlockSpec((tm, tk), lambda i, j, k: (i, k))
    b_spec = pl.BlockSpec((tn, tk), lambda i, j, k: (j, k)) if mode == 'nt' else pl.BlockSpec((tk, tn), lambda i, j, k: (k, j))
    return pl.pallas_call(
        body, grid=(M // tm, N // tn, nk), in_specs=[a_spec, b_spec],
        out_specs=pl.BlockSpec((tm, tn), lambda i, j, k: (i, j)), out_shape=SDS((M, N), F32),
        scratch_shapes=[pltpu.VMEM((tm, tn), F32)], name=name,
        compiler_params=_cparams(("parallel", "parallel", "arbitrary")))(a, b)


def _matmul(x, w16, wz, name):
    @jax.custom_vjp
    def op(x, wz):
        return _mm(x.astype(BF16), w16, 'nn', name + "_f")

    def op_f(x, wz):
        x16 = x.astype(BF16)
        return _mm(x16, w16, 'nn', name + "_f"), x16

    def op_b(x16, g):
        g16 = g.astype(BF16)
        return _mm(g16, w16, 'nt', name + "_dx"), _mm(x16, g16, 'tn', name + "_dw")

    op.defvjp(op_f, op_b)
    return op(x, wz)


def _block_diag_pair(x, wa, wx, name):
    T = x.shape[0]
    nb, bd, _ = wa.shape
    dot = lambda a, b, mode: lax.dot_general(a.astype(BF16), b.astype(BF16), _DIMS[mode], preferred_element_type=F32)
    cols = pl.BlockSpec((T, bd), lambda n: (0, n))
    blk = pl.BlockSpec((1, bd, bd), lambda n: (n, 0, 0))

    def fwd_call(x, wa, wx):
        def body(x_ref, wa_ref, wx_ref, ra_ref, rx_ref):
            ra_ref[...] = dot(x_ref[...], wa_ref[0], 'nn')
            rx_ref[...] = dot(x_ref[...], wx_ref[0], 'nn')

        return pl.pallas_call(body, grid=(nb,), in_specs=[cols, blk, blk], out_specs=[cols, cols],
                              out_shape=[SDS(x.shape, F32)] * 2, name=name + "_f",
                              compiler_params=_cparams(("parallel",)))(x, wa, wx)

    def bwd_call(x, wa, wx, ga, gx):
        def body(x_ref, wa_ref, wx_ref, ga_ref, gx_ref, dx_ref, dwa_ref, dwx_ref):
            dx_ref[...] = dot(ga_ref[...], wa_ref[0], 'nt') + dot(gx_ref[...], wx_ref[0], 'nt')
            dwa_ref[0] = dot(x_ref[...], ga_ref[...], 'tn')
            dwx_ref[0] = dot(x_ref[...], gx_ref[...], 'tn')

        return pl.pallas_call(body, grid=(nb,), in_specs=[cols, blk, blk, cols, cols], out_specs=[cols, blk, blk],
                              out_shape=[SDS(x.shape, F32), SDS(wa.shape, F32), SDS(wx.shape, F32)], name=name + "_b",
                              compiler_params=_cparams(("parallel",)))(x, wa, wx, ga, gx)

    @jax.custom_vjp
    def op(x, wa, wx):
        return tuple(fwd_call(x, wa, wx))

    def op_f(x, wa, wx):
        return op(x, wa, wx), (x, wa, wx)

    def op_b(res, g):
        return tuple(bwd_call(*res, *g))

    op.defvjp(op_f, op_b)
    return op(x, wa, wx)


def _tile_op(name, fn, arrs, params, consts, by_rows, width=LANES):
    arrs, params, consts = tuple(arrs), tuple(params), tuple(consts)
    na, npar, nc = len(arrs), len(params), len(consts)
    T = arrs[0].shape[0]
    if by_rows:
        tile = _pick(T, (256, 128, 64, 32, 16, 8))
        grid = (T // tile,)
        arr_block = lambda a: (tile, a.shape[1])
        arr_spec = lambda a: pl.BlockSpec((tile, a.shape[1]), lambda i: (i, 0))
        par_block = lambda p: p.shape
        par_spec = lambda p: pl.BlockSpec(p.shape, lambda i: (0, 0))
    else:
        grid = (arrs[0].shape[1] // width,)
        arr_block = lambda a: (T, width)
        arr_spec = lambda a: pl.BlockSpec((T, width), lambda i: (0, i))
        par_block = lambda p: (p.shape[0], width)
        par_spec = lambda p: pl.BlockSpec((p.shape[0], width), lambda i: (0, i))
    const_spec = lambda c: pl.BlockSpec(c.shape, lambda i: (0,) * c.ndim)
    outs_sds = jax.eval_shape(fn, *[SDS(arr_block(a), F32) for a in arrs], *[SDS(par_block(p), F32) for p in params],
                              *[SDS(c.shape, c.dtype) for c in consts])
    out_widths = [o.shape[1] for o in outs_sds]
    nout = len(out_widths)
    if by_rows:
        out_shapes = [SDS((T, w), F32) for w in out_widths]
        out_specs = [pl.BlockSpec((tile, w), lambda i: (i, 0)) for w in out_widths]
    else:
        out_shapes = [SDS((T, grid[0] * w), F32) for w in out_widths]
        out_specs = [pl.BlockSpec((T, w), lambda i: (0, i)) for w in out_widths]

    def fwd_call(arrs, params):
        def body(*refs):
            outs = fn(*[r[...] for r in refs[:na + npar + nc]])
            for o_ref, o in zip(refs[na + npar + nc:], outs):
                o_ref[...] = o

        return pl.pallas_call(
            body, grid=grid, in_specs=[arr_spec(a) for a in arrs] + [par_spec(p) for p in params] + [const_spec(c) for c in consts],
            out_specs=out_specs, out_shape=out_shapes, name=name + "_f",
            compiler_params=_cparams(("parallel",)))(*arrs, *params, *consts)

    def bwd_call(arrs, params, cts):
        def body(*refs):
            ins = refs[:na + npar + nc + nout]
            outs = refs[na + npar + nc + nout:]
            av = [r[...] for r in ins[:na]]
            pv = [r[...] for r in ins[na:na + npar]]
            cv = [r[...] for r in ins[na + npar:na + npar + nc]]
            gv = [r[...] for r in ins[na + npar + nc:]]
            _, vjp = jax.vjp(lambda *t: fn(*t, *cv), *av, *pv)
            grads = vjp(tuple(gv))
            for o_ref, g in zip(outs[:na], grads[:na]):
                o_ref[...] = g
            if by_rows and npar:
                @pl.when(pl.program_id(0) == 0)
                def _():
                    for o_ref in outs[na:]:
                        o_ref[...] = jnp.zeros_like(o_ref)

                for o_ref, g in zip(outs[na:], grads[na:]):
                    o_ref[...] += g
            else:
                for o_ref, g in zip(outs[na:], grads[na:]):
                    o_ref[...] = g

        return pl.pallas_call(
            body, grid=grid,
            in_specs=[arr_spec(a) for a in arrs] + [par_spec(p) for p in params] + [const_spec(c) for c in consts] + out_specs,
            out_specs=[arr_spec(a) for a in arrs] + [par_spec(p) for p in params],
            out_shape=[SDS(a.shape, F32) for a in arrs] + [SDS(p.shape, F32) for p in params], name=name + "_b",
            compiler_params=_cparams(("arbitrary",) if by_rows else ("parallel",)))(*arrs, *params, *consts, *cts)

    @jax.custom_vjp
    def op(arrs, params):
        return tuple(fwd_call(arrs, params))

    def op_f(arrs, params):
        return op(arrs, params), (arrs, params)

    def op_b(res, cts):
        arrs, params = res
        g = bwd_call(arrs, params, cts)
        return tuple(g[:na]), tuple(g[na:])

    op.defvjp(op_f, op_b)
    return op(arrs, params)


def _rowwise(name, fn, arrs, params=(), consts=()):
    return _tile_op(name, fn, arrs, params, consts, True)


def _colwise(name, fn, arrs, params=(), width=LANES):
    return _tile_op(name, fn, arrs, params, (), False, width)


@functools.partial(jax.custom_vjp, nondiff_argnums=(1,))
def _shift(x, k):
    rows = lax.broadcasted_iota(jnp.int32, x.shape, 0)
    return jnp.where(rows >= k, pltpu.roll(x, k, 0), 0.0)


def _shift_f(x, k):
    return _shift(x, k), None


def _shift_b(k, _, g):
    n = g.shape[0]
    rows = lax.broadcasted_iota(jnp.int32, g.shape, 0)
    return (jnp.where(rows < n - k, pltpu.roll(g, n - k, 0), 0.0),)


_shift.defvjp(_shift_f, _shift_b)


def _causal_conv(x, w):
    K = w.shape[0]
    y = x * w[K - 1:K, :]
    for j in range(K - 1):
        y = y + _shift(x, K - 1 - j) * w[j:j + 1, :]
    return y


def _silu(x):
    return x * jax.nn.sigmoid(x)


def _softplus(x):
    return jnp.maximum(x, 0.0) + jnp.log1p(jnp.exp(-jnp.abs(x)))


def _split_cols(h, offs, widths):
    @jax.custom_vjp
    def op(h):
        return tuple(h[:, o:o + w] for o, w in zip(offs, widths))

    def op_f(h):
        return op(h), None

    def op_b(_, cts):
        parts, pos = [], 0
        T = cts[0].shape[0]
        for o, w, c in zip(offs, widths, cts):
            if o > pos:
                parts.append(jnp.zeros((T, o - pos), F32))
            parts.append(c)
            pos = o + w
        if pos < h.shape[1]:
            parts.append(jnp.zeros((T, h.shape[1] - pos), F32))
        return (jnp.concatenate(parts, axis=1),)

    op.defvjp(op_f, op_b)
    return op(h)


def _group_ones(width, group):
    g = jnp.arange(width) // group
    return (g[:, None] == g[None, :]).astype(F32)


def _layer_norm_rows(x, g, b, eps):
    mu = jnp.mean(x, axis=1, keepdims=True)
    var = jnp.mean(jnp.square(x - mu), axis=1, keepdims=True)
    return (x - mu) * lax.rsqrt(var + eps) * g + b


def _tri(L, strict=False):
    i = lax.broadcasted_iota(jnp.int32, (L, L), 0)
    j = lax.broadcasted_iota(jnp.int32, (L, L), 1)
    return (i > j) if strict else (i >= j)


def _cumsum_rows(x):
    H, L, _ = x.shape
    tri = jnp.broadcast_to(_tri(L).astype(F32)[None], (H, L, L))
    return jnp.einsum('hls,hsn->hln', tri, x, precision=HI)


def _col_to_row(c):
    L = c.shape[1]
    return jnp.sum(c * _tri_eye(L)[None], axis=1, keepdims=True)


def _row_to_col(r):
    N = r.shape[2]
    return jnp.sum(r * _tri_eye(N)[None], axis=2, keepdims=True)


def _scalar_col(t):
    return _row_to_col(t.reshape(t.shape[0], 1, t.shape[3]))


def _tri_eye(L):
    i = lax.broadcasted_iota(jnp.int32, (L, L), 0)
    j = lax.broadcasted_iota(jnp.int32, (L, L), 1)
    return (i == j).astype(F32)


def _unit_lower_inverse(n_strict):
    L = n_strict.shape[1]
    inv = _tri_eye(L)[None] + n_strict
    x = n_strict
    p = 2
    while p < L:
        x = jnp.einsum('hij,hjk->hik', x, x)
        inv = inv + jnp.einsum('hij,hjk->hik', inv, x)
        p *= 2
    return inv


def _rwkv_chunk(r, lw, k, v, a, b, h0):
    L = r.shape[1]
    mm = jnp.einsum
    cum = _cumsum_rows(lw)
    cum_l = jnp.sum(lw, axis=1, keepdims=True)
    e_neg = jnp.exp(-cum)
    rt, bt, kt, at = r * jnp.exp(cum), b * e_neg, k * e_neg, a * jnp.exp(cum - lw)
    to_end = jnp.exp(cum_l - cum)
    strict, incl = _tri(L, True)[None], _tri(L)[None]
    n = jnp.where(strict, mm('hld,hsd->hls', at, bt), 0.0)
    mk = jnp.where(strict, mm('hld,hsd->hls', at, kt), 0.0)
    u = mm('hls,hsv->hlv', _unit_lower_inverse(n), mm('hld,hdv->hlv', at, h0) + mm('hls,hsv->hlv', mk, v))
    y = (mm('hld,hdv->hlv', rt, h0) + mm('hls,hsv->hlv', jnp.where(incl, mm('hld,hsd->hls', rt, bt), 0.0), u)
         + mm('hls,hsv->hlv', jnp.where(incl, mm('hld,hsd->hls', rt, kt), 0.0), v))
    h1 = (_row_to_col(jnp.exp(cum_l)) * h0 + mm('hld,hlv->hdv', b * to_end, u) + mm('hld,hlv->hdv', k * to_end, v))
    return y, h1


def _gdn_chunk(q, k, v, beta, lg, h0):
    C, D = q.shape[1], q.shape[2]
    scale = D ** -0.5
    beta, lg = _scalar_col(beta), _scalar_col(lg)
    gc = _cumsum_rows(lg)
    gc_l = jnp.sum(lg, axis=1, keepdims=True)
    causal, strict = _tri(C)[None], _tri(C, True)[None]
    decay = jnp.exp(jnp.where(causal, gc - _col_to_row(gc), -jnp.inf))
    k_beta = k * beta
    m = jnp.where(strict, jnp.einsum('hcd,hsd->hcs', k_beta, k) * decay, 0.0)
    inv = _unit_lower_inverse(-m)
    e_gc = jnp.exp(gc)
    u = jnp.einsum('hcs,hsd->hcd', inv, v * beta)
    w = jnp.einsum('hcs,hsd->hcd', inv, k_beta * e_gc)
    attn = jnp.where(causal, jnp.einsum('hcd,hsd->hcs', q * scale, k) * decay, 0.0)
    v_new = u - jnp.einsum('hcd,hde->hce', w, h0)
    o = jnp.einsum('hcd,hde->hce', q * scale * e_gc, h0) + jnp.einsum('hcs,hse->hce', attn, v_new)
    h1 = h0 * jnp.exp(gc_l) + jnp.einsum('hcd,hce->hde', k * jnp.exp(gc_l - gc), v_new)
    return o, h1


def _ssd_chunk(xs, dt, aa, bm, cm, h0):
    H, L, _ = xs.shape
    dt, aa = _scalar_col(dt), _scalar_col(aa)
    x = xs * dt
    cs = _cumsum_rows(aa)
    cs_l = jnp.sum(aa, axis=1, keepdims=True)
    causal = _tri(L)[None]
    cb = jnp.einsum('gln,gsn->gls', cm, bm)
    wd = jnp.where(causal, cb * jnp.exp(jnp.where(causal, cs - _col_to_row(cs), -jnp.inf)), 0.0)
    cmb = jnp.broadcast_to(cm, (H,) + cm.shape[1:])
    bmb = jnp.broadcast_to(bm, (H,) + bm.shape[1:])
    y = jnp.einsum('hls,hsp->hlp', wd, x) + jnp.einsum('hln,hnp->hlp', cmb, h0) * jnp.exp(cs)
    h1 = jnp.exp(cs_l) * h0 + jnp.einsum('hln,hlp->hnp', bmb, x * jnp.exp(cs_l - cs))
    return y, h1


def _chunk_scan(name, fn, seqs, hb, L, state_shape, out_width):
    seqs = tuple(seqs)
    ns = len(seqs)
    H = max(s.shape[0] for s in seqs)
    T = max(s.shape[1] for s in seqs)
    nc, nh = T // L, H // hb
    lead = [hb if s.shape[0] == H else 1 for s in seqs]
    st_block = (hb,) + state_shape

    def seq_spec(s, l, imap):
        if s.ndim == 4:
            return pl.BlockSpec((l, 1, 1, L), lambda h, c: imap(h, c) + (0,))
        return pl.BlockSpec((l, L, s.shape[2]), imap)

    fmap = lambda h, c: (h, c, 0)
    rmap = lambda h, c: (h, nc - 1 - c, 0)

    def fwd_call(seqs):
        def body(*refs):
            y_ref, st_ref, carry = refs[ns], refs[ns + 1], refs[ns + 2]

            @pl.when(pl.program_id(1) == 0)
            def _():
                carry[...] = jnp.zeros_like(carry)

            h0 = carry[...]
            st_ref[0] = h0
            y, h1 = fn(*[r[...] for r in refs[:ns]], h0)
            y_ref[...] = y
            carry[...] = h1

        return pl.pallas_call(
            body, grid=(nh, nc), in_specs=[seq_spec(s, l, fmap) for s, l in zip(seqs, lead)],
            out_specs=[pl.BlockSpec((hb, L, out_width), fmap),
                       pl.BlockSpec((1,) + st_block, lambda h, c: (c, h) + (0,) * len(state_shape))],
            out_shape=[SDS((H, T, out_width), F32), SDS((nc, H) + state_shape, F32)],
            scratch_shapes=[pltpu.VMEM(st_block, F32)], name=name + "_f",
            compiler_params=_cparams(("parallel", "arbitrary")))(*seqs)

    def bwd_call(seqs, states, dy):
        def body(*refs):
            st_ref, dy_ref = refs[ns], refs[ns + 1]
            outs, carry = refs[ns + 2:2 * ns + 2], refs[2 * ns + 2]

            @pl.when(pl.program_id(1) == 0)
            def _():
                carry[...] = jnp.zeros_like(carry)

            _, vjp = jax.vjp(fn, *[r[...] for r in refs[:ns]], st_ref[0])
            grads = vjp((dy_ref[...], carry[...]))
            for o_ref, g in zip(outs, grads[:ns]):
                o_ref[...] = g
            carry[...] = grads[ns]

        return pl.pallas_call(
            body, grid=(nh, nc),
            in_specs=[seq_spec(s, l, rmap) for s, l in zip(seqs, lead)]
            + [pl.BlockSpec((1,) + st_block, lambda h, c: (nc - 1 - c, h) + (0,) * len(state_shape)),
               pl.BlockSpec((hb, L, out_width), rmap)],
            out_specs=[seq_spec(s, l, rmap) for s, l in zip(seqs, lead)],
            out_shape=[SDS(s.shape, F32) for s in seqs],
            scratch_shapes=[pltpu.VMEM(st_block, F32)], name=name + "_b",
            compiler_params=_cparams(("parallel", "arbitrary")))(*seqs, states, dy)

    @jax.custom_vjp
    def op(seqs):
        return fwd_call(seqs)[0]

    def op_f(seqs):
        y, states = fwd_call(seqs)
        return y, (seqs, states)

    def op_b(res, dy):
        seqs, states = res
        return (tuple(bwd_call(seqs, states, dy)),)

    op.defvjp(op_f, op_b)
    return op(seqs)


def _lru_scan_call(a, u, h, reverse, name):
    T, C = a.shape
    cw = _pick(C, (1024, 512, 256, 128))
    tt = _pick(T, (512, 256, 128, 64, 32, 16, 8))
    nt, ng = T // tt, tt // 8
    sub = lambda: lax.broadcasted_iota(jnp.int32, (8, cw), 0)
    first = lambda: pl.program_id(1) == 0

    def fwd_body(a_ref, u_ref, h_ref, carry_ref):
        @pl.when(first())
        def _():
            carry_ref[...] = jnp.zeros_like(carry_ref)

        def group(i, carry):
            r0 = pl.multiple_of(i * 8, 8)
            ab, ub = a_ref[pl.ds(r0, 8), :], u_ref[pl.ds(r0, 8), :]
            out = jnp.zeros((8, cw), F32)
            for j in range(8):
                carry = ab[j:j + 1, :] * carry + ub[j:j + 1, :]
                out = jnp.where(sub() == j, carry, out)
            h_ref[pl.ds(r0, 8), :] = out
            return carry

        carry_ref[...] = lax.fori_loop(0, ng, group, carry_ref[...])

    def bwd_body(a_ref, u_ref, h_ref, hp_ref, g_ref, da_ref, cg_ref, ca_ref):
        @pl.when(first())
        def _():
            cg_ref[...] = jnp.zeros_like(cg_ref)
            ca_ref[...] = jnp.zeros_like(ca_ref)

        h_before = jnp.where(pl.program_id(1) < nt - 1, hp_ref[7:8, :], 0.0)

        def group(i, carry):
            g_next, a_next = carry
            gi = ng - 1 - i
            r0 = pl.multiple_of(gi * 8, 8)
            rp = pl.multiple_of(jnp.maximum(gi - 1, 0) * 8, 8)
            ab, ub, hb = a_ref[pl.ds(r0, 8), :], u_ref[pl.ds(r0, 8), :], h_ref[pl.ds(r0, 8), :]
            h_last_prev = jnp.where(gi > 0, h_ref[pl.ds(rp, 8), :][7:8, :], h_before)
            g_out = jnp.zeros((8, cw), F32)
            da_out = jnp.zeros((8, cw), F32)
            for j in range(7, -1, -1):
                g_next = ub[j:j + 1, :] + a_next * g_next
                a_next = ab[j:j + 1, :]
                h_prev = hb[j - 1:j, :] if j > 0 else h_last_prev
                g_out = jnp.where(sub() == j, g_next, g_out)
                da_out = jnp.where(sub() == j, g_next * h_prev, da_out)
            g_ref[pl.ds(r0, 8), :] = g_out
            da_ref[pl.ds(r0, 8), :] = da_out
            return g_next, a_next

        cg_ref[...], ca_ref[...] = lax.fori_loop(0, ng, group, (cg_ref[...], ca_ref[...]))

    row = pltpu.VMEM((1, cw), F32)
    if not reverse:
        spec = pl.BlockSpec((tt, cw), lambda i, t: (t, i))
        return pl.pallas_call(fwd_body, grid=(C // cw, nt), in_specs=[spec, spec], out_specs=spec,
                              out_shape=SDS((T, C), F32), scratch_shapes=[row], name=name,
                              compiler_params=_cparams(("parallel", "arbitrary")))(a, u)
    spec = pl.BlockSpec((tt, cw), lambda i, t: (nt - 1 - t, i))
    before = pl.BlockSpec((8, cw), lambda i, t: (jnp.maximum((nt - 1 - t) * ng - 1, 0), i))
    return pl.pallas_call(bwd_body, grid=(C // cw, nt), in_specs=[spec, spec, spec, before], out_specs=[spec, spec],
                          out_shape=[SDS((T, C), F32), SDS((T, C), F32)], scratch_shapes=[row, row], name=name,
                          compiler_params=_cparams(("parallel", "arbitrary")))(a, u, h, h)


@jax.custom_vjp
def _lru_scan(a, u):
    return _lru_scan_call(a, u, None, False, "lru_scan_f")


def _lru_scan_f(a, u):
    h = _lru_scan(a, u)
    return h, (a, h)


def _lru_scan_b(res, dh):
    a, h = res
    g, da = _lru_scan_call(a, dh, h, True, "lru_scan_b")
    return da, g


_lru_scan.defvjp(_lru_scan_f, _lru_scan_b)


def _heads_major(x, nheads):
    T, W = x.shape
    return jnp.transpose(x.reshape(T, nheads, W // nheads), (1, 0, 2))


def _tokens_major(x):
    H, T, N = x.shape
    return jnp.transpose(x, (1, 0, 2)).reshape(T, H * N)


def _pad_cols(w, offs, widths, total):
    parts, pos, src = [], 0, 0
    for o, wd in zip(offs, widths):
        if o > pos:
            parts.append(jnp.zeros((w.shape[0], o - pos), w.dtype))
        parts.append(w[:, src:src + wd])
        src += wd
        pos = o + wd
    if pos < total:
        parts.append(jnp.zeros((w.shape[0], total - pos), w.dtype))
    return jnp.concatenate(parts, axis=1)


def _unpad_cols(w, offs, widths):
    return jnp.concatenate([w[:, o:o + wd] for o, wd in zip(offs, widths)], axis=1)


def _aligned_layout(widths):
    offs, pos = [], 0
    for w in widths:
        offs.append(pos)
        pos += _rup(w, LANES)
    return offs, _rup(pos, 512)


def _pad_lanes(v, n):
    return jnp.pad(v, ((0, 0), (0, n - v.shape[1])))


def _even_mixer(x, q, wz, li):
    T = x.shape[0]
    ah, an = q['rwkv_r_k'].shape
    aw = ah * an
    bh, bn = q['gdn_A_log'].shape[0], q['gdn_norm_g'].shape[0]
    bw = bh * bn
    lw_, la_, lg_ = q['rwkv_w2'].shape[0], q['rwkv_a2'].shape[0], q['rwkv_g2'].shape[0]
    widths = [aw, aw, aw, lw_, la_, lg_, bw, bw, bw, bw, bh, bh]
    offs, total = _aligned_layout(widths)
    pw = [_rup(w, LANES) for w in widths]
    hcols = _matmul(x, _pad_cols(q['even_w_in'], offs, widths, total), wz['even_w_in'], f"even_in{li}")
    a_w = offs[6]
    a_cols, bq, bk, bv, bz, beta_raw, alpha_raw = _split_cols(hcols, [0] + offs[6:], [a_w] + pw[6:])

    mu = _pad_cols(q['rwkv_mu'][None], offs[:6], widths[:6], a_w)
    (xs,) = _colwise(f"rwkv_shift{li}", lambda h, m: (h + (_shift(h, 1) - h) * m,), [a_cols], [mu])
    r, k, v, w_lo, a_lo, g_lo = _split_cols(xs, offs[:6], pw[:6])
    tw, sg = _rowwise(f"rwkv_lora_act{li}", lambda w, g: (jnp.tanh(w), jax.nn.sigmoid(g)), [w_lo, g_lo])
    pad_rows = lambda w, n: jnp.pad(w, ((0, n - w.shape[0]), (0, 0)))
    wl = _matmul(tw, pad_rows(q['rwkv_w2'], pw[3]), wz['rwkv_w2'], f"rwkv_w2{li}")
    al = _matmul(a_lo, pad_rows(q['rwkv_a2'], pw[4]), wz['rwkv_a2'], f"rwkv_a2{li}")
    g = _matmul(sg, pad_rows(q['rwkv_g2'], pw[5]), wz['rwkv_g2'], f"rwkv_g2{li}")
    ones_a = _group_ones(aw, an)

    def pre(k, wl, al, w0, a0, k_k, k_a, ones):
        lw = -jnp.exp(-_softplus(-(w0 + wl)) - 0.5)
        a = jax.nn.sigmoid(a0 + al)
        kk = k * k_k
        kk = kk * lax.rsqrt(jnp.dot(kk * kk, ones) + L2_EPS)
        return lw, k * (1.0 + (a - 1.0) * k_a), -kk, kk * a

    lw, k2, sa, sb = _rowwise(f"rwkv_pre{li}", pre, [k, wl, al],
                              [q['rwkv_w0'][None], q['rwkv_a0'][None], q['rwkv_k_k'][None], q['rwkv_k_a'][None]], [ones_a])
    hm = lambda t: _heads_major(t, ah)
    out = _chunk_scan(f"rwkv_scan{li}", _rwkv_chunk, [hm(r), hm(lw), hm(k2), hm(v), hm(sa), hm(sb)],
                      min(ah, SCAN_HEADS), min(RWKV_CHUNK, T), (an, an), an)
    out = _tokens_major(out)

    def post(out, r, k2, v, g, gn_g, gn_b, r_k, ones):
        mean = jnp.dot(out, ones) * (1.0 / an)
        cen = out - mean
        var = jnp.dot(cen * cen, ones) * (1.0 / an)
        normed = cen * lax.rsqrt(var + A_GN_EPS) * gn_g + gn_b
        bonus = jnp.dot(r * k2 * r_k, ones) * v
        return ((normed + bonus) * g,)

    flat = lambda t: t.reshape(1, -1)
    (ya,) = _rowwise(f"rwkv_post{li}", post, [out, r, k2, v, g],
                     [flat(q['rwkv_gn_g']), flat(q['rwkv_gn_b']), flat(q['rwkv_r_k'])], [ones_a])

    cw = q['gdn_conv_w']

    def conv_l2(x, w):
        y = _silu(_causal_conv(x, w))
        return (y * lax.rsqrt(jnp.sum(y * y, axis=1, keepdims=True) + L2_EPS),)

    if bn == LANES:
        (gq,) = _colwise(f"gdn_conv_q{li}", conv_l2, [bq], [cw[:, :bw]])
        (gk,) = _colwise(f"gdn_conv_k{li}", conv_l2, [bk], [cw[:, bw:2 * bw]])
    else:
        raise NotImplementedError("gated DeltaNet head width must equal the lane count")
    (gv,) = _colwise(f"gdn_conv_v{li}", lambda x, w: (_silu(_causal_conv(x, w)),), [bv], [cw[:, 2 * bw:]])

    def gates(beta_raw, alpha_raw, a_log, dt_bias):
        return jax.nn.sigmoid(beta_raw), -jnp.exp(a_log) * _softplus(alpha_raw + dt_bias)

    beta, lg = _rowwise(f"gdn_gates{li}", gates, [beta_raw, alpha_raw],
                        [_pad_lanes(q['gdn_A_log'][None], pw[10]), _pad_lanes(q['gdn_dt_bias'][None], pw[11])])
    gl = min(GDN_CHUNK, T)
    col = lambda t: jnp.transpose(t[:, :bh]).reshape(bh, T // gl, 1, gl)
    hmb = lambda t: _heads_major(t, bh)
    o = _chunk_scan(f"gdn_scan{li}", _gdn_chunk, [hmb(gq), hmb(gk), hmb(gv), col(beta), col(lg)],
                    min(bh, SCAN_HEADS), gl, (bn, bn), bn)
    o = _tokens_major(o)
    ones_b = _group_ones(bw, bn)

    def gdn_post(o, z, ng, ones):
        ms = jnp.dot(o * o, ones) * (1.0 / bn)
        return (o * lax.rsqrt(ms + RMS_EPS) * ng * _silu(z),)

    (yb,) = _rowwise(f"gdn_post{li}", gdn_post, [o, bz], [jnp.tile(q['gdn_norm_g'][None], (1, bh))], [ones_b])
    return _matmul(jnp.concatenate([ya, yb], axis=1), q['even_w_out'], wz['even_w_out'], f"even_out{li}")


def _odd_mixer(x, q, wz, li):
    T = x.shape[0]
    ch = q['mamba_dt_bias'].shape[0]
    cwid = q['mamba_norm_g'].shape[0]
    cp = cwid // ch
    xbc_w = q['mamba_conv_w'].shape[1]
    cn = (xbc_w - cwid) // (2 * C_GROUPS)
    dw = q['lru_lambda'].shape[0]
    widths = [cwid, xbc_w, ch, dw, dw]
    offs, total = _aligned_layout(widths)
    pw = [_rup(w, LANES) for w in widths]
    hcols = _matmul(x, _pad_cols(q['odd_w_in'], offs, widths, total), wz['odd_w_in'], f"odd_in{li}")
    z, xbc, dt_raw, y_br, x_br = _split_cols(hcols, offs, pw)

    (xbc_c,) = _colwise(f"mamba_conv{li}", lambda x, w, b: (_silu(_causal_conv(x, w) + b),), [xbc],
                        [q['mamba_conv_w'], q['mamba_conv_b'][None]])
    gn = C_GROUPS * cn
    xs, bm, cm = _split_cols(xbc_c, [0, cwid, cwid + gn], [cwid, gn, gn])

    def dts(dt_raw, dt_bias, a_log):
        dt = _softplus(dt_raw + dt_bias)
        return dt, dt * (-jnp.exp(a_log))

    dt, aa = _rowwise(f"mamba_dt{li}", dts, [dt_raw],
                      [_pad_lanes(q['mamba_dt_bias'][None], pw[2]), _pad_lanes(q['mamba_A_log'][None], pw[2])])
    sl = min(SSD_CHUNK, T)
    col = lambda t: jnp.transpose(t[:, :ch]).reshape(ch, T // sl, 1, sl)
    y = _chunk_scan(f"ssd_scan{li}", _ssd_chunk,
                    [_heads_major(xs, ch), col(dt), col(aa), _heads_major(bm, C_GROUPS), _heads_major(cm, C_GROUPS)],
                    ch // C_GROUPS, sl, (cn, cp), cp)
    y = _tokens_major(y)
    gsz = cwid // C_GROUPS

    def mamba_post(y, xs, z, d, ng):
        yy = (y + xs * d) * _silu(z)
        lane = lax.broadcasted_iota(jnp.int32, yy.shape, 1)
        ms = jnp.zeros_like(yy)
        for gi in range(C_GROUPS):
            sel = (lane >= gi * gsz) & (lane < (gi + 1) * gsz)
            ms = jnp.where(sel, jnp.sum(jnp.where(sel, yy * yy, 0.0), axis=1, keepdims=True) * (1.0 / gsz), ms)
        return (yy * lax.rsqrt(ms + RMS_EPS) * ng,)

    (yc,) = _rowwise(f"mamba_post{li}", mamba_post, [y, xs, z],
                     [jnp.repeat(q['mamba_D'], cp)[None], q['mamba_norm_g'][None]])

    (xc,) = _colwise(f"lru_conv{li}", lambda x, w, b: (_causal_conv(x, w) + b,), [x_br],
                     [q['lru_conv_w'], q['lru_conv_b'][None]])
    ra, ia = _block_diag_pair(xc, q['lru_wa'], q['lru_wx'], f"lru_gates{li}")

    def lru_pre(ra, ia, xc, ba, bx, lam):
        r = jax.nn.sigmoid(ra + ba)
        i = jax.nn.sigmoid(ia + bx)
        log_a = LRU_C * r * (-_softplus(-lam))
        t = 2.0 * log_a
        series = t * (1.0 + t * (0.5 + t * (1.0 / 6.0 + t * (1.0 / 24.0 + t * (1.0 / 120.0 + t * (1.0 / 720.0))))))
        expm1 = jnp.where(t > -0.2, series, jnp.exp(t) - 1.0)
        return jnp.exp(log_a), jnp.sqrt(-expm1) * (i * xc)

    a, u = _rowwise(f"lru_pre{li}", lru_pre, [ra, ia, xc], [q['lru_ba'][None], q['lru_bx'][None], q['lru_lambda'][None]])
    h = _lru_scan(a, u)
    (yd,) = _rowwise(f"lru_post{li}", lambda h, y: (h * jax.nn.gelu(y),), [h, y_br])
    return _matmul(jnp.concatenate([yc, yd], axis=1), q['odd_w_out'], wz['odd_w_out'], f"odd_out{li}")


def _forward(x, wz, sp, p, w16, depth):
    alpha = (2.0 * depth) ** 0.25
    for i in range(depth):
        j = i // 2
        even = i % 2 == 0
        names = [n for n in WEIGHTS if n.startswith(('rwkv_', 'gdn_', 'even_') if even else ('mamba_', 'lru_', 'odd_'))]
        q = {n: (w16[n][j] if n in MATRICES else sp[n][j]) for n in names}
        wzl = {n: wz[f"{n}.{j}"] for n in names if n in MATRICES}
        y = (_even_mixer if even else _odd_mixer)(x, q, wzl, i)

        def ln_res(x, y, g, b):
            return (_layer_norm_rows(alpha * x + y, g, b, LN_EPS),)

        (h,) = _rowwise(f"ln1_{i}", ln_res, [x, y], [sp['ln1_g'][i][None], sp['ln1_b'][i][None]])
        dff = w16['ffn_up'].shape[2] // 2
        gate = _matmul(h, w16['ffn_up'][i][:, :dff], wz[f"ffn_up.{i}"], f"ffn_gate{i}")
        val = _matmul(h, w16['ffn_up'][i][:, dff:], wz[f"ffn_up_val.{i}"], f"ffn_val{i}")
        cw, cb = sp['ffn_conv_w'][i], sp['ffn_conv_b'][i][None]

        def ffn_act(gate, val, wg, wv, bg, bv):
            return (_silu(_causal_conv(gate, wg) + bg) * (_causal_conv(val, wv) + bv),)

        (act,) = _colwise(f"ffn_act{i}", ffn_act, [gate, val], [cw[:, :dff], cw[:, dff:], cb[:, :dff], cb[:, dff:]])
        f = _matmul(act, w16['ffn_down'][i], wz[f"ffn_down.{i}"], f"ffn_down{i}")
        (h2,) = _rowwise(f"ln2_{i}", ln_res, [h, f], [sp['ln2_g'][i][None], sp['ln2_b'][i][None]])
        e0 = _matmul(p[i], w16['ple_proj'][i], wz[f"ple_proj.{i}"], f"ple_proj{i}")
        gl = _matmul(h2, w16['ple_gate_w'][i], wz[f"ple_gate_w.{i}"], f"ple_gate{i}")

        def ple(h2, gl, e0, gb, ng):
            e = e0 * lax.rsqrt(jnp.mean(e0 * e0, axis=1, keepdims=True) + RMS_EPS) * ng
            return (h2 + jax.nn.sigmoid(gl + gb) * e,)

        (x,) = _rowwise(f"ple{i}", ple, [h2, gl, e0], [sp['ple_gate_b'][i][None], sp['ple_norm_g'][i][None]])
    return x


def _loss_head(y, target):
    T, D = y.shape
    tile = _pick(T, (256, 128, 64, 32, 16, 8))

    def body(y_ref, t_ref, dy_ref, l_ref):
        err = y_ref[...] - t_ref[...]
        dy_ref[...] = err * (1.0 / D)

        @pl.when(pl.program_id(0) == 0)
        def _():
            l_ref[...] = jnp.zeros_like(l_ref)

        l_ref[...] += jnp.sum(jnp.sum(err * err, axis=1, keepdims=True), axis=0, keepdims=True) * (0.5 / D) + jnp.zeros_like(l_ref)

    spec = pl.BlockSpec((tile, D), lambda i: (i, 0))
    dy, l = pl.pallas_call(body, grid=(T // tile,), in_specs=[spec, spec],
                           out_specs=[spec, pl.BlockSpec((8, LANES), lambda i: (0, 0))],
                           out_shape=[SDS((T, D), F32), SDS((8, LANES), F32)], name="loss_head",
                           compiler_params=_cparams(("arbitrary",)))(y, target)
    return l[0, 0], dy


def _my_index():
    return 4 * lax.axis_index("x") + 2 * lax.axis_index("y") + lax.axis_index("c")


def _hbm_specs(n):
    return [pl.BlockSpec(memory_space=pl.ANY)] * n


def _all_gather(blocks, name):
    blocks = tuple(blocks)
    n = len(blocks)
    half = [b.shape[0] // 2 for b in blocks]

    def body(*refs):
        ins, outs = refs[:n], refs[n:2 * n]
        send_sems, recv_sems, local_sems = refs[2 * n:]
        x, y, c = lax.axis_index("x"), lax.axis_index("y"), lax.axis_index("c")
        me, sibling, other = (x, y, c), (x, y, 1 - c), 1 - c
        xn, yn, dg = (1 - x, y), (x, 1 - y), (1 - x, 1 - y)

        def slot(i, px, py, pc, h=None):
            ref = outs[i].at[4 * px + 2 * py + pc]
            return ref if h is None else ref.at[pl.ds(h * half[i], half[i])]

        def copy(i, k, blk, to, h=None, src=None):
            dst = slot(i, *blk, h)
            return pltpu.make_async_remote_copy(
                src_ref=dst if src is None else src, dst_ref=dst, send_sem=send_sems.at[9 * i + k],
                recv_sem=recv_sems.at[9 * i + k], device_id=to, device_id_type=MESH)

        mine = [pltpu.make_async_copy(ins[i], slot(i, *me), local_sems.at[i]) for i in range(n)]
        sent = []
        for i in range(n):
            sent += [copy(i, 1, me, (*xn, c), src=ins[i]), copy(i, 2, me, (*yn, c), src=ins[i])]
        sent += [copy(i, 0, me, sibling, src=ins[i]) for i in range(n)]
        for cp in mine + sent:
            cp.start()

        def after(i, k_in, blk, h_in, forwards):
            copy(i, k_in, blk, me, h_in).wait_recv()
            for k_out, to, h_out in forwards:
                sent.append(copy(i, k_out, blk, to, h_out))
                sent[-1].start()

        for i in range(n):
            after(i, 1, (*xn, c), None, [(3, (*yn, c), 0), (5, sibling, None)])
        for i in range(n):
            after(i, 2, (*yn, c), None, [(4, (*xn, c), 1), (6, sibling, None)])
        for i in range(n):
            after(i, 3, (*dg, c), 0, [(7, sibling, 0)])
        for i in range(n):
            after(i, 4, (*dg, c), 1, [(8, sibling, 1)])
        for i in range(n):
            copy(i, 0, sibling, me).wait_recv()
            copy(i, 5, (*xn, other), me).wait_recv()
            copy(i, 6, (*yn, other), me).wait_recv()
            copy(i, 7, (*dg, other), me, 0).wait_recv()
            copy(i, 8, (*dg, other), me, 1).wait_recv()
        for cp in sent:
            cp.wait_send()
        for cp in mine:
            cp.wait()

    return pl.pallas_call(
        body, out_shape=[SDS((N_DEV,) + b.shape, b.dtype) for b in blocks], in_specs=_hbm_specs(n), out_specs=_hbm_specs(n),
        scratch_shapes=[pltpu.SemaphoreType.DMA((9 * n,)), pltpu.SemaphoreType.DMA((9 * n,)), pltpu.SemaphoreType.DMA((n,))],
        name=name)(*blocks)


def _sibling_exchange(parts, name):
    parts = tuple(parts)
    n = len(parts)

    def body(*refs):
        ins, outs = refs[:n], refs[n:2 * n]
        send_sems, recv_sems = refs[2 * n:]
        x, y, c = lax.axis_index("x"), lax.axis_index("y"), lax.axis_index("c")
        copies = [pltpu.make_async_remote_copy(
            src_ref=ins[i].at[q, 1 - c], dst_ref=outs[i].at[q], send_sem=send_sems.at[4 * i + q],
            recv_sem=recv_sems.at[4 * i + q], device_id=(x, y, 1 - c), device_id_type=MESH)
            for i in range(n) for q in range(4)]
        for cp in copies:
            cp.start()
        for cp in copies:
            cp.wait_recv()
        for cp in copies:
            cp.wait_send()

    return pl.pallas_call(
        body, out_shape=[SDS((4,) + p.shape[2:], p.dtype) for p in parts], in_specs=_hbm_specs(n), out_specs=_hbm_specs(n),
        scratch_shapes=[pltpu.SemaphoreType.DMA((4 * n,)), pltpu.SemaphoreType.DMA((4 * n,))], name=name)(*parts)


def _neighbour_exchange(bufs, k, name):
    bufs = tuple(bufs)
    n = len(bufs)

    def body(*refs):
        ins, outs = refs[:n], refs[n:2 * n]
        send_sems, recv_sems = refs[2 * n:]
        x, y, c = lax.axis_index("x"), lax.axis_index("y"), lax.axis_index("c")
        copies = [pltpu.make_async_remote_copy(
            src_ref=ins[i].at[pl.ds(d * k, k)], dst_ref=outs[i].at[pl.ds(d * k, k)], send_sem=send_sems.at[2 * i + d],
            recv_sem=recv_sems.at[2 * i + d], device_id=to, device_id_type=MESH)
            for d, to in enumerate([(1 - x, y, c), (x, 1 - y, c)]) for i in range(n)]
        for cp in copies:
            cp.start()
        for cp in copies:
            cp.wait_recv()
        for cp in copies:
            cp.wait_send()

    return pl.pallas_call(
        body, out_shape=[SDS((2 * k,) + b.shape[1:], b.dtype) for b in bufs], in_specs=_hbm_specs(n), out_specs=_hbm_specs(n),
        scratch_shapes=[pltpu.SemaphoreType.DMA((2 * n,)), pltpu.SemaphoreType.DMA((2 * n,))], name=name)(*bufs)


_FIRST_HOP_SLOT = ((0, 5), (4, 2), (1, 3))


def _place_ids():
    x, y, c = lax.axis_index("x"), lax.axis_index("y"), lax.axis_index("c")
    place = [c, 2 * x + y, 2 * (1 - x) + y, 2 * x + (1 - y), 2 * (1 - x) + (1 - y)]
    return jnp.stack(place + [jnp.int32(s) for pair in _FIRST_HOP_SLOT for s in pair]).astype(jnp.int32)


def _row_tile(rows, cols):
    best = None
    for t in range(16, rows + 1, 16):
        if rows % t == 0 and t * cols <= 256 * 1024:
            best = t
    return best or rows


def _chip_partials(part, recv_a, ids, dtype, name):
    _, _, R, C = part.shape
    tr = _row_tile(R // 2, C)
    nt = R // 2 // tr

    def body(ids_ref, p_ref, a_ref, o_ref):
        o_ref[...] = (p_ref[...] + a_ref[...]).astype(dtype)

    return pl.pallas_call(
        body, out_shape=SDS((6, R // 2, C), dtype),
        grid_spec=pltpu.PrefetchScalarGridSpec(
            num_scalar_prefetch=1, grid=(3, 2, nt),
            in_specs=[pl.BlockSpec((None, None, tr, C), lambda s, h, i, ids: (ids[2 + s], ids[0], h * nt + i, 0)),
                      pl.BlockSpec((None, tr, C), lambda s, h, i, ids: (ids[2 + s], h * nt + i, 0))],
            out_specs=pl.BlockSpec((None, tr, C), lambda s, h, i, ids: (ids[5 + 2 * s + h], i, 0))),
        name=name, compiler_params=_cparams(("parallel", "parallel", "parallel")))(ids, part, recv_a)


def _second_hop(first, recv1, name):
    _, R2, C = first.shape
    tr = _row_tile(R2, C)

    def body(f_ref, r_ref, o_ref):
        o_ref[...] = (f_ref[...].astype(F32) + r_ref[...].astype(F32)).astype(o_ref.dtype)

    return pl.pallas_call(
        body, out_shape=SDS((2, R2, C), first.dtype), grid=(2, R2 // tr),
        in_specs=[pl.BlockSpec((None, tr, C), lambda d, i: (5 - d, i, 0)),
                  pl.BlockSpec((None, tr, C), lambda d, i: (3 - 2 * d, i, 0))],
        out_specs=pl.BlockSpec((None, tr, C), lambda d, i: (d, i, 0)),
        name=name, compiler_params=_cparams(("parallel", "parallel")))(first, recv1)


def _adamw(part, recv_a, recv1, recv2, ids, w, m, v, name):
    R, C = w.shape
    tr = _row_tile(R // 2, C)
    nt = R // 2 // tr
    c1 = 1.0 / (1.0 - ADAM_B1 ** ADAM_STEP)
    c2 = 1.0 / (1.0 - ADAM_B2 ** ADAM_STEP)

    def body(ids_ref, p_ref, a_ref, x1_ref, y1_ref, x2_ref, y2_ref, w_ref, m_ref, v_ref, g_ref, d_ref, nm_ref, nv_ref):
        first = pl.program_id(0) == 0
        from_x = jnp.where(first, x1_ref[...], x2_ref[...]).astype(F32)
        from_y = jnp.where(first, y2_ref[...], y1_ref[...]).astype(F32)
        g = p_ref[...] + a_ref[...] + from_x + from_y
        nm = ADAM_B1 * m_ref[...] + (1.0 - ADAM_B1) * g
        nv = ADAM_B2 * v_ref[...] + (1.0 - ADAM_B2) * jnp.square(g)
        g_ref[...] = g
        nm_ref[...] = nm
        nv_ref[...] = nv
        d_ref[...] = -ADAM_LR * ((nm * c1) / (jnp.sqrt(nv * c2) + ADAM_EPS) + ADAM_WD * w_ref[...])

    spec = pl.BlockSpec((tr, C), lambda h, i, ids: (h * nt + i, 0))
    half = lambda slot: pl.BlockSpec((None, tr, C), lambda h, i, ids: (slot, i, 0))
    return pl.pallas_call(
        body, out_shape=[SDS((R, C), F32)] * 4,
        grid_spec=pltpu.PrefetchScalarGridSpec(
            num_scalar_prefetch=1, grid=(2, nt),
            in_specs=[pl.BlockSpec((None, None, tr, C), lambda h, i, ids: (ids[1], ids[0], h * nt + i, 0)),
                      pl.BlockSpec((None, tr, C), lambda h, i, ids: (ids[1], h * nt + i, 0)),
                      half(0), half(2), half(0), half(1), spec, spec, spec],
            out_specs=[spec] * 4),
        name=name, compiler_params=_cparams(("parallel", "parallel")))(ids, part, recv_a, recv1, recv1, recv2, recv2, w, m, v)


def _to_flat(vecs, quantum):
    flat = jnp.concatenate([v.reshape(-1) for v in vecs])
    n = _rup(flat.shape[0], quantum * FLAT_COLS)
    return jnp.pad(flat, (0, n - flat.shape[0])).reshape(n // FLAT_COLS, FLAT_COLS)


def _gathered_to_full(g, names, blocks):
    flat = g.reshape(N_DEV, -1)
    out, pos = {}, 0
    for n in names:
        shp = blocks[n]
        size = math.prod(shp)
        out[n] = _blocks_to_full(flat[:, pos:pos + size].reshape((N_DEV,) + shp), SHARD_AXIS[n])
        pos += size
    return out


def _blocks_to_full(g, ax):
    shp = g.shape[1:]
    return jnp.moveaxis(g, 0, ax).reshape(shp[:ax] + (N_DEV * shp[ax],) + shp[ax + 1:])


def _full_to_blocks(g, ax, ndev=N_DEV):
    shp = g.shape
    t = g.reshape(shp[:ax] + (ndev, shp[ax] // ndev) + shp[ax + 1:])
    return jnp.moveaxis(t, ax, 0)


def kernel(x, p, ln1_g, ln1_b, ln2_g, ln2_b, ffn_up, ffn_conv_w, ffn_conv_b, ffn_down, ple_proj, ple_norm_g, ple_gate_w, ple_gate_b, even_w_in, even_w_out, rwkv_mu, rwkv_w0, rwkv_w2, rwkv_a0, rwkv_a2, rwkv_g2, rwkv_k_k, rwkv_k_a, rwkv_r_k, rwkv_gn_g, rwkv_gn_b, gdn_conv_w, gdn_A_log, gdn_dt_bias, gdn_norm_g, odd_w_in, odd_w_out, mamba_conv_w, mamba_conv_b, mamba_dt_bias, mamba_A_log, mamba_D, mamba_norm_g, lru_conv_w, lru_conv_b, lru_wa, lru_ba, lru_wx, lru_bx, lru_lambda, loss_target, m_ln1_g, m_ln1_b, m_ln2_g, m_ln2_b, m_ffn_up, m_ffn_conv_w, m_ffn_conv_b, m_ffn_down, m_ple_proj, m_ple_norm_g, m_ple_gate_w, m_ple_gate_b, m_even_w_in, m_even_w_out, m_rwkv_mu, m_rwkv_w0, m_rwkv_w2, m_rwkv_a0, m_rwkv_a2, m_rwkv_g2, m_rwkv_k_k, m_rwkv_k_a, m_rwkv_r_k, m_rwkv_gn_g, m_rwkv_gn_b, m_gdn_conv_w, m_gdn_A_log, m_gdn_dt_bias, m_gdn_norm_g, m_odd_w_in, m_odd_w_out, m_mamba_conv_w, m_mamba_conv_b, m_mamba_dt_bias, m_mamba_A_log, m_mamba_D, m_mamba_norm_g, m_lru_conv_w, m_lru_conv_b, m_lru_wa, m_lru_ba, m_lru_wx, m_lru_bx, m_lru_lambda, v_ln1_g, v_ln1_b, v_ln2_g, v_ln2_b, v_ffn_up, v_ffn_conv_w, v_ffn_conv_b, v_ffn_down, v_ple_proj, v_ple_norm_g, v_ple_gate_w, v_ple_gate_b, v_even_w_in, v_even_w_out, v_rwkv_mu, v_rwkv_w0, v_rwkv_w2, v_rwkv_a0, v_rwkv_a2, v_rwkv_g2, v_rwkv_k_k, v_rwkv_k_a, v_rwkv_r_k, v_rwkv_gn_g, v_rwkv_gn_b, v_gdn_conv_w, v_gdn_A_log, v_gdn_dt_bias, v_gdn_norm_g, v_odd_w_in, v_odd_w_out, v_mamba_conv_w, v_mamba_conv_b, v_mamba_dt_bias, v_mamba_A_log, v_mamba_D, v_mamba_norm_g, v_lru_conv_w, v_lru_conv_b, v_lru_wa, v_lru_ba, v_lru_wx, v_lru_bx, v_lru_lambda):
    args = locals()
    w = {n: args[n] for n in WEIGHTS}
    m = {n: args["m_" + n] for n in WEIGHTS}
    v = {n: args["v_" + n] for n in WEIGHTS}
    depth = ln1_g.shape[0]
    me = _my_index()
    blocks = {n: w[n].shape for n in WEIGHTS}

    as_rows = lambda t: t.reshape(-1, t.shape[-1])
    small = _to_flat([w[n] for n in SMALL_SHARDED], 16)
    gathered = _all_gather([as_rows(w[n].astype(BF16)) for n in MATRICES] + [small], "gather_params")
    w16 = {n: _blocks_to_full(g.reshape((N_DEV,) + blocks[n]), SHARD_AXIS[n]) for n, g in zip(MATRICES, gathered)}
    sp = _gathered_to_full(gathered[-1], SMALL_SHARDED, blocks)
    sp.update({n: w[n] for n in REPLICATED})

    lay = _matrix_layouts(w16, sp)
    wz = {k: jnp.zeros(shape, F32) for k, shape in lay['padded'].items()}
    y, vjp = jax.vjp(lambda x_, wz_, sp_: _forward(x_, wz_, sp_, p[:, 0], w16, depth), x[0], wz, sp)
    loss_local, dy = _loss_head(y, loss_target[0])
    dx, dwz, dsp = vjp(dy)
    loss = lax.psum(loss_local, ("x", "y", "c"))
    gfull = dict(dsp)
    gblocks = {}
    for n in MATRICES:
        layers = range(w16[n].shape[0])
        ax = SHARD_AXIS[n]
        if n == 'ffn_up':
            gblocks[n] = jnp.concatenate(
                [jnp.concatenate([_full_to_blocks(dwz[f"{k}.{j}"][None], ax, N_DEV // 2) for j in layers], axis=1)
                 for k in ("ffn_up", "ffn_up_val")], axis=0)
        else:
            gblocks[n] = jnp.concatenate(
                [_full_to_blocks(lay['unpad'][n](dwz[f"{n}.{j}"])[None], ax) for j in layers], axis=1)

    rep_flat = jnp.concatenate([gfull[n].reshape(-1) for n in REPLICATED])
    rep_n = rep_flat.shape[0]
    piece = _rup(rep_n, N_DEV * LANES) // N_DEV
    rep_pad = lambda t: jnp.pad(t, (0, N_DEV * piece - rep_n))
    small_parts = jnp.concatenate([_full_to_blocks(gfull[n], SHARD_AXIS[n]).reshape(N_DEV, -1) for n in SMALL_SHARDED]
                                  + [rep_pad(rep_flat).reshape(N_DEV, piece)], axis=1)
    n_flat = small_parts.shape[1]
    n_pad = _rup(n_flat, 16 * FLAT_COLS)
    small_parts = jnp.pad(small_parts, ((0, 0), (0, n_pad - n_flat)))

    def my_small(d):
        rep = rep_pad(jnp.concatenate([d[n].reshape(-1) for n in REPLICATED]))
        mine = lax.dynamic_slice(rep, (me * piece,), (piece,))
        flat = jnp.concatenate([d[n].reshape(-1) for n in SMALL_SHARDED] + [mine])
        return jnp.pad(flat, (0, n_pad - n_flat)).reshape(n_pad // FLAT_COLS, FLAT_COLS)

    by_chip = lambda t, cols: t.reshape(4, 2, -1, cols)
    parts = [by_chip(gblocks[n], blocks[n][-1]) for n in MATRICES]
    parts.append(by_chip(small_parts, FLAT_COLS))
    wire = [BF16] * len(MATRICES) + [F32]
    tags = MATRICES + ["small"]
    ids = _place_ids()
    recv_a = _sibling_exchange(parts, "reduce_sibling")
    first = [_chip_partials(pt, ra, ids, dt, f"chip_partials_{t}") for pt, ra, dt, t in zip(parts, recv_a, wire, tags)]
    recv1 = _neighbour_exchange(first, 2, "reduce_first_hop")
    recv2 = _neighbour_exchange([_second_hop(f, r1, f"second_hop_{t}") for f, r1, t in zip(first, recv1, tags)],
                                1, "reduce_second_hop")
    mine = [(as_rows(w[n]), as_rows(m[n]), as_rows(v[n])) for n in MATRICES] + [(my_small(w), my_small(m), my_small(v))]
    results = [_adamw(pt, ra, r1, r2, ids, *wmv, f"adamw_{t}")
               for pt, ra, r1, r2, wmv, t in zip(parts, recv_a, recv1, recv2, mine, tags)]
    small_res = [r.reshape(-1) for r in results[-1]]
    rep_res = jnp.stack([r[n_flat - piece:n_flat] for r in small_res])
    rep_rows = _rup(4 * piece, 8 * FLAT_COLS) // FLAT_COLS
    rep_blk = jnp.pad(rep_res.reshape(-1), (0, rep_rows * FLAT_COLS - 4 * piece)).reshape(rep_rows, FLAT_COLS)
    (rep_all,) = _all_gather([rep_blk], "gather_replicated")
    rep_all = rep_all.reshape(N_DEV, -1)[:, :4 * piece]
    rep_all = jnp.transpose(rep_all.reshape(N_DEV, 4, piece), (1, 0, 2)).reshape(4, N_DEV * piece)

    outs = [{}, {}, {}, {}]
    for k in range(4):
        for n, res in zip(MATRICES, results):
            outs[k][n] = res[k].reshape(blocks[n])
        pos = 0
        for n in SMALL_SHARDED:
            size = math.prod(blocks[n])
            outs[k][n] = small_res[k][pos:pos + size].reshape(blocks[n])
            pos += size
        pos = 0
        for n in REPLICATED:
            size = math.prod(blocks[n])
            outs[k][n] = rep_all[k, pos:pos + size].reshape(blocks[n])
            pos += size
    return (loss, dx[None], *[outs[0][n] for n in WEIGHTS], *[outs[1][n] for n in WEIGHTS],
            *[outs[2][n] for n in WEIGHTS], *[outs[3][n] for n in WEIGHTS])


def _matrix_layouts(w16, sp):
    padded, unpad = {}, {}
    ident = lambda g: g
    for n in ('ffn_down', 'ple_proj', 'ple_gate_w', 'even_w_out', 'odd_w_out'):
        for j in range(w16[n].shape[0]):
            padded[f"{n}.{j}"] = w16[n].shape[1:]
        unpad[n] = ident
    for j in range(w16['ffn_up'].shape[0]):
        half = (w16['ffn_up'].shape[1], w16['ffn_up'].shape[2] // 2)
        padded[f"ffn_up.{j}"] = padded[f"ffn_up_val.{j}"] = half
    for n in ('rwkv_w2', 'rwkv_a2', 'rwkv_g2'):
        rows, cols = w16[n].shape[1:]
        for j in range(w16[n].shape[0]):
            padded[f"{n}.{j}"] = (_rup(rows, LANES), cols)
        unpad[n] = functools.partial(lambda g, rows: g[:rows], rows=rows)
    ah, an = sp['rwkv_r_k'].shape[1:]
    bh, bn = sp['gdn_A_log'].shape[1], sp['gdn_norm_g'].shape[1]
    ew = [ah * an] * 3 + [w16['rwkv_w2'].shape[1], w16['rwkv_a2'].shape[1], w16['rwkv_g2'].shape[1]] + [bh * bn] * 4 + [bh, bh]
    cwid, ch = sp['mamba_norm_g'].shape[1], sp['mamba_dt_bias'].shape[1]
    dw = sp['lru_lambda'].shape[1]
    ow = [cwid, sp['mamba_conv_w'].shape[2], ch, dw, dw]
    for n, widths in (('even_w_in', ew), ('odd_w_in', ow)):
        offs, total = _aligned_layout(widths)
        for j in range(w16[n].shape[0]):
            padded[f"{n}.{j}"] = (w16[n].shape[1], total)
        unpad[n] = functools.partial(_unpad_cols, offs=offs, widths=widths)
    return {'padded': padded, 'unpad': unpad}
```

```python
import functools
import math

import jax
import jax.numpy as jnp
from jax import lax
from jax.experimental import pallas as pl
from jax.experimental.pallas import tpu as pltpu

F32 = jnp.float32
BF16 = jnp.bfloat16
HI = lax.Precision.HIGHEST
SDS = jax.ShapeDtypeStruct
MESH = pl.DeviceIdType.MESH

LANES = 128
VMEM_LIMIT = 56 * 1024 * 1024
N_DEV = 8
FLAT_COLS = 1024
MM_VMEM = 40 * 1024 * 1024

LN_EPS = 1e-5
RMS_EPS = 1e-6
L2_EPS = 1e-6
A_GN_EPS = 64e-5
LRU_C = 8.0
C_GROUPS = 4
RWKV_CHUNK = 64
GDN_CHUNK = 64
SSD_CHUNK = 128
SCAN_HEADS = 16

ADAM_LR, ADAM_B1, ADAM_B2, ADAM_EPS, ADAM_WD, ADAM_STEP = 0.001, 0.9, 0.999, 1e-08, 0.01, 10

WEIGHTS = ['ln1_g', 'ln1_b', 'ln2_g', 'ln2_b', 'ffn_up', 'ffn_conv_w', 'ffn_conv_b', 'ffn_down', 'ple_proj',
           'ple_norm_g', 'ple_gate_w', 'ple_gate_b', 'even_w_in', 'even_w_out', 'rwkv_mu', 'rwkv_w0', 'rwkv_w2',
           'rwkv_a0', 'rwkv_a2', 'rwkv_g2', 'rwkv_k_k', 'rwkv_k_a', 'rwkv_r_k', 'rwkv_gn_g', 'rwkv_gn_b',
           'gdn_conv_w', 'gdn_A_log', 'gdn_dt_bias', 'gdn_norm_g', 'odd_w_in', 'odd_w_out', 'mamba_conv_w',
           'mamba_conv_b', 'mamba_dt_bias', 'mamba_A_log', 'mamba_D', 'mamba_norm_g', 'lru_conv_w', 'lru_conv_b',
           'lru_wa', 'lru_ba', 'lru_wx', 'lru_bx', 'lru_lambda']
SHARD_AXIS = {'ffn_up': 2, 'ffn_conv_w': 2, 'ffn_down': 1, 'ple_proj': 2, 'ple_gate_w': 1, 'even_w_in': 2,
              'even_w_out': 1, 'rwkv_w2': 2, 'rwkv_a2': 2, 'rwkv_g2': 2, 'gdn_conv_w': 2, 'odd_w_in': 2,
              'odd_w_out': 1, 'mamba_conv_w': 2, 'mamba_conv_b': 1, 'mamba_norm_g': 1, 'lru_conv_w': 2,
              'lru_conv_b': 1, 'lru_ba': 1, 'lru_bx': 1, 'lru_lambda': 1}
MATRICES = ['ffn_up', 'ffn_down', 'ple_proj', 'ple_gate_w', 'even_w_in', 'even_w_out', 'rwkv_w2', 'rwkv_a2',
            'rwkv_g2', 'odd_w_in', 'odd_w_out']
SMALL_SHARDED = [n for n in WEIGHTS if n in SHARD_AXIS and n not in MATRICES]
REPLICATED = [n for n in WEIGHTS if n not in SHARD_AXIS]


def _cparams(sem):
    return pltpu.CompilerParams(dimension_semantics=sem, vmem_limit_bytes=VMEM_LIMIT)


def _rup(n, m):
    return -(-n // m) * m


def _pick(n, cands):
    for c in cands:
        if n % c == 0:
            return c
    return n


_DIMS = {'nn': (((1,), (0,)), ((), ())), 'nt': (((1,), (1,)), ((), ())), 'tn': (((0,), (0,)), ((), ()))}


def _mm(a, b, mode, name):
    if mode == 'tn':
        K, M = a.shape
    else:
        M, K = a.shape
    N = b.shape[0] if mode == 'nt' else b.shape[1]
    tm = _pick(M, (1024, 512, 256, 128))
    tn = _pick(N, (1024, 512, 256, 128))
    room = MM_VMEM - 3 * tm * tn * 4
    per_k = 2 * (tm * a.dtype.itemsize + tn * b.dtype.itemsize)
    tk = max([t for t in range(LANES, K + 1, LANES) if K % t == 0 and t * per_k <= room] or [K])
    nk = K // tk

    def body(a_ref, b_ref, o_ref, acc_ref):
        k = pl.program_id(2)
        part = lax.dot_general(a_ref[...].astype(BF16), b_ref[...].astype(BF16), _DIMS[mode],
                               preferred_element_type=F32)

        @pl.when(k == 0)
        def _():
            acc_ref[...] = part

        @pl.when(k > 0)
        def _():
            acc_ref[...] += part

        @pl.when(k == nk - 1)
        def _():
            o_ref[...] = acc_ref[...]

    a_spec = pl.BlockSpec((tk, tm), lambda i, j, k: (k, i)) if mode == 'tn' else pl.BlockSpec((tm, tk), lambda i, j, k: (i, k))
    b_spec = pl.BlockSpec((tn, tk), lambda i, j, k: (j, k)) if mode == 'nt' else pl.BlockSpec((tk, tn), lambda i, j, k: (k, j))
    return pl.pallas_call(
        body, grid=(M // tm, N // tn, nk), in_specs=[a_spec, b_spec],
        out_specs=pl.BlockSpec((tm, tn), lambda i, j, k: (i, j)), out_shape=SDS((M, N), F32),
        scratch_shapes=[pltpu.VMEM((tm, tn), F32)], name=name,
        compiler_params=_cparams(("parallel", "parallel", "arbitrary")))(a, b)


def _matmul(x, w16, wz, name):
    @jax.custom_vjp
    def op(x, wz):
        return _mm(x.astype(BF16), w16, 'nn', name + "_f")

    def op_f(x, wz):
        x16 = x.astype(BF16)
        return _mm(x16, w16, 'nn', name + "_f"), x16

    def op_b(x16, g):
        g16 = g.astype(BF16)
        return _mm(g16, w16, 'nt', name + "_dx"), _mm(x16, g16, 'tn', name + "_dw")

    op.defvjp(op_f, op_b)
    return op(x, wz)


def _block_diag_pair(x, wa, wx, name):
    T = x.shape[0]
    nb, bd, _ = wa.shape
    dot = lambda a, b, mode: lax.dot_general(a.astype(BF16), b.astype(BF16), _DIMS[mode], preferred_element_type=F32)
    cols = pl.BlockSpec((T, bd), lambda n: (0, n))
    blk = pl.BlockSpec((1, bd, bd), lambda n: (n, 0, 0))

    def fwd_call(x, wa, wx):
        def body(x_ref, wa_ref, wx_ref, ra_ref, rx_ref):
            ra_ref[...] = dot(x_ref[...], wa_ref[0], 'nn')
            rx_ref[...] = dot(x_ref[...], wx_ref[0], 'nn')

        return pl.pallas_call(body, grid=(nb,), in_specs=[cols, blk, blk], out_specs=[cols, cols],
                              out_shape=[SDS(x.shape, F32)] * 2, name=name + "_f",
                              compiler_params=_cparams(("parallel",)))(x, wa, wx)

    def bwd_call(x, wa, wx, ga, gx):
        def body(x_ref, wa_ref, wx_ref, ga_ref, gx_ref, dx_ref, dwa_ref, dwx_ref):
            dx_ref[...] = dot(ga_ref[...], wa_ref[0], 'nt') + dot(gx_ref[...], wx_ref[0], 'nt')
            dwa_ref[0] = dot(x_ref[...], ga_ref[...], 'tn')
            dwx_ref[0] = dot(x_ref[...], gx_ref[...], 'tn')

        return pl.pallas_call(body, grid=(nb,), in_specs=[cols, blk, blk, cols, cols], out_specs=[cols, blk, blk],
                              out_shape=[SDS(x.shape, F32), SDS(wa.shape, F32), SDS(wx.shape, F32)], name=name + "_b",
                              compiler_params=_cparams(("parallel",)))(x, wa, wx, ga, gx)

    @jax.custom_vjp
    def op(x, wa, wx):
        return tuple(fwd_call(x, wa, wx))

    def op_f(x, wa, wx):
        return op(x, wa, wx), (x, wa, wx)

    def op_b(res, g):
        return tuple(bwd_call(*res, *g))

    op.defvjp(op_f, op_b)
    return op(x, wa, wx)


def _tile_op(name, fn, arrs, params, consts, by_rows, width=LANES):
    arrs, params, consts = tuple(arrs), tuple(params), tuple(consts)
    na, npar, nc = len(arrs), len(params), len(consts)
    T = arrs[0].shape[0]
    if by_rows:
        tile = _pick(T, (256, 128, 64, 32, 16, 8))
        grid = (T // tile,)
        arr_block = lambda a: (tile, a.shape[1])
        arr_spec = lambda a: pl.BlockSpec((tile, a.shape[1]), lambda i: (i, 0))
        par_block = lambda p: p.shape
        par_spec = lambda p: pl.BlockSpec(p.shape, lambda i: (0, 0))
    else:
        grid = (arrs[0].shape[1] // width,)
        arr_block = lambda a: (T, width)
        arr_spec = lambda a: pl.BlockSpec((T, width), lambda i: (0, i))
        par_block = lambda p: (p.shape[0], width)
        par_spec = lambda p: pl.BlockSpec((p.shape[0], width), lambda i: (0, i))
    const_spec = lambda c: pl.BlockSpec(c.shape, lambda i: (0,) * c.ndim)
    outs_sds = jax.eval_shape(fn, *[SDS(arr_block(a), F32) for a in arrs], *[SDS(par_block(p), F32) for p in params],
                              *[SDS(c.shape, c.dtype) for c in consts])
    out_widths = [o.shape[1] for o in outs_sds]
    nout = len(out_widths)
    if by_rows:
        out_shapes = [SDS((T, w), F32) for w in out_widths]
        out_specs = [pl.BlockSpec((tile, w), lambda i: (i, 0)) for w in out_widths]
    else:
        out_shapes = [SDS((T, grid[0] * w), F32) for w in out_widths]
        out_specs = [pl.BlockSpec((T, w), lambda i: (0, i)) for w in out_widths]

    def fwd_call(arrs, params):
        def body(*refs):
            outs = fn(*[r[...] for r in refs[:na + npar + nc]])
            for o_ref, o in zip(refs[na + npar + nc:], outs):
                o_ref[...] = o

        return pl.pallas_call(
            body, grid=grid, in_specs=[arr_spec(a) for a in arrs] + [par_spec(p) for p in params] + [const_spec(c) for c in consts],
            out_specs=out_specs, out_shape=out_shapes, name=name + "_f",
            compiler_params=_cparams(("parallel",)))(*arrs, *params, *consts)

    def bwd_call(arrs, params, cts):
        def body(*refs):
            ins = refs[:na + npar + nc + nout]
            outs = refs[na + npar + nc + nout:]
            av = [r[...] for r in ins[:na]]
            pv = [r[...] for r in ins[na:na + npar]]
            cv = [r[...] for r in ins[na + npar:na + npar + nc]]
            gv = [r[...] for r in ins[na + npar + nc:]]
            _, vjp = jax.vjp(lambda *t: fn(*t, *cv), *av, *pv)
            grads = vjp(tuple(gv))
            for o_ref, g in zip(outs[:na], grads[:na]):
                o_ref[...] = g
            if by_rows and npar:
                @pl.when(pl.program_id(0) == 0)
                def _():
                    for o_ref in outs[na:]:
                        o_ref[...] = jnp.zeros_like(o_ref)

                for o_ref, g in zip(outs[na:], grads[na:]):
                    o_ref[...] += g
            else:
                for o_ref, g in zip(outs[na:], grads[na:]):
                    o_ref[...] = g

        return pl.pallas_call(
            body, grid=grid,
            in_specs=[arr_spec(a) for a in arrs] + [par_spec(p) for p in params] + [const_spec(c) for c in consts] + out_specs,
            out_specs=[arr_spec(a) for a in arrs] + [par_spec(p) for p in params],
            out_shape=[SDS(a.shape, F32) for a in arrs] + [SDS(p.shape, F32) for p in params], name=name + "_b",
            compiler_params=_cparams(("arbitrary",) if by_rows else ("parallel",)))(*arrs, *params, *consts, *cts)

    @jax.custom_vjp
    def op(arrs, params):
        return tuple(fwd_call(arrs, params))

    def op_f(arrs, params):
        return op(arrs, params), (arrs, params)

    def op_b(res, cts):
        arrs, params = res
        g = bwd_call(arrs, params, cts)
        return tuple(g[:na]), tuple(g[na:])

    op.defvjp(op_f, op_b)
    return op(arrs, params)


def _rowwise(name, fn, arrs, params=(), consts=()):
    return _tile_op(name, fn, arrs, params, consts, True)


def _colwise(name, fn, arrs, params=(), width=LANES):
    return _tile_op(name, fn, arrs, params, (), False, width)


@functools.partial(jax.custom_vjp, nondiff_argnums=(1,))
def _shift(x, k):
    rows = lax.broadcasted_iota(jnp.int32, x.shape, 0)
    return jnp.where(rows >= k, pltpu.roll(x, k, 0), 0.0)


def _shift_f(x, k):
    return _shift(x, k), None


def _shift_b(k, _, g):
    n = g.shape[0]
    rows = lax.broadcasted_iota(jnp.int32, g.shape, 0)
    return (jnp.where(rows < n - k, pltpu.roll(g, n - k, 0), 0.0),)


_shift.defvjp(_shift_f, _shift_b)


def _causal_conv(x, w):
    K = w.shape[0]
    y = x * w[K - 1:K, :]
    for j in range(K - 1):
        y = y + _shift(x, K - 1 - j) * w[j:j + 1, :]
    return y


def _silu(x):
    return x * jax.nn.sigmoid(x)


def _softplus(x):
    return jnp.maximum(x, 0.0) + jnp.log1p(jnp.exp(-jnp.abs(x)))


def _split_cols(h, offs, widths):
    @jax.custom_vjp
    def op(h):
        return tuple(h[:, o:o + w] for o, w in zip(offs, widths))

    def op_f(h):
        return op(h), None

    def op_b(_, cts):
        parts, pos = [], 0
        T = cts[0].shape[0]
        for o, w, c in zip(offs, widths, cts):
            if o > pos:
                parts.append(jnp.zeros((T, o - pos), F32))
            parts.append(c)
            pos = o + w
        if pos < h.shape[1]:
            parts.append(jnp.zeros((T, h.shape[1] - pos), F32))
        return (jnp.concatenate(parts, axis=1),)

    op.defvjp(op_f, op_b)
    return op(h)


def _group_ones(width, group):
    g = jnp.arange(width) // group
    return (g[:, None] == g[None, :]).astype(F32)


def _layer_norm_rows(x, g, b, eps):
    mu = jnp.mean(x, axis=1, keepdims=True)
    var = jnp.mean(jnp.square(x - mu), axis=1, keepdims=True)
    return (x - mu) * lax.rsqrt(var + eps) * g + b


def _tri(L, strict=False):
    i = lax.broadcasted_iota(jnp.int32, (L, L), 0)
    j = lax.broadcasted_iota(jnp.int32, (L, L), 1)
    return (i > j) if strict else (i >= j)


def _cumsum_rows(x):
    H, L, _ = x.shape
    tri = jnp.broadcast_to(_tri(L).astype(F32)[None], (H, L, L))
    return jnp.einsum('hls,hsn->hln', tri, x, precision=HI)


def _col_to_row(c):
    L = c.shape[1]
    return jnp.sum(c * _tri_eye(L)[None], axis=1, keepdims=True)


def _row_to_col(r):
    N = r.shape[2]
    return jnp.sum(r * _tri_eye(N)[None], axis=2, keepdims=True)


def _scalar_col(t):
    return _row_to_col(t.reshape(t.shape[0], 1, t.shape[3]))


def _tri_eye(L):
    i = lax.broadcasted_iota(jnp.int32, (L, L), 0)
    j = lax.broadcasted_iota(jnp.int32, (L, L), 1)
    return (i == j).astype(F32)


def _unit_lower_inverse(n_strict):
    L = n_strict.shape[1]
    inv = _tri_eye(L)[None] + n_strict
    x = n_strict
    p = 2
    while p < L:
        x = jnp.einsum('hij,hjk->hik', x, x)
        inv = inv + jnp.einsum('hij,hjk->hik', inv, x)
        p *= 2
    return inv


def _rwkv_chunk(r, lw, k, v, a, b, h0):
    L = r.shape[1]
    mm = jnp.einsum
    cum = _cumsum_rows(lw)
    cum_l = jnp.sum(lw, axis=1, keepdims=True)
    e_neg = jnp.exp(-cum)
    rt, bt, kt, at = r * jnp.exp(cum), b * e_neg, k * e_neg, a * jnp.exp(cum - lw)
    to_end = jnp.exp(cum_l - cum)
    strict, incl = _tri(L, True)[None], _tri(L)[None]
    n = jnp.where(strict, mm('hld,hsd->hls', at, bt), 0.0)
    mk = jnp.where(strict, mm('hld,hsd->hls', at, kt), 0.0)
    u = mm('hls,hsv->hlv', _unit_lower_inverse(n), mm('hld,hdv->hlv', at, h0) + mm('hls,hsv->hlv', mk, v))
    y = (mm('hld,hdv->hlv', rt, h0) + mm('hls,hsv->hlv', jnp.where(incl, mm('hld,hsd->hls', rt, bt), 0.0), u)
         + mm('hls,hsv->hlv', jnp.where(incl, mm('hld,hsd->hls', rt, kt), 0.0), v))
    h1 = (_row_to_col(jnp.exp(cum_l)) * h0 + mm('hld,hlv->hdv', b * to_end, u) + mm('hld,hlv->hdv', k * to_end, v))
    return y, h1


def _gdn_chunk(q, k, v, beta, lg, h0):
    C, D = q.shape[1], q.shape[2]
    scale = D ** -0.5
    beta, lg = _scalar_col(beta), _scalar_col(lg)
    gc = _cumsum_rows(lg)
    gc_l = jnp.sum(lg, axis=1, keepdims=True)
    causal, strict = _tri(C)[None], _tri(C, True)[None]
    decay = jnp.exp(jnp.where(causal, gc - _col_to_row(gc), -jnp.inf))
    k_beta = k * beta
    m = jnp.where(strict, jnp.einsum('hcd,hsd->hcs', k_beta, k) * decay, 0.0)
    inv = _unit_lower_inverse(-m)
    e_gc = jnp.exp(gc)
    u = jnp.einsum('hcs,hsd->hcd', inv, v * beta)
    w = jnp.einsum('hcs,hsd->hcd', inv, k_beta * e_gc)
    attn = jnp.where(causal, jnp.einsum('hcd,hsd->hcs', q * scale, k) * decay, 0.0)
    v_new = u - jnp.einsum('hcd,hde->hce', w, h0)
    o = jnp.einsum('hcd,hde->hce', q * scale * e_gc, h0) + jnp.einsum('hcs,hse->hce', attn, v_new)
    h1 = h0 * jnp.exp(gc_l) + jnp.einsum('hcd,hce->hde', k * jnp.exp(gc_l - gc), v_new)
    return o, h1


def _ssd_chunk(xs, dt, aa, bm, cm, h0):
    H, L, _ = xs.shape
    G = bm.shape[0]
    per_head = lambda t: jnp.broadcast_to(t[:, None], (G, H // G) + t.shape[1:]).reshape((H,) + t.shape[1:])
    dt, aa = _scalar_col(dt), _scalar_col(aa)
    x = xs * dt
    cs = _cumsum_rows(aa)
    cs_l = jnp.sum(aa, axis=1, keepdims=True)
    causal = _tri(L)[None]
    cb = per_head(jnp.einsum('gln,gsn->gls', cm, bm))
    wd = jnp.where(causal, cb * jnp.exp(jnp.where(causal, cs - _col_to_row(cs), -jnp.inf)), 0.0)
    cmb, bmb = per_head(cm), per_head(bm)
    y = jnp.einsum('hls,hsp->hlp', wd, x) + jnp.einsum('hln,hnp->hlp', cmb, h0) * jnp.exp(cs)
    h1 = jnp.exp(cs_l) * h0 + jnp.einsum('hln,hlp->hnp', bmb, x * jnp.exp(cs_l - cs))
    return y, h1


def _chunk_scan(name, fn, seqs, hb, L, state_shape, out_width):
    seqs = tuple(seqs)
    ns = len(seqs)
    H = max(s.shape[0] for s in seqs)
    T = max(s.shape[1] for s in seqs)
    nc, nh = T // L, H // hb
    lead = [hb * s.shape[0] // H for s in seqs]
    st_block = (hb,) + state_shape

    def seq_spec(s, l, imap):
        if s.ndim == 4:
            return pl.BlockSpec((l, 1, 1, L), lambda h, c: imap(h, c) + (0,))
        return pl.BlockSpec((l, L, s.shape[2]), imap)

    fmap = lambda h, c: (h, c, 0)
    rmap = lambda h, c: (h, nc - 1 - c, 0)

    def fwd_call(seqs):
        def body(*refs):
            y_ref, st_ref, carry = refs[ns], refs[ns + 1], refs[ns + 2]

            @pl.when(pl.program_id(1) == 0)
            def _():
                carry[...] = jnp.zeros_like(carry)

            h0 = carry[...]
            st_ref[0] = h0
            y, h1 = fn(*[r[...] for r in refs[:ns]], h0)
            y_ref[...] = y
            carry[...] = h1

        return pl.pallas_call(
            body, grid=(nh, nc), in_specs=[seq_spec(s, l, fmap) for s, l in zip(seqs, lead)],
            out_specs=[pl.BlockSpec((hb, L, out_width), fmap),
                       pl.BlockSpec((1,) + st_block, lambda h, c: (c, h) + (0,) * len(state_shape))],
            out_shape=[SDS((H, T, out_width), F32), SDS((nc, H) + state_shape, F32)],
            scratch_shapes=[pltpu.VMEM(st_block, F32)], name=name + "_f",
            compiler_params=_cparams(("parallel", "arbitrary")))(*seqs)

    def bwd_call(seqs, states, dy):
        def body(*refs):
            st_ref, dy_ref = refs[ns], refs[ns + 1]
            outs, carry = refs[ns + 2:2 * ns + 2], refs[2 * ns + 2]

            @pl.when(pl.program_id(1) == 0)
            def _():
                carry[...] = jnp.zeros_like(carry)

            _, vjp = jax.vjp(fn, *[r[...] for r in refs[:ns]], st_ref[0])
            grads = vjp((dy_ref[...], carry[...]))
            for o_ref, g in zip(outs, grads[:ns]):
                o_ref[...] = g
            carry[...] = grads[ns]

        return pl.pallas_call(
            body, grid=(nh, nc),
            in_specs=[seq_spec(s, l, rmap) for s, l in zip(seqs, lead)]
            + [pl.BlockSpec((1,) + st_block, lambda h, c: (nc - 1 - c, h) + (0,) * len(state_shape)),
               pl.BlockSpec((hb, L, out_width), rmap)],
            out_specs=[seq_spec(s, l, rmap) for s, l in zip(seqs, lead)],
            out_shape=[SDS(s.shape, F32) for s in seqs],
            scratch_shapes=[pltpu.VMEM(st_block, F32)], name=name + "_b",
            compiler_params=_cparams(("parallel", "arbitrary")))(*seqs, states, dy)

    @jax.custom_vjp
    def op(seqs):
        return fwd_call(seqs)[0]

    def op_f(seqs):
        y, states = fwd_call(seqs)
        return y, (seqs, states)

    def op_b(res, dy):
        seqs, states = res
        return (tuple(bwd_call(seqs, states, dy)),)

    op.defvjp(op_f, op_b)
    return op(seqs)


def _lru_scan_call(a, u, h, reverse, name):
    T, C = a.shape
    cw = _pick(C, (1024, 512, 256, 128))
    tt = _pick(T, (512, 256, 128, 64, 32, 16, 8))
    nt, ng = T // tt, tt // 8
    sub = lambda: lax.broadcasted_iota(jnp.int32, (8, cw), 0)
    first = lambda: pl.program_id(1) == 0

    def fwd_body(a_ref, u_ref, h_ref, carry_ref):
        @pl.when(first())
        def _():
            carry_ref[...] = jnp.zeros_like(carry_ref)

        def group(i, carry):
            r0 = pl.multiple_of(i * 8, 8)
            ab, ub = a_ref[pl.ds(r0, 8), :], u_ref[pl.ds(r0, 8), :]
            out = jnp.zeros((8, cw), F32)
            for j in range(8):
                carry = ab[j:j + 1, :] * carry + ub[j:j + 1, :]
                out = jnp.where(sub() == j, carry, out)
            h_ref[pl.ds(r0, 8), :] = out
            return carry

        carry_ref[...] = lax.fori_loop(0, ng, group, carry_ref[...])

    def bwd_body(a_ref, u_ref, h_ref, hp_ref, g_ref, da_ref, cg_ref, ca_ref):
        @pl.when(first())
        def _():
            cg_ref[...] = jnp.zeros_like(cg_ref)
            ca_ref[...] = jnp.zeros_like(ca_ref)

        h_before = jnp.where(pl.program_id(1) < nt - 1, hp_ref[7:8, :], 0.0)

        def group(i, carry):
            g_next, a_next = carry
            gi = ng - 1 - i
            r0 = pl.multiple_of(gi * 8, 8)
            rp = pl.multiple_of(jnp.maximum(gi - 1, 0) * 8, 8)
            ab, ub, hb = a_ref[pl.ds(r0, 8), :], u_ref[pl.ds(r0, 8), :], h_ref[pl.ds(r0, 8), :]
            h_last_prev = jnp.where(gi > 0, h_ref[pl.ds(rp, 8), :][7:8, :], h_before)
            g_out = jnp.zeros((8, cw), F32)
            da_out = jnp.zeros((8, cw), F32)
            for j in range(7, -1, -1):
                g_next = ub[j:j + 1, :] + a_next * g_next
                a_next = ab[j:j + 1, :]
                h_prev = hb[j - 1:j, :] if j > 0 else h_last_prev
                g_out = jnp.where(sub() == j, g_next, g_out)
                da_out = jnp.where(sub() == j, g_next * h_prev, da_out)
            g_ref[pl.ds(r0, 8), :] = g_out
            da_ref[pl.ds(r0, 8), :] = da_out
            return g_next, a_next

        cg_ref[...], ca_ref[...] = lax.fori_loop(0, ng, group, (cg_ref[...], ca_ref[...]))

    row = pltpu.VMEM((1, cw), F32)
    if not reverse:
        spec = pl.BlockSpec((tt, cw), lambda i, t: (t, i))
        return pl.pallas_call(fwd_body, grid=(C // cw, nt), in_specs=[spec, spec], out_specs=spec,
                              out_shape=SDS((T, C), F32), scratch_shapes=[row], name=name,
                              compiler_params=_cparams(("parallel", "arbitrary")))(a, u)
    spec = pl.BlockSpec((tt, cw), lambda i, t: (nt - 1 - t, i))
    before = pl.BlockSpec((8, cw), lambda i, t: (jnp.maximum((nt - 1 - t) * ng - 1, 0), i))
    return pl.pallas_call(bwd_body, grid=(C // cw, nt), in_specs=[spec, spec, spec, before], out_specs=[spec, spec],
                          out_shape=[SDS((T, C), F32), SDS((T, C), F32)], scratch_shapes=[row, row], name=name,
                          compiler_params=_cparams(("parallel", "arbitrary")))(a, u, h, h)


@jax.custom_vjp
def _lru_scan(a, u):
    return _lru_scan_call(a, u, None, False, "lru_scan_f")


def _lru_scan_f(a, u):
    h = _lru_scan(a, u)
    return h, (a, h)


def _lru_scan_b(res, dh):
    a, h = res
    g, da = _lru_scan_call(a, dh, h, True, "lru_scan_b")
    return da, g


_lru_scan.defvjp(_lru_scan_f, _lru_scan_b)


def _heads_major(x, nheads):
    T, W = x.shape
    return jnp.transpose(x.reshape(T, nheads, W // nheads), (1, 0, 2))


def _tokens_major(x):
    H, T, N = x.shape
    return jnp.transpose(x, (1, 0, 2)).reshape(T, H * N)


def _pad_cols(w, offs, widths, total):
    parts, pos, src = [], 0, 0
    for o, wd in zip(offs, widths):
        if o > pos:
            parts.append(jnp.zeros((w.shape[0], o - pos), w.dtype))
        parts.append(w[:, src:src + wd])
        src += wd
        pos = o + wd
    if pos < total:
        parts.append(jnp.zeros((w.shape[0], total - pos), w.dtype))
    return jnp.concatenate(parts, axis=1)


def _unpad_cols(w, offs, widths):
    return jnp.concatenate([w[:, o:o + wd] for o, wd in zip(offs, widths)], axis=1)


def _aligned_layout(widths):
    offs, pos = [], 0
    for w in widths:
        offs.append(pos)
        pos += _rup(w, LANES)
    return offs, _rup(pos, 512)


def _pad_lanes(v, n):
    return jnp.pad(v, ((0, 0), (0, n - v.shape[1])))


def _even_mixer(x, q, wz, li):
    T = x.shape[0]
    ah, an = q['rwkv_r_k'].shape
    aw = ah * an
    bh, bn = q['gdn_A_log'].shape[0], q['gdn_norm_g'].shape[0]
    bw = bh * bn
    lw_, la_, lg_ = q['rwkv_w2'].shape[0], q['rwkv_a2'].shape[0], q['rwkv_g2'].shape[0]
    widths = [aw, aw, aw, lw_, la_, lg_, bw, bw, bw, bw, bh, bh]
    offs, total = _aligned_layout(widths)
    pw = [_rup(w, LANES) for w in widths]
    hcols = _matmul(x, _pad_cols(q['even_w_in'], offs, widths, total), wz['even_w_in'], f"even_in{li}")
    a_w = offs[6]
    a_cols, bq, bk, bv, bz, beta_raw, alpha_raw = _split_cols(hcols, [0] + offs[6:], [a_w] + pw[6:])

    mu = _pad_cols(q['rwkv_mu'][None], offs[:6], widths[:6], a_w)
    (xs,) = _colwise(f"rwkv_shift{li}", lambda h, m: (h + (_shift(h, 1) - h) * m,), [a_cols], [mu])
    r, k, v, w_lo, a_lo, g_lo = _split_cols(xs, offs[:6], pw[:6])
    tw, sg = _rowwise(f"rwkv_lora_act{li}", lambda w, g: (jnp.tanh(w), jax.nn.sigmoid(g)), [w_lo, g_lo])
    pad_rows = lambda w, n: jnp.pad(w, ((0, n - w.shape[0]), (0, 0)))
    wl = _matmul(tw, pad_rows(q['rwkv_w2'], pw[3]), wz['rwkv_w2'], f"rwkv_w2{li}")
    al = _matmul(a_lo, pad_rows(q['rwkv_a2'], pw[4]), wz['rwkv_a2'], f"rwkv_a2{li}")
    g = _matmul(sg, pad_rows(q['rwkv_g2'], pw[5]), wz['rwkv_g2'], f"rwkv_g2{li}")
    ones_a = _group_ones(aw, an)

    def pre(k, wl, al, w0, a0, k_k, k_a, ones):
        lw = -jnp.exp(-_softplus(-(w0 + wl)) - 0.5)
        a = jax.nn.sigmoid(a0 + al)
        kk = k * k_k
        kk = kk * lax.rsqrt(jnp.dot(kk * kk, ones) + L2_EPS)
        return lw, k * (1.0 + (a - 1.0) * k_a), -kk, kk * a

    lw, k2, sa, sb = _rowwise(f"rwkv_pre{li}", pre, [k, wl, al],
                              [q['rwkv_w0'][None], q['rwkv_a0'][None], q['rwkv_k_k'][None], q['rwkv_k_a'][None]], [ones_a])
    hm = lambda t: _heads_major(t, ah)
    out = _chunk_scan(f"rwkv_scan{li}", _rwkv_chunk, [hm(r), hm(lw), hm(k2), hm(v), hm(sa), hm(sb)],
                      min(ah, SCAN_HEADS), min(RWKV_CHUNK, T), (an, an), an)
    out = _tokens_major(out)

    def post(out, r, k2, v, g, gn_g, gn_b, r_k, ones):
        mean = jnp.dot(out, ones) * (1.0 / an)
        cen = out - mean
        var = jnp.dot(cen * cen, ones) * (1.0 / an)
        normed = cen * lax.rsqrt(var + A_GN_EPS) * gn_g + gn_b
        bonus = jnp.dot(r * k2 * r_k, ones) * v
        return ((normed + bonus) * g,)

    flat = lambda t: t.reshape(1, -1)
    (ya,) = _rowwise(f"rwkv_post{li}", post, [out, r, k2, v, g],
                     [flat(q['rwkv_gn_g']), flat(q['rwkv_gn_b']), flat(q['rwkv_r_k'])], [ones_a])

    cw = q['gdn_conv_w']

    def conv_l2(x, w):
        y = _silu(_causal_conv(x, w))
        return (y * lax.rsqrt(jnp.sum(y * y, axis=1, keepdims=True) + L2_EPS),)

    if bn == LANES:
        (gq,) = _colwise(f"gdn_conv_q{li}", conv_l2, [bq], [cw[:, :bw]])
        (gk,) = _colwise(f"gdn_conv_k{li}", conv_l2, [bk], [cw[:, bw:2 * bw]])
    else:
        raise NotImplementedError("gated DeltaNet head width must equal the lane count")
    (gv,) = _colwise(f"gdn_conv_v{li}", lambda x, w: (_silu(_causal_conv(x, w)),), [bv], [cw[:, 2 * bw:]])

    def gates(beta_raw, alpha_raw, a_log, dt_bias):
        return jax.nn.sigmoid(beta_raw), -jnp.exp(a_log) * _softplus(alpha_raw + dt_bias)

    beta, lg = _rowwise(f"gdn_gates{li}", gates, [beta_raw, alpha_raw],
                        [_pad_lanes(q['gdn_A_log'][None], pw[10]), _pad_lanes(q['gdn_dt_bias'][None], pw[11])])
    gl = min(GDN_CHUNK, T)
    col = lambda t: jnp.transpose(t[:, :bh]).reshape(bh, T // gl, 1, gl)
    hmb = lambda t: _heads_major(t, bh)
    o = _chunk_scan(f"gdn_scan{li}", _gdn_chunk, [hmb(gq), hmb(gk), hmb(gv), col(beta), col(lg)],
                    min(bh, SCAN_HEADS), gl, (bn, bn), bn)
    o = _tokens_major(o)
    ones_b = _group_ones(bw, bn)

    def gdn_post(o, z, ng, ones):
        ms = jnp.dot(o * o, ones) * (1.0 / bn)
        return (o * lax.rsqrt(ms + RMS_EPS) * ng * _silu(z),)

    (yb,) = _rowwise(f"gdn_post{li}", gdn_post, [o, bz], [jnp.tile(q['gdn_norm_g'][None], (1, bh))], [ones_b])
    return _matmul(jnp.concatenate([ya, yb], axis=1), q['even_w_out'], wz['even_w_out'], f"even_out{li}")


def _odd_mixer(x, q, wz, li):
    T = x.shape[0]
    ch = q['mamba_dt_bias'].shape[0]
    cwid = q['mamba_norm_g'].shape[0]
    cp = cwid // ch
    xbc_w = q['mamba_conv_w'].shape[1]
    cn = (xbc_w - cwid) // (2 * C_GROUPS)
    dw = q['lru_lambda'].shape[0]
    widths = [cwid, xbc_w, ch, dw, dw]
    offs, total = _aligned_layout(widths)
    pw = [_rup(w, LANES) for w in widths]
    hcols = _matmul(x, _pad_cols(q['odd_w_in'], offs, widths, total), wz['odd_w_in'], f"odd_in{li}")
    z, xbc, dt_raw, y_br, x_br = _split_cols(hcols, offs, pw)

    (xbc_c,) = _colwise(f"mamba_conv{li}", lambda x, w, b: (_silu(_causal_conv(x, w) + b),), [xbc],
                        [q['mamba_conv_w'], q['mamba_conv_b'][None]])
    gn = C_GROUPS * cn
    xs, bm, cm = _split_cols(xbc_c, [0, cwid, cwid + gn], [cwid, gn, gn])

    def dts(dt_raw, dt_bias, a_log):
        dt = _softplus(dt_raw + dt_bias)
        return dt, dt * (-jnp.exp(a_log))

    dt, aa = _rowwise(f"mamba_dt{li}", dts, [dt_raw],
                      [_pad_lanes(q['mamba_dt_bias'][None], pw[2]), _pad_lanes(q['mamba_A_log'][None], pw[2])])
    sl = min(SSD_CHUNK, T)
    col = lambda t: jnp.transpose(t[:, :ch]).reshape(ch, T // sl, 1, sl)
    y = _chunk_scan(f"ssd_scan{li}", _ssd_chunk,
                    [_heads_major(xs, ch), col(dt), col(aa), _heads_major(bm, C_GROUPS), _heads_major(cm, C_GROUPS)],
                    max(ch // C_GROUPS, min(ch, SCAN_HEADS)), sl, (cn, cp), cp)
    y = _tokens_major(y)
    gsz = cwid // C_GROUPS

    def mamba_post(y, xs, z, d, ng):
        yy = (y + xs * d) * _silu(z)
        lane = lax.broadcasted_iota(jnp.int32, yy.shape, 1)
        ms = jnp.zeros_like(yy)
        for gi in range(C_GROUPS):
            sel = (lane >= gi * gsz) & (lane < (gi + 1) * gsz)
            ms = jnp.where(sel, jnp.sum(jnp.where(sel, yy * yy, 0.0), axis=1, keepdims=True) * (1.0 / gsz), ms)
        return (yy * lax.rsqrt(ms + RMS_EPS) * ng,)

    (yc,) = _rowwise(f"mamba_post{li}", mamba_post, [y, xs, z],
                     [jnp.repeat(q['mamba_D'], cp)[None], q['mamba_norm_g'][None]])

    (xc,) = _colwise(f"lru_conv{li}", lambda x, w, b: (_causal_conv(x, w) + b,), [x_br],
                     [q['lru_conv_w'], q['lru_conv_b'][None]])
    ra, ia = _block_diag_pair(xc, q['lru_wa'], q['lru_wx'], f"lru_gates{li}")

    def lru_pre(ra, ia, xc, ba, bx, lam):
        r = jax.nn.sigmoid(ra + ba)
        i = jax.nn.sigmoid(ia + bx)
        log_a = LRU_C * r * (-_softplus(-lam))
        t = 2.0 * log_a
        series = t * (1.0 + t * (0.5 + t * (1.0 / 6.0 + t * (1.0 / 24.0 + t * (1.0 / 120.0 + t * (1.0 / 720.0))))))
        expm1 = jnp.where(t > -0.2, series, jnp.exp(t) - 1.0)
        return jnp.exp(log_a), jnp.sqrt(-expm1) * (i * xc)

    a, u = _rowwise(f"lru_pre{li}", lru_pre, [ra, ia, xc], [q['lru_ba'][None], q['lru_bx'][None], q['lru_lambda'][None]])
    h = _lru_scan(a, u)
    (yd,) = _rowwise(f"lru_post{li}", lambda h, y: (h * jax.nn.gelu(y),), [h, y_br])
    return _matmul(jnp.concatenate([yc, yd], axis=1), q['odd_w_out'], wz['odd_w_out'], f"odd_out{li}")


def _forward(x, wz, sp, p, w16, depth):
    alpha = (2.0 * depth) ** 0.25
    for i in range(depth):
        j = i // 2
        even = i % 2 == 0
        names = [n for n in WEIGHTS if n.startswith(('rwkv_', 'gdn_', 'even_') if even else ('mamba_', 'lru_', 'odd_'))]
        q = {n: (w16[n][j] if n in MATRICES else sp[n][j]) for n in names}
        wzl = {n: wz[f"{n}.{j}"] for n in names if n in MATRICES}
        y = (_even_mixer if even else _odd_mixer)(x, q, wzl, i)

        def ln_res(x, y, g, b):
            return (_layer_norm_rows(alpha * x + y, g, b, LN_EPS),)

        (h,) = _rowwise(f"ln1_{i}", ln_res, [x, y], [sp['ln1_g'][i][None], sp['ln1_b'][i][None]])
        dff = w16['ffn_up'].shape[2] // 2
        gate = _matmul(h, w16['ffn_up'][i][:, :dff], wz[f"ffn_up.{i}"], f"ffn_gate{i}")
        val = _matmul(h, w16['ffn_up'][i][:, dff:], wz[f"ffn_up_val.{i}"], f"ffn_val{i}")
        cw, cb = sp['ffn_conv_w'][i], sp['ffn_conv_b'][i][None]

        def ffn_act(gate, val, wg, wv, bg, bv):
            return (_silu(_causal_conv(gate, wg) + bg) * (_causal_conv(val, wv) + bv),)

        (act,) = _colwise(f"ffn_act{i}", ffn_act, [gate, val], [cw[:, :dff], cw[:, dff:], cb[:, :dff], cb[:, dff:]])
        f = _matmul(act, w16['ffn_down'][i], wz[f"ffn_down.{i}"], f"ffn_down{i}")
        (h2,) = _rowwise(f"ln2_{i}", ln_res, [h, f], [sp['ln2_g'][i][None], sp['ln2_b'][i][None]])
        e0 = _matmul(p[i], w16['ple_proj'][i], wz[f"ple_proj.{i}"], f"ple_proj{i}")
        gl = _matmul(h2, w16['ple_gate_w'][i], wz[f"ple_gate_w.{i}"], f"ple_gate{i}")

        def ple(h2, gl, e0, gb, ng):
            e = e0 * lax.rsqrt(jnp.mean(e0 * e0, axis=1, keepdims=True) + RMS_EPS) * ng
            return (h2 + jax.nn.sigmoid(gl + gb) * e,)

        (x,) = _rowwise(f"ple{i}", ple, [h2, gl, e0], [sp['ple_gate_b'][i][None], sp['ple_norm_g'][i][None]])
    return x


def _loss_head(y, target):
    T, D = y.shape
    tile = _pick(T, (256, 128, 64, 32, 16, 8))

    def body(y_ref, t_ref, dy_ref, l_ref):
        err = y_ref[...] - t_ref[...]
        dy_ref[...] = err * (1.0 / D)

        @pl.when(pl.program_id(0) == 0)
        def _():
            l_ref[...] = jnp.zeros_like(l_ref)

        l_ref[...] += jnp.sum(jnp.sum(err * err, axis=1, keepdims=True), axis=0, keepdims=True) * (0.5 / D) + jnp.zeros_like(l_ref)

    spec = pl.BlockSpec((tile, D), lambda i: (i, 0))
    dy, l = pl.pallas_call(body, grid=(T // tile,), in_specs=[spec, spec],
                           out_specs=[spec, pl.BlockSpec((8, LANES), lambda i: (0, 0))],
                           out_shape=[SDS((T, D), F32), SDS((8, LANES), F32)], name="loss_head",
                           compiler_params=_cparams(("arbitrary",)))(y, target)
    return l[0, 0], dy


def _my_index():
    return 4 * lax.axis_index("x") + 2 * lax.axis_index("y") + lax.axis_index("c")


def _hbm_specs(n):
    return [pl.BlockSpec(memory_space=pl.ANY)] * n


def _all_gather(blocks, name):
    blocks = tuple(blocks)
    n = len(blocks)
    half = [b.shape[0] // 2 for b in blocks]

    def body(*refs):
        ins, outs = refs[:n], refs[n:2 * n]
        send_sems, recv_sems, local_sems = refs[2 * n:]
        x, y, c = lax.axis_index("x"), lax.axis_index("y"), lax.axis_index("c")
        me, sibling, other = (x, y, c), (x, y, 1 - c), 1 - c
        xn, yn, dg = (1 - x, y), (x, 1 - y), (1 - x, 1 - y)

        def slot(i, px, py, pc, h=None):
            ref = outs[i].at[4 * px + 2 * py + pc]
            return ref if h is None else ref.at[pl.ds(h * half[i], half[i])]

        def copy(i, k, blk, to, h=None, src=None):
            dst = slot(i, *blk, h)
            return pltpu.make_async_remote_copy(
                src_ref=dst if src is None else src, dst_ref=dst, send_sem=send_sems.at[9 * i + k],
                recv_sem=recv_sems.at[9 * i + k], device_id=to, device_id_type=MESH)

        mine = [pltpu.make_async_copy(ins[i], slot(i, *me), local_sems.at[i]) for i in range(n)]
        sent = []
        for i in range(n):
            sent += [copy(i, 1, me, (*xn, c), src=ins[i]), copy(i, 2, me, (*yn, c), src=ins[i])]
        sent += [copy(i, 0, me, sibling, src=ins[i]) for i in range(n)]
        for cp in mine + sent:
            cp.start()

        def after(i, k_in, blk, h_in, forwards):
            copy(i, k_in, blk, me, h_in).wait_recv()
            for k_out, to, h_out in forwards:
                sent.append(copy(i, k_out, blk, to, h_out))
                sent[-1].start()

        for i in range(n):
            after(i, 1, (*xn, c), None, [(3, (*yn, c), 0), (5, sibling, None)])
        for i in range(n):
            after(i, 2, (*yn, c), None, [(4, (*xn, c), 1), (6, sibling, None)])
        for i in range(n):
            after(i, 3, (*dg, c), 0, [(7, sibling, 0)])
        for i in range(n):
            after(i, 4, (*dg, c), 1, [(8, sibling, 1)])
        for i in range(n):
            copy(i, 0, sibling, me).wait_recv()
            copy(i, 5, (*xn, other), me).wait_recv()
            copy(i, 6, (*yn, other), me).wait_recv()
            copy(i, 7, (*dg, other), me, 0).wait_recv()
            copy(i, 8, (*dg, other), me, 1).wait_recv()
        for cp in sent:
            cp.wait_send()
        for cp in mine:
            cp.wait()

    return pl.pallas_call(
        body, out_shape=[SDS((N_DEV,) + b.shape, b.dtype) for b in blocks], in_specs=_hbm_specs(n), out_specs=_hbm_specs(n),
        scratch_shapes=[pltpu.SemaphoreType.DMA((9 * n,)), pltpu.SemaphoreType.DMA((9 * n,)), pltpu.SemaphoreType.DMA((n,))],
        name=name)(*blocks)


def _sibling_exchange(parts, name):
    parts = tuple(parts)
    n = len(parts)

    def body(*refs):
        ins, outs = refs[:n], refs[n:2 * n]
        send_sems, recv_sems = refs[2 * n:]
        x, y, c = lax.axis_index("x"), lax.axis_index("y"), lax.axis_index("c")
        copies = [pltpu.make_async_remote_copy(
            src_ref=ins[i].at[q, 1 - c], dst_ref=outs[i].at[q], send_sem=send_sems.at[4 * i + q],
            recv_sem=recv_sems.at[4 * i + q], device_id=(x, y, 1 - c), device_id_type=MESH)
            for i in range(n) for q in range(4)]
        for cp in copies:
            cp.start()
        for cp in copies:
            cp.wait_recv()
        for cp in copies:
            cp.wait_send()

    return pl.pallas_call(
        body, out_shape=[SDS((4,) + p.shape[2:], p.dtype) for p in parts], in_specs=_hbm_specs(n), out_specs=_hbm_specs(n),
        scratch_shapes=[pltpu.SemaphoreType.DMA((4 * n,)), pltpu.SemaphoreType.DMA((4 * n,))], name=name)(*parts)


def _neighbour_exchange(bufs, k, name):
    bufs = tuple(bufs)
    n = len(bufs)

    def body(*refs):
        ins, outs = refs[:n], refs[n:2 * n]
        send_sems, recv_sems = refs[2 * n:]
        x, y, c = lax.axis_index("x"), lax.axis_index("y"), lax.axis_index("c")
        copies = [pltpu.make_async_remote_copy(
            src_ref=ins[i].at[pl.ds(d * k, k)], dst_ref=outs[i].at[pl.ds(d * k, k)], send_sem=send_sems.at[2 * i + d],
            recv_sem=recv_sems.at[2 * i + d], device_id=to, device_id_type=MESH)
            for d, to in enumerate([(1 - x, y, c), (x, 1 - y, c)]) for i in range(n)]
        for cp in copies:
            cp.start()
        for cp in copies:
            cp.wait_recv()
        for cp in copies:
            cp.wait_send()

    return pl.pallas_call(
        body, out_shape=[SDS((2 * k,) + b.shape[1:], b.dtype) for b in bufs], in_specs=_hbm_specs(n), out_specs=_hbm_specs(n),
        scratch_shapes=[pltpu.SemaphoreType.DMA((2 * n,)), pltpu.SemaphoreType.DMA((2 * n,))], name=name)(*bufs)


_FIRST_HOP_SLOT = ((0, 5), (4, 2), (1, 3))


def _place_ids():
    x, y, c = lax.axis_index("x"), lax.axis_index("y"), lax.axis_index("c")
    place = [c, 2 * x + y, 2 * (1 - x) + y, 2 * x + (1 - y), 2 * (1 - x) + (1 - y)]
    return jnp.stack(place + [jnp.int32(s) for pair in _FIRST_HOP_SLOT for s in pair]).astype(jnp.int32)


def _row_tile(rows, cols):
    best = None
    for t in range(16, rows + 1, 16):
        if rows % t == 0 and t * cols <= 256 * 1024:
            best = t
    return best or rows


def _chip_partials(part, recv_a, ids, dtype, name):
    _, _, R, C = part.shape
    tr = _row_tile(R // 2, C)
    nt = R // 2 // tr

    def body(ids_ref, p_ref, a_ref, o_ref):
        o_ref[...] = (p_ref[...] + a_ref[...]).astype(dtype)

    return pl.pallas_call(
        body, out_shape=SDS((6, R // 2, C), dtype),
        grid_spec=pltpu.PrefetchScalarGridSpec(
            num_scalar_prefetch=1, grid=(3, 2, nt),
            in_specs=[pl.BlockSpec((None, None, tr, C), lambda s, h, i, ids: (ids[2 + s], ids[0], h * nt + i, 0)),
                      pl.BlockSpec((None, tr, C), lambda s, h, i, ids: (ids[2 + s], h * nt + i, 0))],
            out_specs=pl.BlockSpec((None, tr, C), lambda s, h, i, ids: (ids[5 + 2 * s + h], i, 0))),
        name=name, compiler_params=_cparams(("parallel", "parallel", "parallel")))(ids, part, recv_a)


def _second_hop(first, recv1, name):
    _, R2, C = first.shape
    tr = _row_tile(R2, C)

    def body(f_ref, r_ref, o_ref):
        o_ref[...] = (f_ref[...].astype(F32) + r_ref[...].astype(F32)).astype(o_ref.dtype)

    return pl.pallas_call(
        body, out_shape=SDS((2, R2, C), first.dtype), grid=(2, R2 // tr),
        in_specs=[pl.BlockSpec((None, tr, C), lambda d, i: (5 - d, i, 0)),
                  pl.BlockSpec((None, tr, C), lambda d, i: (3 - 2 * d, i, 0))],
        out_specs=pl.BlockSpec((None, tr, C), lambda d, i: (d, i, 0)),
        name=name, compiler_params=_cparams(("parallel", "parallel")))(first, recv1)


def _adamw(part, recv_a, recv1, recv2, ids, w, m, v, name):
    R, C = w.shape
    tr = _row_tile(R // 2, C)
    nt = R // 2 // tr
    c1 = 1.0 / (1.0 - ADAM_B1 ** ADAM_STEP)
    c2 = 1.0 / (1.0 - ADAM_B2 ** ADAM_STEP)

    def body(ids_ref, p_ref, a_ref, x1_ref, y1_ref, x2_ref, y2_ref, w_ref, m_ref, v_ref, g_ref, d_ref, nm_ref, nv_ref):
        first = pl.program_id(0) == 0
        from_x = jnp.where(first, x1_ref[...], x2_ref[...]).astype(F32)
        from_y = jnp.where(first, y2_ref[...], y1_ref[...]).astype(F32)
        g = p_ref[...] + a_ref[...] + from_x + from_y
        nm = ADAM_B1 * m_ref[...] + (1.0 - ADAM_B1) * g
        nv = ADAM_B2 * v_ref[...] + (1.0 - ADAM_B2) * jnp.square(g)
        g_ref[...] = g
        nm_ref[...] = nm
        nv_ref[...] = nv
        d_ref[...] = -ADAM_LR * ((nm * c1) / (jnp.sqrt(nv * c2) + ADAM_EPS) + ADAM_WD * w_ref[...])

    spec = pl.BlockSpec((tr, C), lambda h, i, ids: (h * nt + i, 0))
    half = lambda slot: pl.BlockSpec((None, tr, C), lambda h, i, ids: (slot, i, 0))
    return pl.pallas_call(
        body, out_shape=[SDS((R, C), F32)] * 4,
        grid_spec=pltpu.PrefetchScalarGridSpec(
            num_scalar_prefetch=1, grid=(2, nt),
            in_specs=[pl.BlockSpec((None, None, tr, C), lambda h, i, ids: (ids[1], ids[0], h * nt + i, 0)),
                      pl.BlockSpec((None, tr, C), lambda h, i, ids: (ids[1], h * nt + i, 0)),
                      half(0), half(2), half(0), half(1), spec, spec, spec],
            out_specs=[spec] * 4),
        name=name, compiler_params=_cparams(("parallel", "parallel")))(ids, part, recv_a, recv1, recv1, recv2, recv2, w, m, v)


def _to_flat(vecs, quantum):
    flat = jnp.concatenate([v.reshape(-1) for v in vecs])
    n = _rup(flat.shape[0], quantum * FLAT_COLS)
    return jnp.pad(flat, (0, n - flat.shape[0])).reshape(n // FLAT_COLS, FLAT_COLS)


def _gathered_to_full(g, names, blocks):
    flat = g.reshape(N_DEV, -1)
    out, pos = {}, 0
    for n in names:
        shp = blocks[n]
        size = math.prod(shp)
        out[n] = _blocks_to_full(flat[:, pos:pos + size].reshape((N_DEV,) + shp), SHARD_AXIS[n])
        pos += size
    return out


def _blocks_to_full(g, ax):
    shp = g.shape[1:]
    return jnp.moveaxis(g, 0, ax).reshape(shp[:ax] + (N_DEV * shp[ax],) + shp[ax + 1:])


def _full_to_blocks(g, ax, ndev=N_DEV):
    shp = g.shape
    t = g.reshape(shp[:ax] + (ndev, shp[ax] // ndev) + shp[ax + 1:])
    return jnp.moveaxis(t, ax, 0)


def kernel(x, p, ln1_g, ln1_b, ln2_g, ln2_b, ffn_up, ffn_conv_w, ffn_conv_b, ffn_down, ple_proj, ple_norm_g, ple_gate_w, ple_gate_b, even_w_in, even_w_out, rwkv_mu, rwkv_w0, rwkv_w2, rwkv_a0, rwkv_a2, rwkv_g2, rwkv_k_k, rwkv_k_a, rwkv_r_k, rwkv_gn_g, rwkv_gn_b, gdn_conv_w, gdn_A_log, gdn_dt_bias, gdn_norm_g, odd_w_in, odd_w_out, mamba_conv_w, mamba_conv_b, mamba_dt_bias, mamba_A_log, mamba_D, mamba_norm_g, lru_conv_w, lru_conv_b, lru_wa, lru_ba, lru_wx, lru_bx, lru_lambda, loss_target, m_ln1_g, m_ln1_b, m_ln2_g, m_ln2_b, m_ffn_up, m_ffn_conv_w, m_ffn_conv_b, m_ffn_down, m_ple_proj, m_ple_norm_g, m_ple_gate_w, m_ple_gate_b, m_even_w_in, m_even_w_out, m_rwkv_mu, m_rwkv_w0, m_rwkv_w2, m_rwkv_a0, m_rwkv_a2, m_rwkv_g2, m_rwkv_k_k, m_rwkv_k_a, m_rwkv_r_k, m_rwkv_gn_g, m_rwkv_gn_b, m_gdn_conv_w, m_gdn_A_log, m_gdn_dt_bias, m_gdn_norm_g, m_odd_w_in, m_odd_w_out, m_mamba_conv_w, m_mamba_conv_b, m_mamba_dt_bias, m_mamba_A_log, m_mamba_D, m_mamba_norm_g, m_lru_conv_w, m_lru_conv_b, m_lru_wa, m_lru_ba, m_lru_wx, m_lru_bx, m_lru_lambda, v_ln1_g, v_ln1_b, v_ln2_g, v_ln2_b, v_ffn_up, v_ffn_conv_w, v_ffn_conv_b, v_ffn_down, v_ple_proj, v_ple_norm_g, v_ple_gate_w, v_ple_gate_b, v_even_w_in, v_even_w_out, v_rwkv_mu, v_rwkv_w0, v_rwkv_w2, v_rwkv_a0, v_rwkv_a2, v_rwkv_g2, v_rwkv_k_k, v_rwkv_k_a, v_rwkv_r_k, v_rwkv_gn_g, v_rwkv_gn_b, v_gdn_conv_w, v_gdn_A_log, v_gdn_dt_bias, v_gdn_norm_g, v_odd_w_in, v_odd_w_out, v_mamba_conv_w, v_mamba_conv_b, v_mamba_dt_bias, v_mamba_A_log, v_mamba_D, v_mamba_norm_g, v_lru_conv_w, v_lru_conv_b, v_lru_wa, v_lru_ba, v_lru_wx, v_lru_bx, v_lru_lambda):
    args = locals()
    w = {n: args[n] for n in WEIGHTS}
    m = {n: args["m_" + n] for n in WEIGHTS}
    v = {n: args["v_" + n] for n in WEIGHTS}
    depth = ln1_g.shape[0]
    me = _my_index()
    blocks = {n: w[n].shape for n in WEIGHTS}

    as_rows = lambda t: t.reshape(-1, t.shape[-1])
    small = _to_flat([w[n] for n in SMALL_SHARDED], 16)
    gathered = _all_gather([as_rows(w[n].astype(BF16)) for n in MATRICES] + [small], "gather_params")
    w16 = {n: _blocks_to_full(g.reshape((N_DEV,) + blocks[n]), SHARD_AXIS[n]) for n, g in zip(MATRICES, gathered)}
    sp = _gathered_to_full(gathered[-1], SMALL_SHARDED, blocks)
    sp.update({n: w[n] for n in REPLICATED})

    lay = _matrix_layouts(w16, sp)
    wz = {k: jnp.zeros(shape, F32) for k, shape in lay['padded'].items()}
    y, vjp = jax.vjp(lambda x_, wz_, sp_: _forward(x_, wz_, sp_, p[:, 0], w16, depth), x[0], wz, sp)
    loss_local, dy = _loss_head(y, loss_target[0])
    dx, dwz, dsp = vjp(dy)
    loss = lax.psum(loss_local, ("x", "y", "c"))
    gfull = dict(dsp)
    gblocks = {}
    for n in MATRICES:
        layers = range(w16[n].shape[0])
        ax = SHARD_AXIS[n]
        if n == 'ffn_up':
            gblocks[n] = jnp.concatenate(
                [jnp.concatenate([_full_to_blocks(dwz[f"{k}.{j}"][None], ax, N_DEV // 2) for j in layers], axis=1)
                 for k in ("ffn_up", "ffn_up_val")], axis=0)
        else:
            gblocks[n] = jnp.concatenate(
                [_full_to_blocks(lay['unpad'][n](dwz[f"{n}.{j}"])[None], ax) for j in layers], axis=1)

    rep_flat = jnp.concatenate([gfull[n].reshape(-1) for n in REPLICATED])
    rep_n = rep_flat.shape[0]
    piece = _rup(rep_n, N_DEV * LANES) // N_DEV
    rep_pad = lambda t: jnp.pad(t, (0, N_DEV * piece - rep_n))
    small_parts = jnp.concatenate([_full_to_blocks(gfull[n], SHARD_AXIS[n]).reshape(N_DEV, -1) for n in SMALL_SHARDED]
                                  + [rep_pad(rep_flat).reshape(N_DEV, piece)], axis=1)
    n_flat = small_parts.shape[1]
    n_pad = _rup(n_flat, 16 * FLAT_COLS)
    small_parts = jnp.pad(small_parts, ((0, 0), (0, n_pad - n_flat)))

    def my_small(d):
        rep = rep_pad(jnp.concatenate([d[n].reshape(-1) for n in REPLICATED]))
        mine = lax.dynamic_slice(rep, (me * piece,), (piece,))
        flat = jnp.concatenate([d[n].reshape(-1) for n in SMALL_SHARDED] + [mine])
        return jnp.pad(flat, (0, n_pad - n_flat)).reshape(n_pad // FLAT_COLS, FLAT_COLS)

    by_chip = lambda t, cols: t.reshape(4, 2, -1, cols)
    parts = [by_chip(gblocks[n], blocks[n][-1]) for n in MATRICES]
    parts.append(by_chip(small_parts, FLAT_COLS))
    wire = [BF16] * len(MATRICES) + [F32]
    tags = MATRICES + ["small"]
    ids = _place_ids()
    recv_a = _sibling_exchange(parts, "reduce_sibling")
    first = [_chip_partials(pt, ra, ids, dt, f"chip_partials_{t}") for pt, ra, dt, t in zip(parts, recv_a, wire, tags)]
    recv1 = _neighbour_exchange(first, 2, "reduce_first_hop")
    recv2 = _neighbour_exchange([_second_hop(f, r1, f"second_hop_{t}") for f, r1, t in zip(first, recv1, tags)],
                                1, "reduce_second_hop")
    mine = [(as_rows(w[n]), as_rows(m[n]), as_rows(v[n])) for n in MATRICES] + [(my_small(w), my_small(m), my_small(v))]
    results = [_adamw(pt, ra, r1, r2, ids, *wmv, f"adamw_{t}")
               for pt, ra, r1, r2, wmv, t in zip(parts, recv_a, recv1, recv2, mine, tags)]
    small_res = [r.reshape(-1) for r in results[-1]]
    rep_res = jnp.stack([r[n_flat - piece:n_flat] for r in small_res])
    rep_rows = _rup(4 * piece, 8 * FLAT_COLS) // FLAT_COLS
    rep_blk = jnp.pad(rep_res.reshape(-1), (0, rep_rows * FLAT_COLS - 4 * piece)).reshape(rep_rows, FLAT_COLS)
    (rep_all,) = _all_gather([rep_blk], "gather_replicated")
    rep_all = rep_all.reshape(N_DEV, -1)[:, :4 * piece]
    rep_all = jnp.transpose(rep_all.reshape(N_DEV, 4, piece), (1, 0, 2)).reshape(4, N_DEV * piece)

    outs = [{}, {}, {}, {}]
    for k in range(4):
        for n, res in zip(MATRICES, results):
            outs[k][n] = res[k].reshape(blocks[n])
        pos = 0
        for n in SMALL_SHARDED:
            size = math.prod(blocks[n])
            outs[k][n] = small_res[k][pos:pos + size].reshape(blocks[n])
            pos += size
        pos = 0
        for n in REPLICATED:
            size = math.prod(blocks[n])
            outs[k][n] = rep_all[k, pos:pos + size].reshape(blocks[n])
            pos += size
    return (loss, dx[None], *[outs[0][n] for n in WEIGHTS], *[outs[1][n] for n in WEIGHTS],
            *[outs[2][n] for n in WEIGHTS], *[outs[3][n] for n in WEIGHTS])


def _matrix_layouts(w16, sp):
    padded, unpad = {}, {}
    ident = lambda g: g
    for n in ('ffn_down', 'ple_proj', 'ple_gate_w', 'even_w_out', 'odd_w_out'):
        for j in range(w16[n].shape[0]):
            padded[f"{n}.{j}"] = w16[n].shape[1:]
        unpad[n] = ident
    for j in range(w16['ffn_up'].shape[0]):
        half = (w16['ffn_up'].shape[1], w16['ffn_up'].shape[2] // 2)
        padded[f"ffn_up.{j}"] = padded[f"ffn_up_val.{j}"] = half
    for n in ('rwkv_w2', 'rwkv_a2', 'rwkv_g2'):
        rows, cols = w16[n].shape[1:]
        for j in range(w16[n].shape[0]):
            padded[f"{n}.{j}"] = (_rup(rows, LANES), cols)
        unpad[n] = functools.partial(lambda g, rows: g[:rows], rows=rows)
    ah, an = sp['rwkv_r_k'].shape[1:]
    bh, bn = sp['gdn_A_log'].shape[1], sp['gdn_norm_g'].shape[1]
    ew = [ah * an] * 3 + [w16['rwkv_w2'].shape[1], w16['rwkv_a2'].shape[1], w16['rwkv_g2'].shape[1]] + [bh * bn] * 4 + [bh, bh]
    cwid, ch = sp['mamba_norm_g'].shape[1], sp['mamba_dt_bias'].shape[1]
    dw = sp['lru_lambda'].shape[1]
    ow = [cwid, sp['mamba_conv_w'].shape[2], ch, dw, dw]
    for n, widths in (('even_w_in', ew), ('odd_w_in', ow)):
        offs, total = _aligned_layout(widths)
        for j in range(w16[n].shape[0]):
            padded[f"{n}.{j}"] = (w16[n].shape[1], total)
        unpad[n] = functools.partial(_unpad_cols, offs=offs, widths=widths)
    return {'padded': padded, 'unpad': unpad}
```

```python
import functools
import math

import jax
import jax.numpy as jnp
from jax import lax
from jax.experimental import pallas as pl
from jax.experimental.pallas import tpu as pltpu

F32 = jnp.float32
BF16 = jnp.bfloat16
HI = lax.Precision.HIGHEST
SDS = jax.ShapeDtypeStruct
MESH = pl.DeviceIdType.MESH

LANES = 128
VMEM_LIMIT = 56 * 1024 * 1024
N_DEV = 8
FLAT_COLS = 1024
MM_VMEM = 40 * 1024 * 1024

LN_EPS = 1e-5
RMS_EPS = 1e-6
L2_EPS = 1e-6
A_GN_EPS = 64e-5
LRU_C = 8.0
C_GROUPS = 4
RWKV_CHUNK = 64
GDN_CHUNK = 64
SSD_CHUNK = 128
SCAN_HEADS = 32

ADAM_LR, ADAM_B1, ADAM_B2, ADAM_EPS, ADAM_WD, ADAM_STEP = 0.001, 0.9, 0.999, 1e-08, 0.01, 10

WEIGHTS = ['ln1_g', 'ln1_b', 'ln2_g', 'ln2_b', 'ffn_up', 'ffn_conv_w', 'ffn_conv_b', 'ffn_down', 'ple_proj',
           'ple_norm_g', 'ple_gate_w', 'ple_gate_b', 'even_w_in', 'even_w_out', 'rwkv_mu', 'rwkv_w0', 'rwkv_w2',
           'rwkv_a0', 'rwkv_a2', 'rwkv_g2', 'rwkv_k_k', 'rwkv_k_a', 'rwkv_r_k', 'rwkv_gn_g', 'rwkv_gn_b',
           'gdn_conv_w', 'gdn_A_log', 'gdn_dt_bias', 'gdn_norm_g', 'odd_w_in', 'odd_w_out', 'mamba_conv_w',
           'mamba_conv_b', 'mamba_dt_bias', 'mamba_A_log', 'mamba_D', 'mamba_norm_g', 'lru_conv_w', 'lru_conv_b',
           'lru_wa', 'lru_ba', 'lru_wx', 'lru_bx', 'lru_lambda']
SHARD_AXIS = {'ffn_up': 2, 'ffn_conv_w': 2, 'ffn_down': 1, 'ple_proj': 2, 'ple_gate_w': 1, 'even_w_in': 2,
              'even_w_out': 1, 'rwkv_w2': 2, 'rwkv_a2': 2, 'rwkv_g2': 2, 'gdn_conv_w': 2, 'odd_w_in': 2,
              'odd_w_out': 1, 'mamba_conv_w': 2, 'mamba_conv_b': 1, 'mamba_norm_g': 1, 'lru_conv_w': 2,
              'lru_conv_b': 1, 'lru_ba': 1, 'lru_bx': 1, 'lru_lambda': 1}
MATRICES = ['ffn_up', 'ffn_down', 'ple_proj', 'ple_gate_w', 'even_w_in', 'even_w_out', 'rwkv_w2', 'rwkv_a2',
            'rwkv_g2', 'odd_w_in', 'odd_w_out']
SMALL_SHARDED = [n for n in WEIGHTS if n in SHARD_AXIS and n not in MATRICES]
REPLICATED = [n for n in WEIGHTS if n not in SHARD_AXIS]


def _cparams(sem):
    return pltpu.CompilerParams(dimension_semantics=sem, vmem_limit_bytes=VMEM_LIMIT)


def _rup(n, m):
    return -(-n // m) * m


def _pick(n, cands):
    for c in cands:
        if n % c == 0:
            return c
    return n


_DIMS = {'nn': (((1,), (0,)), ((), ())), 'nt': (((1,), (1,)), ((), ())), 'tn': (((0,), (0,)), ((), ()))}


def _mm(a, b, mode, name):
    if mode == 'tn':
        K, M = a.shape
    else:
        M, K = a.shape
    N = b.shape[0] if mode == 'nt' else b.shape[1]
    tm = _pick(M, (1024, 512, 256, 128))
    tn = _pick(N, (1024, 512, 256, 128))
    room = MM_VMEM - 3 * tm * tn * 4
    per_k = 2 * (tm * a.dtype.itemsize + tn * b.dtype.itemsize)
    tk = max([t for t in range(LANES, K + 1, LANES) if K % t == 0 and t * per_k <= room] or [K])
    nk = K // tk

    def body(a_ref, b_ref, o_ref, acc_ref):
        k = pl.program_id(2)
        part = lax.dot_general(a_ref[...].astype(BF16), b_ref[...].astype(BF16), _DIMS[mode],
                               preferred_element_type=F32)

        @pl.when(k == 0)
        def _():
            acc_ref[...] = part

        @pl.when(k > 0)
        def _():
            acc_ref[...] += part

        @pl.when(k == nk - 1)
        def _():
            o_ref[...] = acc_ref[...]

    a_spec = pl.BlockSpec((tk, tm), lambda i, j, k: (k, i)) if mode == 'tn' else pl.BlockSpec((tm, tk), lambda i, j, k: (i, k))
    b_spec = pl.BlockSpec((tn, tk), lambda i, j, k: (j, k)) if mode == 'nt' else pl.BlockSpec((tk, tn), lambda i, j, k: (k, j))
    return pl.pallas_call(
        body, grid=(M // tm, N // tn, nk), in_specs=[a_spec, b_spec],
        out_specs=pl.BlockSpec((tm, tn), lambda i, j, k: (i, j)), out_shape=SDS((M, N), F32),
        scratch_shapes=[pltpu.VMEM((tm, tn), F32)], name=name,
        compiler_params=_cparams(("parallel", "parallel", "arbitrary")))(a, b)


def _matmul(x, w16, wz, name):
    @jax.custom_vjp
    def op(x, wz):
        return _mm(x.astype(BF16), w16, 'nn', name + "_f")

    def op_f(x, wz):
        x16 = x.astype(BF16)
        return _mm(x16, w16, 'nn', name + "_f"), x16

    def op_b(x16, g):
        g16 = g.astype(BF16)
        return _mm(g16, w16, 'nt', name + "_dx"), _mm(x16, g16, 'tn', name + "_dw")

    op.defvjp(op_f, op_b)
    return op(x, wz)


def _block_diag_pair(x, wa, wx, name):
    T = x.shape[0]
    nb, bd, _ = wa.shape
    dot = lambda a, b, mode: lax.dot_general(a.astype(BF16), b.astype(BF16), _DIMS[mode], preferred_element_type=F32)
    cols = pl.BlockSpec((T, bd), lambda n: (0, n))
    blk = pl.BlockSpec((1, bd, bd), lambda n: (n, 0, 0))

    def fwd_call(x, wa, wx):
        def body(x_ref, wa_ref, wx_ref, ra_ref, rx_ref):
            ra_ref[...] = dot(x_ref[...], wa_ref[0], 'nn')
            rx_ref[...] = dot(x_ref[...], wx_ref[0], 'nn')

        return pl.pallas_call(body, grid=(nb,), in_specs=[cols, blk, blk], out_specs=[cols, cols],
                              out_shape=[SDS(x.shape, F32)] * 2, name=name + "_f",
                              compiler_params=_cparams(("parallel",)))(x, wa, wx)

    def bwd_call(x, wa, wx, ga, gx):
        def body(x_ref, wa_ref, wx_ref, ga_ref, gx_ref, dx_ref, dwa_ref, dwx_ref):
            dx_ref[...] = dot(ga_ref[...], wa_ref[0], 'nt') + dot(gx_ref[...], wx_ref[0], 'nt')
            dwa_ref[0] = dot(x_ref[...], ga_ref[...], 'tn')
            dwx_ref[0] = dot(x_ref[...], gx_ref[...], 'tn')

        return pl.pallas_call(body, grid=(nb,), in_specs=[cols, blk, blk, cols, cols], out_specs=[cols, blk, blk],
                              out_shape=[SDS(x.shape, F32), SDS(wa.shape, F32), SDS(wx.shape, F32)], name=name + "_b",
                              compiler_params=_cparams(("parallel",)))(x, wa, wx, ga, gx)

    @jax.custom_vjp
    def op(x, wa, wx):
        return tuple(fwd_call(x, wa, wx))

    def op_f(x, wa, wx):
        return op(x, wa, wx), (x, wa, wx)

    def op_b(res, g):
        return tuple(bwd_call(*res, *g))

    op.defvjp(op_f, op_b)
    return op(x, wa, wx)


def _tile_op(name, fn, arrs, params, consts, by_rows, width=LANES):
    arrs, params, consts = tuple(arrs), tuple(params), tuple(consts)
    na, npar, nc = len(arrs), len(params), len(consts)
    T = arrs[0].shape[0]
    if by_rows:
        tile = _pick(T, (256, 128, 64, 32, 16, 8))
        grid = (T // tile,)
        arr_block = lambda a: (tile, a.shape[1])
        arr_spec = lambda a: pl.BlockSpec((tile, a.shape[1]), lambda i: (i, 0))
        par_block = lambda p: p.shape
        par_spec = lambda p: pl.BlockSpec(p.shape, lambda i: (0, 0))
    else:
        grid = (arrs[0].shape[1] // width,)
        arr_block = lambda a: (T, width)
        arr_spec = lambda a: pl.BlockSpec((T, width), lambda i: (0, i))
        par_block = lambda p: (p.shape[0], width)
        par_spec = lambda p: pl.BlockSpec((p.shape[0], width), lambda i: (0, i))
    const_spec = lambda c: pl.BlockSpec(c.shape, lambda i: (0,) * c.ndim)
    outs_sds = jax.eval_shape(fn, *[SDS(arr_block(a), F32) for a in arrs], *[SDS(par_block(p), F32) for p in params],
                              *[SDS(c.shape, c.dtype) for c in consts])
    out_widths = [o.shape[1] for o in outs_sds]
    nout = len(out_widths)
    if by_rows:
        out_shapes = [SDS((T, w), F32) for w in out_widths]
        out_specs = [pl.BlockSpec((tile, w), lambda i: (i, 0)) for w in out_widths]
    else:
        out_shapes = [SDS((T, grid[0] * w), F32) for w in out_widths]
        out_specs = [pl.BlockSpec((T, w), lambda i: (0, i)) for w in out_widths]

    def fwd_call(arrs, params):
        def body(*refs):
            outs = fn(*[r[...] for r in refs[:na + npar + nc]])
            for o_ref, o in zip(refs[na + npar + nc:], outs):
                o_ref[...] = o

        return pl.pallas_call(
            body, grid=grid, in_specs=[arr_spec(a) for a in arrs] + [par_spec(p) for p in params] + [const_spec(c) for c in consts],
            out_specs=out_specs, out_shape=out_shapes, name=name + "_f",
            compiler_params=_cparams(("parallel",)))(*arrs, *params, *consts)

    def bwd_call(arrs, params, cts):
        def body(*refs):
            ins = refs[:na + npar + nc + nout]
            outs = refs[na + npar + nc + nout:]
            av = [r[...] for r in ins[:na]]
            pv = [r[...] for r in ins[na:na + npar]]
            cv = [r[...] for r in ins[na + npar:na + npar + nc]]
            gv = [r[...] for r in ins[na + npar + nc:]]
            _, vjp = jax.vjp(lambda *t: fn(*t, *cv), *av, *pv)
            grads = vjp(tuple(gv))
            for o_ref, g in zip(outs[:na], grads[:na]):
                o_ref[...] = g
            if by_rows and npar:
                @pl.when(pl.program_id(0) == 0)
                def _():
                    for o_ref in outs[na:]:
                        o_ref[...] = jnp.zeros_like(o_ref)

                for o_ref, g in zip(outs[na:], grads[na:]):
                    o_ref[...] += g
            else:
                for o_ref, g in zip(outs[na:], grads[na:]):
                    o_ref[...] = g

        return pl.pallas_call(
            body, grid=grid,
            in_specs=[arr_spec(a) for a in arrs] + [par_spec(p) for p in params] + [const_spec(c) for c in consts] + out_specs,
            out_specs=[arr_spec(a) for a in arrs] + [par_spec(p) for p in params],
            out_shape=[SDS(a.shape, F32) for a in arrs] + [SDS(p.shape, F32) for p in params], name=name + "_b",
            compiler_params=_cparams(("arbitrary",) if by_rows else ("parallel",)))(*arrs, *params, *consts, *cts)

    @jax.custom_vjp
    def op(arrs, params):
        return tuple(fwd_call(arrs, params))

    def op_f(arrs, params):
        return op(arrs, params), (arrs, params)

    def op_b(res, cts):
        arrs, params = res
        g = bwd_call(arrs, params, cts)
        return tuple(g[:na]), tuple(g[na:])

    op.defvjp(op_f, op_b)
    return op(arrs, params)


def _rowwise(name, fn, arrs, params=(), consts=()):
    return _tile_op(name, fn, arrs, params, consts, True)


def _colwise(name, fn, arrs, params=(), width=LANES):
    return _tile_op(name, fn, arrs, params, (), False, width)


@functools.partial(jax.custom_vjp, nondiff_argnums=(1,))
def _shift(x, k):
    rows = lax.broadcasted_iota(jnp.int32, x.shape, 0)
    return jnp.where(rows >= k, pltpu.roll(x, k, 0), 0.0)


def _shift_f(x, k):
    return _shift(x, k), None


def _shift_b(k, _, g):
    n = g.shape[0]
    rows = lax.broadcasted_iota(jnp.int32, g.shape, 0)
    return (jnp.where(rows < n - k, pltpu.roll(g, n - k, 0), 0.0),)


_shift.defvjp(_shift_f, _shift_b)


def _causal_conv(x, w):
    K = w.shape[0]
    y = x * w[K - 1:K, :]
    for j in range(K - 1):
        y = y + _shift(x, K - 1 - j) * w[j:j + 1, :]
    return y


def _silu(x):
    return x * jax.nn.sigmoid(x)


def _softplus(x):
    return jnp.maximum(x, 0.0) + jnp.log1p(jnp.exp(-jnp.abs(x)))


def _split_cols(h, offs, widths):
    @jax.custom_vjp
    def op(h):
        return tuple(h[:, o:o + w] for o, w in zip(offs, widths))

    def op_f(h):
        return op(h), None

    def op_b(_, cts):
        parts, pos = [], 0
        T = cts[0].shape[0]
        for o, w, c in zip(offs, widths, cts):
            if o > pos:
                parts.append(jnp.zeros((T, o - pos), F32))
            parts.append(c)
            pos = o + w
        if pos < h.shape[1]:
            parts.append(jnp.zeros((T, h.shape[1] - pos), F32))
        return (jnp.concatenate(parts, axis=1),)

    op.defvjp(op_f, op_b)
    return op(h)


def _group_ones(width, group):
    g = jnp.arange(width) // group
    return (g[:, None] == g[None, :]).astype(F32)


def _layer_norm_rows(x, g, b, eps):
    mu = jnp.mean(x, axis=1, keepdims=True)
    var = jnp.mean(jnp.square(x - mu), axis=1, keepdims=True)
    return (x - mu) * lax.rsqrt(var + eps) * g + b


def _tri(L, strict=False):
    i = lax.broadcasted_iota(jnp.int32, (L, L), 0)
    j = lax.broadcasted_iota(jnp.int32, (L, L), 1)
    return (i > j) if strict else (i >= j)


def _cumsum_rows(x):
    H, L, _ = x.shape
    tri = jnp.broadcast_to(_tri(L).astype(F32)[None], (H, L, L))
    return jnp.einsum('hls,hsn->hln', tri, x, precision=HI)


def _col_to_row(c):
    L = c.shape[1]
    return jnp.sum(c * _tri_eye(L)[None], axis=1, keepdims=True)


def _row_to_col(r):
    N = r.shape[2]
    return jnp.sum(r * _tri_eye(N)[None], axis=2, keepdims=True)


def _scalar_col(t):
    return _row_to_col(t.reshape(t.shape[0], 1, t.shape[3]))


def _tri_eye(L):
    i = lax.broadcasted_iota(jnp.int32, (L, L), 0)
    j = lax.broadcasted_iota(jnp.int32, (L, L), 1)
    return (i == j).astype(F32)


def _unit_lower_inverse(n_strict):
    L = n_strict.shape[1]
    inv = _tri_eye(L)[None] + n_strict
    x = n_strict
    p = 2
    while p < L:
        x = jnp.einsum('hij,hjk->hik', x, x)
        inv = inv + jnp.einsum('hij,hjk->hik', inv, x)
        p *= 2
    return inv


def _rwkv_chunk(r, lw, k, v, a, b, h0):
    L = r.shape[1]
    mm = jnp.einsum
    cum = _cumsum_rows(lw)
    cum_l = jnp.sum(lw, axis=1, keepdims=True)
    e_neg = jnp.exp(-cum)
    rt, bt, kt, at = r * jnp.exp(cum), b * e_neg, k * e_neg, a * jnp.exp(cum - lw)
    to_end = jnp.exp(cum_l - cum)
    strict, incl = _tri(L, True)[None], _tri(L)[None]
    n = jnp.where(strict, mm('hld,hsd->hls', at, bt), 0.0)
    mk = jnp.where(strict, mm('hld,hsd->hls', at, kt), 0.0)
    u = mm('hls,hsv->hlv', _unit_lower_inverse(n), mm('hld,hdv->hlv', at, h0) + mm('hls,hsv->hlv', mk, v))
    y = (mm('hld,hdv->hlv', rt, h0) + mm('hls,hsv->hlv', jnp.where(incl, mm('hld,hsd->hls', rt, bt), 0.0), u)
         + mm('hls,hsv->hlv', jnp.where(incl, mm('hld,hsd->hls', rt, kt), 0.0), v))
    h1 = (_row_to_col(jnp.exp(cum_l)) * h0 + mm('hld,hlv->hdv', b * to_end, u) + mm('hld,hlv->hdv', k * to_end, v))
    return y, h1


def _gdn_chunk(q, k, v, beta, lg, h0):
    C, D = q.shape[1], q.shape[2]
    scale = D ** -0.5
    beta, lg = _scalar_col(beta), _scalar_col(lg)
    gc = _cumsum_rows(lg)
    gc_l = jnp.sum(lg, axis=1, keepdims=True)
    causal, strict = _tri(C)[None], _tri(C, True)[None]
    decay = jnp.exp(jnp.where(causal, gc - _col_to_row(gc), -jnp.inf))
    k_beta = k * beta
    m = jnp.where(strict, jnp.einsum('hcd,hsd->hcs', k_beta, k) * decay, 0.0)
    inv = _unit_lower_inverse(-m)
    e_gc = jnp.exp(gc)
    u = jnp.einsum('hcs,hsd->hcd', inv, v * beta)
    w = jnp.einsum('hcs,hsd->hcd', inv, k_beta * e_gc)
    attn = jnp.where(causal, jnp.einsum('hcd,hsd->hcs', q * scale, k) * decay, 0.0)
    v_new = u - jnp.einsum('hcd,hde->hce', w, h0)
    o = jnp.einsum('hcd,hde->hce', q * scale * e_gc, h0) + jnp.einsum('hcs,hse->hce', attn, v_new)
    h1 = h0 * jnp.exp(gc_l) + jnp.einsum('hcd,hce->hde', k * jnp.exp(gc_l - gc), v_new)
    return o, h1


def _ssd_chunk(xs, dt, aa, bm, cm, h0):
    H, L, _ = xs.shape
    G = bm.shape[0]
    per_head = lambda t: jnp.broadcast_to(t[:, None], (G, H // G) + t.shape[1:]).reshape((H,) + t.shape[1:])
    dt, aa = _scalar_col(dt), _scalar_col(aa)
    x = xs * dt
    cs = _cumsum_rows(aa)
    cs_l = jnp.sum(aa, axis=1, keepdims=True)
    causal = _tri(L)[None]
    cb = per_head(jnp.einsum('gln,gsn->gls', cm, bm))
    wd = jnp.where(causal, cb * jnp.exp(jnp.where(causal, cs - _col_to_row(cs), -jnp.inf)), 0.0)
    cmb, bmb = per_head(cm), per_head(bm)
    y = jnp.einsum('hls,hsp->hlp', wd, x) + jnp.einsum('hln,hnp->hlp', cmb, h0) * jnp.exp(cs)
    h1 = jnp.exp(cs_l) * h0 + jnp.einsum('hln,hlp->hnp', bmb, x * jnp.exp(cs_l - cs))
    return y, h1


def _chunk_scan(name, fn, seqs, hb, L, state_shape, out_width):
    seqs = tuple(seqs)
    ns = len(seqs)
    H = max(s.shape[0] for s in seqs)
    T = max(s.shape[1] for s in seqs)
    nc, nh = T // L, H // hb
    lead = [hb * s.shape[0] // H for s in seqs]
    st_block = (hb,) + state_shape

    def seq_spec(s, l, imap):
        if s.ndim == 4:
            return pl.BlockSpec((l, 1, 1, L), lambda h, c: imap(h, c) + (0,))
        return pl.BlockSpec((l, L, s.shape[2]), imap)

    fmap = lambda h, c: (h, c, 0)
    rmap = lambda h, c: (h, nc - 1 - c, 0)

    def fwd_call(seqs):
        def body(*refs):
            y_ref, st_ref, carry = refs[ns], refs[ns + 1], refs[ns + 2]

            @pl.when(pl.program_id(1) == 0)
            def _():
                carry[...] = jnp.zeros_like(carry)

            h0 = carry[...]
            st_ref[0] = h0
            y, h1 = fn(*[r[...] for r in refs[:ns]], h0)
            y_ref[...] = y
            carry[...] = h1

        return pl.pallas_call(
            body, grid=(nh, nc), in_specs=[seq_spec(s, l, fmap) for s, l in zip(seqs, lead)],
            out_specs=[pl.BlockSpec((hb, L, out_width), fmap),
                       pl.BlockSpec((1,) + st_block, lambda h, c: (c, h) + (0,) * len(state_shape))],
            out_shape=[SDS((H, T, out_width), F32), SDS((nc, H) + state_shape, F32)],
            scratch_shapes=[pltpu.VMEM(st_block, F32)], name=name + "_f",
            compiler_params=_cparams(("parallel", "arbitrary")))(*seqs)

    def bwd_call(seqs, states, dy):
        def body(*refs):
            st_ref, dy_ref = refs[ns], refs[ns + 1]
            outs, carry = refs[ns + 2:2 * ns + 2], refs[2 * ns + 2]

            @pl.when(pl.program_id(1) == 0)
            def _():
                carry[...] = jnp.zeros_like(carry)

            _, vjp = jax.vjp(fn, *[r[...] for r in refs[:ns]], st_ref[0])
            grads = vjp((dy_ref[...], carry[...]))
            for o_ref, g in zip(outs, grads[:ns]):
                o_ref[...] = g
            carry[...] = grads[ns]

        return pl.pallas_call(
            body, grid=(nh, nc),
            in_specs=[seq_spec(s, l, rmap) for s, l in zip(seqs, lead)]
            + [pl.BlockSpec((1,) + st_block, lambda h, c: (nc - 1 - c, h) + (0,) * len(state_shape)),
               pl.BlockSpec((hb, L, out_width), rmap)],
            out_specs=[seq_spec(s, l, rmap) for s, l in zip(seqs, lead)],
            out_shape=[SDS(s.shape, F32) for s in seqs],
            scratch_shapes=[pltpu.VMEM(st_block, F32)], name=name + "_b",
            compiler_params=_cparams(("parallel", "arbitrary")))(*seqs, states, dy)

    @jax.custom_vjp
    def op(seqs):
        return fwd_call(seqs)[0]

    def op_f(seqs):
        y, states = fwd_call(seqs)
        return y, (seqs, states)

    def op_b(res, dy):
        seqs, states = res
        return (tuple(bwd_call(seqs, states, dy)),)

    op.defvjp(op_f, op_b)
    return op(seqs)


def _lru_scan_call(a, u, h, reverse, name):
    T, C = a.shape
    cw = _pick(C, (1024, 512, 256, 128))
    tt = _pick(T, (512, 256, 128, 64, 32, 16, 8))
    nt, ng = T // tt, tt // 8
    sub = lambda: lax.broadcasted_iota(jnp.int32, (8, cw), 0)
    first = lambda: pl.program_id(1) == 0

    def fwd_body(a_ref, u_ref, h_ref, carry_ref):
        @pl.when(first())
        def _():
            carry_ref[...] = jnp.zeros_like(carry_ref)

        def group(i, carry):
            r0 = pl.multiple_of(i * 8, 8)
            ab, ub = a_ref[pl.ds(r0, 8), :], u_ref[pl.ds(r0, 8), :]
            out = jnp.zeros((8, cw), F32)
            for j in range(8):
                carry = ab[j:j + 1, :] * carry + ub[j:j + 1, :]
                out = jnp.where(sub() == j, carry, out)
            h_ref[pl.ds(r0, 8), :] = out
            return carry

        carry_ref[...] = lax.fori_loop(0, ng, group, carry_ref[...])

    def bwd_body(a_ref, u_ref, h_ref, hp_ref, g_ref, da_ref, cg_ref, ca_ref):
        @pl.when(first())
        def _():
            cg_ref[...] = jnp.zeros_like(cg_ref)
            ca_ref[...] = jnp.zeros_like(ca_ref)

        h_before = jnp.where(pl.program_id(1) < nt - 1, hp_ref[7:8, :], 0.0)

        def group(i, carry):
            g_next, a_next = carry
            gi = ng - 1 - i
            r0 = pl.multiple_of(gi * 8, 8)
            rp = pl.multiple_of(jnp.maximum(gi - 1, 0) * 8, 8)
            ab, ub, hb = a_ref[pl.ds(r0, 8), :], u_ref[pl.ds(r0, 8), :], h_ref[pl.ds(r0, 8), :]
            h_last_prev = jnp.where(gi > 0, h_ref[pl.ds(rp, 8), :][7:8, :], h_before)
            g_out = jnp.zeros((8, cw), F32)
            da_out = jnp.zeros((8, cw), F32)
            for j in range(7, -1, -1):
                g_next = ub[j:j + 1, :] + a_next * g_next
                a_next = ab[j:j + 1, :]
                h_prev = hb[j - 1:j, :] if j > 0 else h_last_prev
                g_out = jnp.where(sub() == j, g_next, g_out)
                da_out = jnp.where(sub() == j, g_next * h_prev, da_out)
            g_ref[pl.ds(r0, 8), :] = g_out
            da_ref[pl.ds(r0, 8), :] = da_out
            return g_next, a_next

        cg_ref[...], ca_ref[...] = lax.fori_loop(0, ng, group, (cg_ref[...], ca_ref[...]))

    row = pltpu.VMEM((1, cw), F32)
    if not reverse:
        spec = pl.BlockSpec((tt, cw), lambda i, t: (t, i))
        return pl.pallas_call(fwd_body, grid=(C // cw, nt), in_specs=[spec, spec], out_specs=spec,
                              out_shape=SDS((T, C), F32), scratch_shapes=[row], name=name,
                              compiler_params=_cparams(("parallel", "arbitrary")))(a, u)
    spec = pl.BlockSpec((tt, cw), lambda i, t: (nt - 1 - t, i))
    before = pl.BlockSpec((8, cw), lambda i, t: (jnp.maximum((nt - 1 - t) * ng - 1, 0), i))
    return pl.pallas_call(bwd_body, grid=(C // cw, nt), in_specs=[spec, spec, spec, before], out_specs=[spec, spec],
                          out_shape=[SDS((T, C), F32), SDS((T, C), F32)], scratch_shapes=[row, row], name=name,
                          compiler_params=_cparams(("parallel", "arbitrary")))(a, u, h, h)


@jax.custom_vjp
def _lru_scan(a, u):
    return _lru_scan_call(a, u, None, False, "lru_scan_f")


def _lru_scan_f(a, u):
    h = _lru_scan(a, u)
    return h, (a, h)


def _lru_scan_b(res, dh):
    a, h = res
    g, da = _lru_scan_call(a, dh, h, True, "lru_scan_b")
    return da, g


_lru_scan.defvjp(_lru_scan_f, _lru_scan_b)


def _heads_major(x, nheads):
    T, W = x.shape
    return jnp.transpose(x.reshape(T, nheads, W // nheads), (1, 0, 2))


def _tokens_major(x):
    H, T, N = x.shape
    return jnp.transpose(x, (1, 0, 2)).reshape(T, H * N)


def _pad_cols(w, offs, widths, total):
    parts, pos, src = [], 0, 0
    for o, wd in zip(offs, widths):
        if o > pos:
            parts.append(jnp.zeros((w.shape[0], o - pos), w.dtype))
        parts.append(w[:, src:src + wd])
        src += wd
        pos = o + wd
    if pos < total:
        parts.append(jnp.zeros((w.shape[0], total - pos), w.dtype))
    return jnp.concatenate(parts, axis=1)


def _unpad_cols(w, offs, widths):
    return jnp.concatenate([w[:, o:o + wd] for o, wd in zip(offs, widths)], axis=1)


def _aligned_layout(widths):
    offs, pos = [], 0
    for w in widths:
        offs.append(pos)
        pos += _rup(w, LANES)
    return offs, _rup(pos, 512)


def _pad_lanes(v, n):
    return jnp.pad(v, ((0, 0), (0, n - v.shape[1])))


def _even_mixer(x, q, wz, li):
    T = x.shape[0]
    ah, an = q['rwkv_r_k'].shape
    aw = ah * an
    bh, bn = q['gdn_A_log'].shape[0], q['gdn_norm_g'].shape[0]
    bw = bh * bn
    lw_, la_, lg_ = q['rwkv_w2'].shape[0], q['rwkv_a2'].shape[0], q['rwkv_g2'].shape[0]
    widths = [aw, aw, aw, lw_, la_, lg_, bw, bw, bw, bw, bh, bh]
    offs, total = _aligned_layout(widths)
    pw = [_rup(w, LANES) for w in widths]
    hcols = _matmul(x, _pad_cols(q['even_w_in'], offs, widths, total), wz['even_w_in'], f"even_in{li}")
    a_w = offs[6]
    a_cols, bq, bk, bv, bz, beta_raw, alpha_raw = _split_cols(hcols, [0] + offs[6:], [a_w] + pw[6:])

    mu = _pad_cols(q['rwkv_mu'][None], offs[:6], widths[:6], a_w)
    (xs,) = _colwise(f"rwkv_shift{li}", lambda h, m: (h + (_shift(h, 1) - h) * m,), [a_cols], [mu])
    r, k, v, w_lo, a_lo, g_lo = _split_cols(xs, offs[:6], pw[:6])
    tw, sg = _rowwise(f"rwkv_lora_act{li}", lambda w, g: (jnp.tanh(w), jax.nn.sigmoid(g)), [w_lo, g_lo])
    pad_rows = lambda w, n: jnp.pad(w, ((0, n - w.shape[0]), (0, 0)))
    wl = _matmul(tw, pad_rows(q['rwkv_w2'], pw[3]), wz['rwkv_w2'], f"rwkv_w2{li}")
    al = _matmul(a_lo, pad_rows(q['rwkv_a2'], pw[4]), wz['rwkv_a2'], f"rwkv_a2{li}")
    g = _matmul(sg, pad_rows(q['rwkv_g2'], pw[5]), wz['rwkv_g2'], f"rwkv_g2{li}")
    ones_a = _group_ones(aw, an)

    def pre(k, wl, al, w0, a0, k_k, k_a, ones):
        lw = -jnp.exp(-_softplus(-(w0 + wl)) - 0.5)
        a = jax.nn.sigmoid(a0 + al)
        kk = k * k_k
        kk = kk * lax.rsqrt(jnp.dot(kk * kk, ones) + L2_EPS)
        return lw, k * (1.0 + (a - 1.0) * k_a), -kk, kk * a

    lw, k2, sa, sb = _rowwise(f"rwkv_pre{li}", pre, [k, wl, al],
                              [q['rwkv_w0'][None], q['rwkv_a0'][None], q['rwkv_k_k'][None], q['rwkv_k_a'][None]], [ones_a])
    hm = lambda t: _heads_major(t, ah)
    out = _chunk_scan(f"rwkv_scan{li}", _rwkv_chunk, [hm(r), hm(lw), hm(k2), hm(v), hm(sa), hm(sb)],
                      min(ah, SCAN_HEADS), min(RWKV_CHUNK, T), (an, an), an)
    out = _tokens_major(out)

    def post(out, r, k2, v, g, gn_g, gn_b, r_k, ones):
        mean = jnp.dot(out, ones) * (1.0 / an)
        cen = out - mean
        var = jnp.dot(cen * cen, ones) * (1.0 / an)
        normed = cen * lax.rsqrt(var + A_GN_EPS) * gn_g + gn_b
        bonus = jnp.dot(r * k2 * r_k, ones) * v
        return ((normed + bonus) * g,)

    flat = lambda t: t.reshape(1, -1)
    (ya,) = _rowwise(f"rwkv_post{li}", post, [out, r, k2, v, g],
                     [flat(q['rwkv_gn_g']), flat(q['rwkv_gn_b']), flat(q['rwkv_r_k'])], [ones_a])

    cw = q['gdn_conv_w']

    def conv_l2(x, w):
        y = _silu(_causal_conv(x, w))
        return (y * lax.rsqrt(jnp.sum(y * y, axis=1, keepdims=True) + L2_EPS),)

    if bn == LANES:
        (gq,) = _colwise(f"gdn_conv_q{li}", conv_l2, [bq], [cw[:, :bw]])
        (gk,) = _colwise(f"gdn_conv_k{li}", conv_l2, [bk], [cw[:, bw:2 * bw]])
    else:
        raise NotImplementedError("gated DeltaNet head width must equal the lane count")
    (gv,) = _colwise(f"gdn_conv_v{li}", lambda x, w: (_silu(_causal_conv(x, w)),), [bv], [cw[:, 2 * bw:]])

    def gates(beta_raw, alpha_raw, a_log, dt_bias):
        return jax.nn.sigmoid(beta_raw), -jnp.exp(a_log) * _softplus(alpha_raw + dt_bias)

    beta, lg = _rowwise(f"gdn_gates{li}", gates, [beta_raw, alpha_raw],
                        [_pad_lanes(q['gdn_A_log'][None], pw[10]), _pad_lanes(q['gdn_dt_bias'][None], pw[11])])
    gl = min(GDN_CHUNK, T)
    col = lambda t: jnp.transpose(t[:, :bh]).reshape(bh, T // gl, 1, gl)
    hmb = lambda t: _heads_major(t, bh)
    o = _chunk_scan(f"gdn_scan{li}", _gdn_chunk, [hmb(gq), hmb(gk), hmb(gv), col(beta), col(lg)],
                    min(bh, SCAN_HEADS), gl, (bn, bn), bn)
    o = _tokens_major(o)
    ones_b = _group_ones(bw, bn)

    def gdn_post(o, z, ng, ones):
        ms = jnp.dot(o * o, ones) * (1.0 / bn)
        return (o * lax.rsqrt(ms + RMS_EPS) * ng * _silu(z),)

    (yb,) = _rowwise(f"gdn_post{li}", gdn_post, [o, bz], [jnp.tile(q['gdn_norm_g'][None], (1, bh))], [ones_b])
    return _matmul(jnp.concatenate([ya, yb], axis=1), q['even_w_out'], wz['even_w_out'], f"even_out{li}")


def _odd_mixer(x, q, wz, li):
    T = x.shape[0]
    ch = q['mamba_dt_bias'].shape[0]
    cwid = q['mamba_norm_g'].shape[0]
    cp = cwid // ch
    xbc_w = q['mamba_conv_w'].shape[1]
    cn = (xbc_w - cwid) // (2 * C_GROUPS)
    dw = q['lru_lambda'].shape[0]
    widths = [cwid, xbc_w, ch, dw, dw]
    offs, total = _aligned_layout(widths)
    pw = [_rup(w, LANES) for w in widths]
    hcols = _matmul(x, _pad_cols(q['odd_w_in'], offs, widths, total), wz['odd_w_in'], f"odd_in{li}")
    z, xbc, dt_raw, y_br, x_br = _split_cols(hcols, offs, pw)

    (xbc_c,) = _colwise(f"mamba_conv{li}", lambda x, w, b: (_silu(_causal_conv(x, w) + b),), [xbc],
                        [q['mamba_conv_w'], q['mamba_conv_b'][None]])
    gn = C_GROUPS * cn
    xs, bm, cm = _split_cols(xbc_c, [0, cwid, cwid + gn], [cwid, gn, gn])

    def dts(dt_raw, dt_bias, a_log):
        dt = _softplus(dt_raw + dt_bias)
        return dt, dt * (-jnp.exp(a_log))

    dt, aa = _rowwise(f"mamba_dt{li}", dts, [dt_raw],
                      [_pad_lanes(q['mamba_dt_bias'][None], pw[2]), _pad_lanes(q['mamba_A_log'][None], pw[2])])
    sl = min(SSD_CHUNK, T)
    col = lambda t: jnp.transpose(t[:, :ch]).reshape(ch, T // sl, 1, sl)
    y = _chunk_scan(f"ssd_scan{li}", _ssd_chunk,
                    [_heads_major(xs, ch), col(dt), col(aa), _heads_major(bm, C_GROUPS), _heads_major(cm, C_GROUPS)],
                    max(ch // C_GROUPS, min(ch, SCAN_HEADS)), sl, (cn, cp), cp)
    y = _tokens_major(y)
    gsz = cwid // C_GROUPS

    def mamba_post(y, xs, z, d, ng):
        yy = (y + xs * d) * _silu(z)
        lane = lax.broadcasted_iota(jnp.int32, yy.shape, 1)
        ms = jnp.zeros_like(yy)
        for gi in range(C_GROUPS):
            sel = (lane >= gi * gsz) & (lane < (gi + 1) * gsz)
            ms = jnp.where(sel, jnp.sum(jnp.where(sel, yy * yy, 0.0), axis=1, keepdims=True) * (1.0 / gsz), ms)
        return (yy * lax.rsqrt(ms + RMS_EPS) * ng,)

    (yc,) = _rowwise(f"mamba_post{li}", mamba_post, [y, xs, z],
                     [jnp.repeat(q['mamba_D'], cp)[None], q['mamba_norm_g'][None]])

    (xc,) = _colwise(f"lru_conv{li}", lambda x, w, b: (_causal_conv(x, w) + b,), [x_br],
                     [q['lru_conv_w'], q['lru_conv_b'][None]])
    ra, ia = _block_diag_pair(xc, q['lru_wa'], q['lru_wx'], f"lru_gates{li}")

    def lru_pre(ra, ia, xc, ba, bx, lam):
        r = jax.nn.sigmoid(ra + ba)
        i = jax.nn.sigmoid(ia + bx)
        log_a = LRU_C * r * (-_softplus(-lam))
        t = 2.0 * log_a
        series = t * (1.0 + t * (0.5 + t * (1.0 / 6.0 + t * (1.0 / 24.0 + t * (1.0 / 120.0 + t * (1.0 / 720.0))))))
        expm1 = jnp.where(t > -0.2, series, jnp.exp(t) - 1.0)
        return jnp.exp(log_a), jnp.sqrt(-expm1) * (i * xc)

    a, u = _rowwise(f"lru_pre{li}", lru_pre, [ra, ia, xc], [q['lru_ba'][None], q['lru_bx'][None], q['lru_lambda'][None]])
    h = _lru_scan(a, u)
    (yd,) = _rowwise(f"lru_post{li}", lambda h, y: (h * jax.nn.gelu(y),), [h, y_br])
    return _matmul(jnp.concatenate([yc, yd], axis=1), q['odd_w_out'], wz['odd_w_out'], f"odd_out{li}")


def _forward(x, wz, sp, p, w16, depth):
    alpha = (2.0 * depth) ** 0.25
    for i in range(depth):
        j = i // 2
        even = i % 2 == 0
        names = [n for n in WEIGHTS if n.startswith(('rwkv_', 'gdn_', 'even_') if even else ('mamba_', 'lru_', 'odd_'))]
        q = {n: (w16[n][j] if n in MATRICES else sp[n][j]) for n in names}
        wzl = {n: wz[f"{n}.{j}"] for n in names if n in MATRICES}
        y = (_even_mixer if even else _odd_mixer)(x, q, wzl, i)

        def ln_res(x, y, g, b):
            return (_layer_norm_rows(alpha * x + y, g, b, LN_EPS),)

        (h,) = _rowwise(f"ln1_{i}", ln_res, [x, y], [sp['ln1_g'][i][None], sp['ln1_b'][i][None]])
        dff = w16['ffn_up'].shape[2] // 2
        gate = _matmul(h, w16['ffn_up'][i][:, :dff], wz[f"ffn_up.{i}"], f"ffn_gate{i}")
        val = _matmul(h, w16['ffn_up'][i][:, dff:], wz[f"ffn_up_val.{i}"], f"ffn_val{i}")
        cw, cb = sp['ffn_conv_w'][i], sp['ffn_conv_b'][i][None]

        def ffn_act(gate, val, wg, wv, bg, bv):
            return (_silu(_causal_conv(gate, wg) + bg) * (_causal_conv(val, wv) + bv),)

        (act,) = _colwise(f"ffn_act{i}", ffn_act, [gate, val], [cw[:, :dff], cw[:, dff:], cb[:, :dff], cb[:, dff:]])
        f = _matmul(act, w16['ffn_down'][i], wz[f"ffn_down.{i}"], f"ffn_down{i}")
        (h2,) = _rowwise(f"ln2_{i}", ln_res, [h, f], [sp['ln2_g'][i][None], sp['ln2_b'][i][None]])
        e0 = _matmul(p[i], w16['ple_proj'][i], wz[f"ple_proj.{i}"], f"ple_proj{i}")
        gl = _matmul(h2, w16['ple_gate_w'][i], wz[f"ple_gate_w.{i}"], f"ple_gate{i}")

        def ple(h2, gl, e0, gb, ng):
            e = e0 * lax.rsqrt(jnp.mean(e0 * e0, axis=1, keepdims=True) + RMS_EPS) * ng
            return (h2 + jax.nn.sigmoid(gl + gb) * e,)

        (x,) = _rowwise(f"ple{i}", ple, [h2, gl, e0], [sp['ple_gate_b'][i][None], sp['ple_norm_g'][i][None]])
    return x


def _loss_head(y, target):
    T, D = y.shape
    tile = _pick(T, (256, 128, 64, 32, 16, 8))

    def body(y_ref, t_ref, dy_ref, l_ref):
        err = y_ref[...] - t_ref[...]
        dy_ref[...] = err * (1.0 / D)

        @pl.when(pl.program_id(0) == 0)
        def _():
            l_ref[...] = jnp.zeros_like(l_ref)

        l_ref[...] += jnp.sum(jnp.sum(err * err, axis=1, keepdims=True), axis=0, keepdims=True) * (0.5 / D) + jnp.zeros_like(l_ref)

    spec = pl.BlockSpec((tile, D), lambda i: (i, 0))
    dy, l = pl.pallas_call(body, grid=(T // tile,), in_specs=[spec, spec],
                           out_specs=[spec, pl.BlockSpec((8, LANES), lambda i: (0, 0))],
                           out_shape=[SDS((T, D), F32), SDS((8, LANES), F32)], name="loss_head",
                           compiler_params=_cparams(("arbitrary",)))(y, target)
    return l[0, 0], dy


def _my_index():
    return 4 * lax.axis_index("x") + 2 * lax.axis_index("y") + lax.axis_index("c")


def _hbm_specs(n):
    return [pl.BlockSpec(memory_space=pl.ANY)] * n


def _all_gather(blocks, name):
    blocks = tuple(blocks)
    n = len(blocks)
    half = [b.shape[0] // 2 for b in blocks]

    def body(*refs):
        ins, outs = refs[:n], refs[n:2 * n]
        send_sems, recv_sems, local_sems = refs[2 * n:]
        x, y, c = lax.axis_index("x"), lax.axis_index("y"), lax.axis_index("c")
        me, sibling, other = (x, y, c), (x, y, 1 - c), 1 - c
        xn, yn, dg = (1 - x, y), (x, 1 - y), (1 - x, 1 - y)

        def slot(i, px, py, pc, h=None):
            ref = outs[i].at[4 * px + 2 * py + pc]
            return ref if h is None else ref.at[pl.ds(h * half[i], half[i])]

        def copy(i, k, blk, to, h=None, src=None):
            dst = slot(i, *blk, h)
            return pltpu.make_async_remote_copy(
                src_ref=dst if src is None else src, dst_ref=dst, send_sem=send_sems.at[9 * i + k],
                recv_sem=recv_sems.at[9 * i + k], device_id=to, device_id_type=MESH)

        mine = [pltpu.make_async_copy(ins[i], slot(i, *me), local_sems.at[i]) for i in range(n)]
        sent = []
        for i in range(n):
            sent += [copy(i, 1, me, (*xn, c), src=ins[i]), copy(i, 2, me, (*yn, c), src=ins[i])]
        sent += [copy(i, 0, me, sibling, src=ins[i]) for i in range(n)]
        for cp in mine + sent:
            cp.start()

        def after(i, k_in, blk, h_in, forwards):
            copy(i, k_in, blk, me, h_in).wait_recv()
            for k_out, to, h_out in forwards:
                sent.append(copy(i, k_out, blk, to, h_out))
                sent[-1].start()

        for i in range(n):
            after(i, 1, (*xn, c), None, [(3, (*yn, c), 0), (5, sibling, None)])
        for i in range(n):
            after(i, 2, (*yn, c), None, [(4, (*xn, c), 1), (6, sibling, None)])
        for i in range(n):
            after(i, 3, (*dg, c), 0, [(7, sibling, 0)])
        for i in range(n):
            after(i, 4, (*dg, c), 1, [(8, sibling, 1)])
        for i in range(n):
            copy(i, 0, sibling, me).wait_recv()
            copy(i, 5, (*xn, other), me).wait_recv()
            copy(i, 6, (*yn, other), me).wait_recv()
            copy(i, 7, (*dg, other), me, 0).wait_recv()
            copy(i, 8, (*dg, other), me, 1).wait_recv()
        for cp in sent:
            cp.wait_send()
        for cp in mine:
            cp.wait()

    return pl.pallas_call(
        body, out_shape=[SDS((N_DEV,) + b.shape, b.dtype) for b in blocks], in_specs=_hbm_specs(n), out_specs=_hbm_specs(n),
        scratch_shapes=[pltpu.SemaphoreType.DMA((9 * n,)), pltpu.SemaphoreType.DMA((9 * n,)), pltpu.SemaphoreType.DMA((n,))],
        name=name)(*blocks)


def _sibling_exchange(parts, name):
    parts = tuple(parts)
    n = len(parts)

    def body(*refs):
        ins, outs = refs[:n], refs[n:2 * n]
        send_sems, recv_sems = refs[2 * n:]
        x, y, c = lax.axis_index("x"), lax.axis_index("y"), lax.axis_index("c")
        copies = [pltpu.make_async_remote_copy(
            src_ref=ins[i].at[q, 1 - c], dst_ref=outs[i].at[q], send_sem=send_sems.at[4 * i + q],
            recv_sem=recv_sems.at[4 * i + q], device_id=(x, y, 1 - c), device_id_type=MESH)
            for i in range(n) for q in range(4)]
        for cp in copies:
            cp.start()
        for cp in copies:
            cp.wait_recv()
        for cp in copies:
            cp.wait_send()

    return pl.pallas_call(
        body, out_shape=[SDS((4,) + p.shape[2:], p.dtype) for p in parts], in_specs=_hbm_specs(n), out_specs=_hbm_specs(n),
        scratch_shapes=[pltpu.SemaphoreType.DMA((4 * n,)), pltpu.SemaphoreType.DMA((4 * n,))], name=name)(*parts)


def _neighbour_exchange(bufs, k, name):
    bufs = tuple(bufs)
    n = len(bufs)

    def body(*refs):
        ins, outs = refs[:n], refs[n:2 * n]
        send_sems, recv_sems = refs[2 * n:]
        x, y, c = lax.axis_index("x"), lax.axis_index("y"), lax.axis_index("c")
        copies = [pltpu.make_async_remote_copy(
            src_ref=ins[i].at[pl.ds(d * k, k)], dst_ref=outs[i].at[pl.ds(d * k, k)], send_sem=send_sems.at[2 * i + d],
            recv_sem=recv_sems.at[2 * i + d], device_id=to, device_id_type=MESH)
            for d, to in enumerate([(1 - x, y, c), (x, 1 - y, c)]) for i in range(n)]
        for cp in copies:
            cp.start()
        for cp in copies:
            cp.wait_recv()
        for cp in copies:
            cp.wait_send()

    return pl.pallas_call(
        body, out_shape=[SDS((2 * k,) + b.shape[1:], b.dtype) for b in bufs], in_specs=_hbm_specs(n), out_specs=_hbm_specs(n),
        scratch_shapes=[pltpu.SemaphoreType.DMA((2 * n,)), pltpu.SemaphoreType.DMA((2 * n,))], name=name)(*bufs)


_FIRST_HOP_SLOT = ((0, 5), (4, 2), (1, 3))


def _place_ids():
    x, y, c = lax.axis_index("x"), lax.axis_index("y"), lax.axis_index("c")
    place = [c, 2 * x + y, 2 * (1 - x) + y, 2 * x + (1 - y), 2 * (1 - x) + (1 - y)]
    return jnp.stack(place + [jnp.int32(s) for pair in _FIRST_HOP_SLOT for s in pair]).astype(jnp.int32)


def _row_tile(rows, cols):
    best = None
    for t in range(16, rows + 1, 16):
        if rows % t == 0 and t * cols <= 256 * 1024:
            best = t
    return best or rows


def _chip_partials(part, recv_a, ids, dtype, name):
    _, _, R, C = part.shape
    tr = _row_tile(R // 2, C)
    nt = R // 2 // tr

    def body(ids_ref, p_ref, a_ref, o_ref):
        o_ref[...] = (p_ref[...] + a_ref[...]).astype(dtype)

    return pl.pallas_call(
        body, out_shape=SDS((6, R // 2, C), dtype),
        grid_spec=pltpu.PrefetchScalarGridSpec(
            num_scalar_prefetch=1, grid=(3, 2, nt),
            in_specs=[pl.BlockSpec((None, None, tr, C), lambda s, h, i, ids: (ids[2 + s], ids[0], h * nt + i, 0)),
                      pl.BlockSpec((None, tr, C), lambda s, h, i, ids: (ids[2 + s], h * nt + i, 0))],
            out_specs=pl.BlockSpec((None, tr, C), lambda s, h, i, ids: (ids[5 + 2 * s + h], i, 0))),
        name=name, compiler_params=_cparams(("parallel", "parallel", "parallel")))(ids, part, recv_a)


def _second_hop(first, recv1, name):
    _, R2, C = first.shape
    tr = _row_tile(R2, C)

    def body(f_ref, r_ref, o_ref):
        o_ref[...] = (f_ref[...].astype(F32) + r_ref[...].astype(F32)).astype(o_ref.dtype)

    return pl.pallas_call(
        body, out_shape=SDS((2, R2, C), first.dtype), grid=(2, R2 // tr),
        in_specs=[pl.BlockSpec((None, tr, C), lambda d, i: (5 - d, i, 0)),
                  pl.BlockSpec((None, tr, C), lambda d, i: (3 - 2 * d, i, 0))],
        out_specs=pl.BlockSpec((None, tr, C), lambda d, i: (d, i, 0)),
        name=name, compiler_params=_cparams(("parallel", "parallel")))(first, recv1)


def _adamw(part, recv_a, recv1, recv2, ids, w, m, v, name):
    R, C = w.shape
    tr = _row_tile(R // 2, C)
    nt = R // 2 // tr
    c1 = 1.0 / (1.0 - ADAM_B1 ** ADAM_STEP)
    c2 = 1.0 / (1.0 - ADAM_B2 ** ADAM_STEP)

    def body(ids_ref, p_ref, a_ref, x1_ref, y1_ref, x2_ref, y2_ref, w_ref, m_ref, v_ref, g_ref, d_ref, nm_ref, nv_ref):
        first = pl.program_id(0) == 0
        from_x = jnp.where(first, x1_ref[...], x2_ref[...]).astype(F32)
        from_y = jnp.where(first, y2_ref[...], y1_ref[...]).astype(F32)
        g = p_ref[...] + a_ref[...] + from_x + from_y
        nm = ADAM_B1 * m_ref[...] + (1.0 - ADAM_B1) * g
        nv = ADAM_B2 * v_ref[...] + (1.0 - ADAM_B2) * jnp.square(g)
        g_ref[...] = g
        nm_ref[...] = nm
        nv_ref[...] = nv
        d_ref[...] = -ADAM_LR * ((nm * c1) / (jnp.sqrt(nv * c2) + ADAM_EPS) + ADAM_WD * w_ref[...])

    spec = pl.BlockSpec((tr, C), lambda h, i, ids: (h * nt + i, 0))
    half = lambda slot: pl.BlockSpec((None, tr, C), lambda h, i, ids: (slot, i, 0))
    return pl.pallas_call(
        body, out_shape=[SDS((R, C), F32)] * 4,
        grid_spec=pltpu.PrefetchScalarGridSpec(
            num_scalar_prefetch=1, grid=(2, nt),
            in_specs=[pl.BlockSpec((None, None, tr, C), lambda h, i, ids: (ids[1], ids[0], h * nt + i, 0)),
                      pl.BlockSpec((None, tr, C), lambda h, i, ids: (ids[1], h * nt + i, 0)),
                      half(0), half(2), half(0), half(1), spec, spec, spec],
            out_specs=[spec] * 4),
        name=name, compiler_params=_cparams(("parallel", "parallel")))(ids, part, recv_a, recv1, recv1, recv2, recv2, w, m, v)


def _to_flat(vecs, quantum):
    flat = jnp.concatenate([v.reshape(-1) for v in vecs])
    n = _rup(flat.shape[0], quantum * FLAT_COLS)
    return jnp.pad(flat, (0, n - flat.shape[0])).reshape(n // FLAT_COLS, FLAT_COLS)


def _gathered_to_full(g, names, blocks):
    flat = g.reshape(N_DEV, -1)
    out, pos = {}, 0
    for n in names:
        shp = blocks[n]
        size = math.prod(shp)
        out[n] = _blocks_to_full(flat[:, pos:pos + size].reshape((N_DEV,) + shp), SHARD_AXIS[n])
        pos += size
    return out


def _blocks_to_full(g, ax):
    shp = g.shape[1:]
    return jnp.moveaxis(g, 0, ax).reshape(shp[:ax] + (N_DEV * shp[ax],) + shp[ax + 1:])


def _full_to_blocks(g, ax, ndev=N_DEV):
    shp = g.shape
    t = g.reshape(shp[:ax] + (ndev, shp[ax] // ndev) + shp[ax + 1:])
    return jnp.moveaxis(t, ax, 0)


def kernel(x, p, ln1_g, ln1_b, ln2_g, ln2_b, ffn_up, ffn_conv_w, ffn_conv_b, ffn_down, ple_proj, ple_norm_g, ple_gate_w, ple_gate_b, even_w_in, even_w_out, rwkv_mu, rwkv_w0, rwkv_w2, rwkv_a0, rwkv_a2, rwkv_g2, rwkv_k_k, rwkv_k_a, rwkv_r_k, rwkv_gn_g, rwkv_gn_b, gdn_conv_w, gdn_A_log, gdn_dt_bias, gdn_norm_g, odd_w_in, odd_w_out, mamba_conv_w, mamba_conv_b, mamba_dt_bias, mamba_A_log, mamba_D, mamba_norm_g, lru_conv_w, lru_conv_b, lru_wa, lru_ba, lru_wx, lru_bx, lru_lambda, loss_target, m_ln1_g, m_ln1_b, m_ln2_g, m_ln2_b, m_ffn_up, m_ffn_conv_w, m_ffn_conv_b, m_ffn_down, m_ple_proj, m_ple_norm_g, m_ple_gate_w, m_ple_gate_b, m_even_w_in, m_even_w_out, m_rwkv_mu, m_rwkv_w0, m_rwkv_w2, m_rwkv_a0, m_rwkv_a2, m_rwkv_g2, m_rwkv_k_k, m_rwkv_k_a, m_rwkv_r_k, m_rwkv_gn_g, m_rwkv_gn_b, m_gdn_conv_w, m_gdn_A_log, m_gdn_dt_bias, m_gdn_norm_g, m_odd_w_in, m_odd_w_out, m_mamba_conv_w, m_mamba_conv_b, m_mamba_dt_bias, m_mamba_A_log, m_mamba_D, m_mamba_norm_g, m_lru_conv_w, m_lru_conv_b, m_lru_wa, m_lru_ba, m_lru_wx, m_lru_bx, m_lru_lambda, v_ln1_g, v_ln1_b, v_ln2_g, v_ln2_b, v_ffn_up, v_ffn_conv_w, v_ffn_conv_b, v_ffn_down, v_ple_proj, v_ple_norm_g, v_ple_gate_w, v_ple_gate_b, v_even_w_in, v_even_w_out, v_rwkv_mu, v_rwkv_w0, v_rwkv_w2, v_rwkv_a0, v_rwkv_a2, v_rwkv_g2, v_rwkv_k_k, v_rwkv_k_a, v_rwkv_r_k, v_rwkv_gn_g, v_rwkv_gn_b, v_gdn_conv_w, v_gdn_A_log, v_gdn_dt_bias, v_gdn_norm_g, v_odd_w_in, v_odd_w_out, v_mamba_conv_w, v_mamba_conv_b, v_mamba_dt_bias, v_mamba_A_log, v_mamba_D, v_mamba_norm_g, v_lru_conv_w, v_lru_conv_b, v_lru_wa, v_lru_ba, v_lru_wx, v_lru_bx, v_lru_lambda):
    args = locals()
    w = {n: args[n] for n in WEIGHTS}
    m = {n: args["m_" + n] for n in WEIGHTS}
    v = {n: args["v_" + n] for n in WEIGHTS}
    depth = ln1_g.shape[0]
    me = _my_index()
    blocks = {n: w[n].shape for n in WEIGHTS}

    as_rows = lambda t: t.reshape(-1, t.shape[-1])
    small = _to_flat([w[n] for n in SMALL_SHARDED], 16)
    gathered = _all_gather([as_rows(w[n].astype(BF16)) for n in MATRICES] + [small], "gather_params")
    w16 = {n: _blocks_to_full(g.reshape((N_DEV,) + blocks[n]), SHARD_AXIS[n]) for n, g in zip(MATRICES, gathered)}
    sp = _gathered_to_full(gathered[-1], SMALL_SHARDED, blocks)
    sp.update({n: w[n] for n in REPLICATED})

    lay = _matrix_layouts(w16, sp)
    wz = {k: jnp.zeros(shape, F32) for k, shape in lay['padded'].items()}
    y, vjp = jax.vjp(lambda x_, wz_, sp_: _forward(x_, wz_, sp_, p[:, 0], w16, depth), x[0], wz, sp)
    loss_local, dy = _loss_head(y, loss_target[0])
    dx, dwz, dsp = vjp(dy)
    loss = lax.psum(loss_local, ("x", "y", "c"))
    gfull = dict(dsp)
    gblocks = {}
    for n in MATRICES:
        layers = range(w16[n].shape[0])
        ax = SHARD_AXIS[n]
        if n == 'ffn_up':
            gblocks[n] = jnp.concatenate(
                [jnp.concatenate([_full_to_blocks(dwz[f"{k}.{j}"][None], ax, N_DEV // 2) for j in layers], axis=1)
                 for k in ("ffn_up", "ffn_up_val")], axis=0)
        else:
            gblocks[n] = jnp.concatenate(
                [_full_to_blocks(lay['unpad'][n](dwz[f"{n}.{j}"])[None], ax) for j in layers], axis=1)

    rep_flat = jnp.concatenate([gfull[n].reshape(-1) for n in REPLICATED])
    rep_n = rep_flat.shape[0]
    piece = _rup(rep_n, N_DEV * LANES) // N_DEV
    rep_pad = lambda t: jnp.pad(t, (0, N_DEV * piece - rep_n))
    small_parts = jnp.concatenate([_full_to_blocks(gfull[n], SHARD_AXIS[n]).reshape(N_DEV, -1) for n in SMALL_SHARDED]
                                  + [rep_pad(rep_flat).reshape(N_DEV, piece)], axis=1)
    n_flat = small_parts.shape[1]
    n_pad = _rup(n_flat, 16 * FLAT_COLS)
    small_parts = jnp.pad(small_parts, ((0, 0), (0, n_pad - n_flat)))

    def my_small(d):
        rep = rep_pad(jnp.concatenate([d[n].reshape(-1) for n in REPLICATED]))
        mine = lax.dynamic_slice(rep, (me * piece,), (piece,))
        flat = jnp.concatenate([d[n].reshape(-1) for n in SMALL_SHARDED] + [mine])
        return jnp.pad(flat, (0, n_pad - n_flat)).reshape(n_pad // FLAT_COLS, FLAT_COLS)

    by_chip = lambda t, cols: t.reshape(4, 2, -1, cols)
    parts = [by_chip(gblocks[n], blocks[n][-1]) for n in MATRICES]
    parts.append(by_chip(small_parts, FLAT_COLS))
    wire = [BF16] * len(MATRICES) + [F32]
    tags = MATRICES + ["small"]
    ids = _place_ids()
    recv_a = _sibling_exchange(parts, "reduce_sibling")
    first = [_chip_partials(pt, ra, ids, dt, f"chip_partials_{t}") for pt, ra, dt, t in zip(parts, recv_a, wire, tags)]
    recv1 = _neighbour_exchange(first, 2, "reduce_first_hop")
    recv2 = _neighbour_exchange([_second_hop(f, r1, f"second_hop_{t}") for f, r1, t in zip(first, recv1, tags)],
                                1, "reduce_second_hop")
    mine = [(as_rows(w[n]), as_rows(m[n]), as_rows(v[n])) for n in MATRICES] + [(my_small(w), my_small(m), my_small(v))]
    results = [_adamw(pt, ra, r1, r2, ids, *wmv, f"adamw_{t}")
               for pt, ra, r1, r2, wmv, t in zip(parts, recv_a, recv1, recv2, mine, tags)]
    small_res = [r.reshape(-1) for r in results[-1]]
    rep_res = jnp.stack([r[n_flat - piece:n_flat] for r in small_res])
    rep_rows = _rup(4 * piece, 8 * FLAT_COLS) // FLAT_COLS
    rep_blk = jnp.pad(rep_res.reshape(-1), (0, rep_rows * FLAT_COLS - 4 * piece)).reshape(rep_rows, FLAT_COLS)
    (rep_all,) = _all_gather([rep_blk], "gather_replicated")
    rep_all = rep_all.reshape(N_DEV, -1)[:, :4 * piece]
    rep_all = jnp.transpose(rep_all.reshape(N_DEV, 4, piece), (1, 0, 2)).reshape(4, N_DEV * piece)

    outs = [{}, {}, {}, {}]
    for k in range(4):
        for n, res in zip(MATRICES, results):
            outs[k][n] = res[k].reshape(blocks[n])
        pos = 0
        for n in SMALL_SHARDED:
            size = math.prod(blocks[n])
            outs[k][n] = small_res[k][pos:pos + size].reshape(blocks[n])
            pos += size
        pos = 0
        for n in REPLICATED:
            size = math.prod(blocks[n])
            outs[k][n] = rep_all[k, pos:pos + size].reshape(blocks[n])
            pos += size
    return (loss, dx[None], *[outs[0][n] for n in WEIGHTS], *[outs[1][n] for n in WEIGHTS],
            *[outs[2][n] for n in WEIGHTS], *[outs[3][n] for n in WEIGHTS])


def _matrix_layouts(w16, sp):
    padded, unpad = {}, {}
    ident = lambda g: g
    for n in ('ffn_down', 'ple_proj', 'ple_gate_w', 'even_w_out', 'odd_w_out'):
        for j in range(w16[n].shape[0]):
            padded[f"{n}.{j}"] = w16[n].shape[1:]
        unpad[n] = ident
    for j in range(w16['ffn_up'].shape[0]):
        half = (w16['ffn_up'].shape[1], w16['ffn_up'].shape[2] // 2)
        padded[f"ffn_up.{j}"] = padded[f"ffn_up_val.{j}"] = half
    for n in ('rwkv_w2', 'rwkv_a2', 'rwkv_g2'):
        rows, cols = w16[n].shape[1:]
        for j in range(w16[n].shape[0]):
            padded[f"{n}.{j}"] = (_rup(rows, LANES), cols)
        unpad[n] = functools.partial(lambda g, rows: g[:rows], rows=rows)
    ah, an = sp['rwkv_r_k'].shape[1:]
    bh, bn = sp['gdn_A_log'].shape[1], sp['gdn_norm_g'].shape[1]
    ew = [ah * an] * 3 + [w16['rwkv_w2'].shape[1], w16['rwkv_a2'].shape[1], w16['rwkv_g2'].shape[1]] + [bh * bn] * 4 + [bh, bh]
    cwid, ch = sp['mamba_norm_g'].shape[1], sp['mamba_dt_bias'].shape[1]
    dw = sp['lru_lambda'].shape[1]
    ow = [cwid, sp['mamba_conv_w'].shape[2], ch, dw, dw]
    for n, widths in (('even_w_in', ew), ('odd_w_in', ow)):
        offs, total = _aligned_layout(widths)
        for j in range(w16[n].shape[0]):
            padded[f"{n}.{j}"] = (w16[n].shape[1], total)
        unpad[n] = functools.partial(_unpad_cols, offs=offs, widths=widths)
    return {'padded': padded, 'unpad': unpad}
```

```python
import functools
import math

import jax
import jax.numpy as jnp
from jax import lax
from jax.experimental import pallas as pl
from jax.experimental.pallas import tpu as pltpu

F32 = jnp.float32
BF16 = jnp.bfloat16
HI = lax.Precision.HIGHEST
SDS = jax.ShapeDtypeStruct
MESH = pl.DeviceIdType.MESH

LANES = 128
VMEM_LIMIT = 56 * 1024 * 1024
N_DEV = 8
FLAT_COLS = 1024
MM_VMEM = 40 * 1024 * 1024

LN_EPS = 1e-5
RMS_EPS = 1e-6
L2_EPS = 1e-6
A_GN_EPS = 64e-5
LRU_C = 8.0
C_GROUPS = 4
RWKV_CHUNK = 64
GDN_CHUNK = 64
SSD_CHUNK = 128
SCAN_HEADS = 32

ADAM_LR, ADAM_B1, ADAM_B2, ADAM_EPS, ADAM_WD, ADAM_STEP = 0.001, 0.9, 0.999, 1e-08, 0.01, 10

WEIGHTS = ['ln1_g', 'ln1_b', 'ln2_g', 'ln2_b', 'ffn_up', 'ffn_conv_w', 'ffn_conv_b', 'ffn_down', 'ple_proj',
           'ple_norm_g', 'ple_gate_w', 'ple_gate_b', 'even_w_in', 'even_w_out', 'rwkv_mu', 'rwkv_w0', 'rwkv_w2',
           'rwkv_a0', 'rwkv_a2', 'rwkv_g2', 'rwkv_k_k', 'rwkv_k_a', 'rwkv_r_k', 'rwkv_gn_g', 'rwkv_gn_b',
           'gdn_conv_w', 'gdn_A_log', 'gdn_dt_bias', 'gdn_norm_g', 'odd_w_in', 'odd_w_out', 'mamba_conv_w',
           'mamba_conv_b', 'mamba_dt_bias', 'mamba_A_log', 'mamba_D', 'mamba_norm_g', 'lru_conv_w', 'lru_conv_b',
           'lru_wa', 'lru_ba', 'lru_wx', 'lru_bx', 'lru_lambda']
SHARD_AXIS = {'ffn_up': 2, 'ffn_conv_w': 2, 'ffn_down': 1, 'ple_proj': 2, 'ple_gate_w': 1, 'even_w_in': 2,
              'even_w_out': 1, 'rwkv_w2': 2, 'rwkv_a2': 2, 'rwkv_g2': 2, 'gdn_conv_w': 2, 'odd_w_in': 2,
              'odd_w_out': 1, 'mamba_conv_w': 2, 'mamba_conv_b': 1, 'mamba_norm_g': 1, 'lru_conv_w': 2,
              'lru_conv_b': 1, 'lru_ba': 1, 'lru_bx': 1, 'lru_lambda': 1}
MATRICES = ['ffn_up', 'ffn_down', 'ple_proj', 'ple_gate_w', 'even_w_in', 'even_w_out', 'rwkv_w2', 'rwkv_a2',
            'rwkv_g2', 'odd_w_in', 'odd_w_out']
SMALL_SHARDED = [n for n in WEIGHTS if n in SHARD_AXIS and n not in MATRICES]
REPLICATED = [n for n in WEIGHTS if n not in SHARD_AXIS]


def _cparams(sem):
    return pltpu.CompilerParams(dimension_semantics=sem, vmem_limit_bytes=VMEM_LIMIT)


def _rup(n, m):
    return -(-n // m) * m


def _pick(n, cands):
    for c in cands:
        if n % c == 0:
            return c
    return n


_DIMS = {'nn': (((1,), (0,)), ((), ())), 'nt': (((1,), (1,)), ((), ())), 'tn': (((0,), (0,)), ((), ()))}


def _mm(a, b, mode, name):
    if mode == 'tn':
        K, M = a.shape
    else:
        M, K = a.shape
    N = b.shape[0] if mode == 'nt' else b.shape[1]
    tm = _pick(M, (1024, 512, 256, 128))
    tn = _pick(N, (1024, 512, 256, 128))
    room = MM_VMEM - 3 * tm * tn * 4
    per_k = 2 * (tm * a.dtype.itemsize + tn * b.dtype.itemsize)
    tk = max([t for t in range(LANES, K + 1, LANES) if K % t == 0 and t * per_k <= room] or [K])
    nk = K // tk

    def body(a_ref, b_ref, o_ref, acc_ref):
        k = pl.program_id(2)
        part = lax.dot_general(a_ref[...].astype(BF16), b_ref[...].astype(BF16), _DIMS[mode],
                               preferred_element_type=F32)

        @pl.when(k == 0)
        def _():
            acc_ref[...] = part

        @pl.when(k > 0)
        def _():
            acc_ref[...] += part

        @pl.when(k == nk - 1)
        def _():
            o_ref[...] = acc_ref[...]

    a_spec = pl.BlockSpec((tk, tm), lambda i, j, k: (k, i)) if mode == 'tn' else pl.BlockSpec((tm, tk), lambda i, j, k: (i, k))
    b_spec = pl.BlockSpec((tn, tk), lambda i, j, k: (j, k)) if mode == 'nt' else pl.BlockSpec((tk, tn), lambda i, j, k: (k, j))
    return pl.pallas_call(
        body, grid=(M // tm, N // tn, nk), in_specs=[a_spec, b_spec],
        out_specs=pl.BlockSpec((tm, tn), lambda i, j, k: (i, j)), out_shape=SDS((M, N), F32),
        scratch_shapes=[pltpu.VMEM((tm, tn), F32)], name=name,
        compiler_params=_cparams(("parallel", "parallel", "arbitrary")))(a, b)


def _matmul(x, w16, wz, name):
    @jax.custom_vjp
    def op(x, wz):
        return _mm(x.astype(BF16), w16, 'nn', name + "_f")

    def op_f(x, wz):
        x16 = x.astype(BF16)
        return _mm(x16, w16, 'nn', name + "_f"), x16

    def op_b(x16, g):
        return _mm(g, w16, 'nt', name + "_dx"), _mm(x16, g, 'tn', name + "_dw")

    op.defvjp(op_f, op_b)
    return op(x, wz)


def _block_diag_pair(x, wa, wx, name):
    T = x.shape[0]
    nb, bd, _ = wa.shape
    dot = lambda a, b, mode: lax.dot_general(a.astype(BF16), b.astype(BF16), _DIMS[mode], preferred_element_type=F32)
    cols = pl.BlockSpec((T, bd), lambda n: (0, n))
    blk = pl.BlockSpec((1, bd, bd), lambda n: (n, 0, 0))

    def fwd_call(x, wa, wx):
        def body(x_ref, wa_ref, wx_ref, ra_ref, rx_ref):
            ra_ref[...] = dot(x_ref[...], wa_ref[0], 'nn')
            rx_ref[...] = dot(x_ref[...], wx_ref[0], 'nn')

        return pl.pallas_call(body, grid=(nb,), in_specs=[cols, blk, blk], out_specs=[cols, cols],
                              out_shape=[SDS(x.shape, F32)] * 2, name=name + "_f",
                              compiler_params=_cparams(("parallel",)))(x, wa, wx)

    def bwd_call(x, wa, wx, ga, gx):
        def body(x_ref, wa_ref, wx_ref, ga_ref, gx_ref, dx_ref, dwa_ref, dwx_ref):
            dx_ref[...] = dot(ga_ref[...], wa_ref[0], 'nt') + dot(gx_ref[...], wx_ref[0], 'nt')
            dwa_ref[0] = dot(x_ref[...], ga_ref[...], 'tn')
            dwx_ref[0] = dot(x_ref[...], gx_ref[...], 'tn')

        return pl.pallas_call(body, grid=(nb,), in_specs=[cols, blk, blk, cols, cols], out_specs=[cols, blk, blk],
                              out_shape=[SDS(x.shape, F32), SDS(wa.shape, F32), SDS(wx.shape, F32)], name=name + "_b",
                              compiler_params=_cparams(("parallel",)))(x, wa, wx, ga, gx)

    @jax.custom_vjp
    def op(x, wa, wx):
        return tuple(fwd_call(x, wa, wx))

    def op_f(x, wa, wx):
        return op(x, wa, wx), (x, wa, wx)

    def op_b(res, g):
        return tuple(bwd_call(*res, *g))

    op.defvjp(op_f, op_b)
    return op(x, wa, wx)


def _tile_op(name, fn, arrs, params, consts, by_rows, width=LANES):
    arrs, params, consts = tuple(arrs), tuple(params), tuple(consts)
    na, npar, nc = len(arrs), len(params), len(consts)
    T = arrs[0].shape[0]
    if by_rows:
        tile = _pick(T, (256, 128, 64, 32, 16, 8))
        grid = (T // tile,)
        arr_block = lambda a: (tile, a.shape[1])
        arr_spec = lambda a: pl.BlockSpec((tile, a.shape[1]), lambda i: (i, 0))
        par_block = lambda p: p.shape
        par_spec = lambda p: pl.BlockSpec(p.shape, lambda i: (0, 0))
    else:
        grid = (arrs[0].shape[1] // width,)
        arr_block = lambda a: (T, width)
        arr_spec = lambda a: pl.BlockSpec((T, width), lambda i: (0, i))
        par_block = lambda p: (p.shape[0], width)
        par_spec = lambda p: pl.BlockSpec((p.shape[0], width), lambda i: (0, i))
    const_spec = lambda c: pl.BlockSpec(c.shape, lambda i: (0,) * c.ndim)
    outs_sds = jax.eval_shape(fn, *[SDS(arr_block(a), F32) for a in arrs], *[SDS(par_block(p), F32) for p in params],
                              *[SDS(c.shape, c.dtype) for c in consts])
    out_widths = [o.shape[1] for o in outs_sds]
    nout = len(out_widths)
    if by_rows:
        out_shapes = [SDS((T, w), F32) for w in out_widths]
        out_specs = [pl.BlockSpec((tile, w), lambda i: (i, 0)) for w in out_widths]
    else:
        out_shapes = [SDS((T, grid[0] * w), F32) for w in out_widths]
        out_specs = [pl.BlockSpec((T, w), lambda i: (0, i)) for w in out_widths]

    def fwd_call(arrs, params):
        def body(*refs):
            outs = fn(*[r[...] for r in refs[:na + npar + nc]])
            for o_ref, o in zip(refs[na + npar + nc:], outs):
                o_ref[...] = o

        return pl.pallas_call(
            body, grid=grid, in_specs=[arr_spec(a) for a in arrs] + [par_spec(p) for p in params] + [const_spec(c) for c in consts],
            out_specs=out_specs, out_shape=out_shapes, name=name + "_f",
            compiler_params=_cparams(("parallel",)))(*arrs, *params, *consts)

    def bwd_call(arrs, params, cts):
        def body(*refs):
            ins = refs[:na + npar + nc + nout]
            outs = refs[na + npar + nc + nout:]
            av = [r[...] for r in ins[:na]]
            pv = [r[...] for r in ins[na:na + npar]]
            cv = [r[...] for r in ins[na + npar:na + npar + nc]]
            gv = [r[...] for r in ins[na + npar + nc:]]
            _, vjp = jax.vjp(lambda *t: fn(*t, *cv), *av, *pv)
            grads = vjp(tuple(gv))
            for o_ref, g in zip(outs[:na], grads[:na]):
                o_ref[...] = g
            if by_rows and npar:
                @pl.when(pl.program_id(0) == 0)
                def _():
                    for o_ref in outs[na:]:
                        o_ref[...] = jnp.zeros_like(o_ref)

                for o_ref, g in zip(outs[na:], grads[na:]):
                    o_ref[...] += g
            else:
                for o_ref, g in zip(outs[na:], grads[na:]):
                    o_ref[...] = g

        return pl.pallas_call(
            body, grid=grid,
            in_specs=[arr_spec(a) for a in arrs] + [par_spec(p) for p in params] + [const_spec(c) for c in consts] + out_specs,
            out_specs=[arr_spec(a) for a in arrs] + [par_spec(p) for p in params],
            out_shape=[SDS(a.shape, F32) for a in arrs] + [SDS(p.shape, F32) for p in params], name=name + "_b",
            compiler_params=_cparams(("arbitrary",) if by_rows else ("parallel",)))(*arrs, *params, *consts, *cts)

    @jax.custom_vjp
    def op(arrs, params):
        return tuple(fwd_call(arrs, params))

    def op_f(arrs, params):
        return op(arrs, params), (arrs, params)

    def op_b(res, cts):
        arrs, params = res
        g = bwd_call(arrs, params, cts)
        return tuple(g[:na]), tuple(g[na:])

    op.defvjp(op_f, op_b)
    return op(arrs, params)


def _rowwise(name, fn, arrs, params=(), consts=()):
    return _tile_op(name, fn, arrs, params, consts, True)


def _colwise(name, fn, arrs, params=(), width=LANES):
    return _tile_op(name, fn, arrs, params, (), False, width)


@functools.partial(jax.custom_vjp, nondiff_argnums=(1,))
def _shift(x, k):
    rows = lax.broadcasted_iota(jnp.int32, x.shape, 0)
    return jnp.where(rows >= k, pltpu.roll(x, k, 0), 0.0)


def _shift_f(x, k):
    return _shift(x, k), None


def _shift_b(k, _, g):
    n = g.shape[0]
    rows = lax.broadcasted_iota(jnp.int32, g.shape, 0)
    return (jnp.where(rows < n - k, pltpu.roll(g, n - k, 0), 0.0),)


_shift.defvjp(_shift_f, _shift_b)


def _causal_conv(x, w):
    K = w.shape[0]
    y = x * w[K - 1:K, :]
    for j in range(K - 1):
        y = y + _shift(x, K - 1 - j) * w[j:j + 1, :]
    return y


def _silu(x):
    return x * jax.nn.sigmoid(x)


def _softplus(x):
    return jnp.maximum(x, 0.0) + jnp.log1p(jnp.exp(-jnp.abs(x)))


def _split_cols(h, offs, widths):
    @jax.custom_vjp
    def op(h):
        return tuple(h[:, o:o + w] for o, w in zip(offs, widths))

    def op_f(h):
        return op(h), None

    def op_b(_, cts):
        parts, pos = [], 0
        T = cts[0].shape[0]
        for o, w, c in zip(offs, widths, cts):
            if o > pos:
                parts.append(jnp.zeros((T, o - pos), F32))
            parts.append(c)
            pos = o + w
        if pos < h.shape[1]:
            parts.append(jnp.zeros((T, h.shape[1] - pos), F32))
        return (jnp.concatenate(parts, axis=1),)

    op.defvjp(op_f, op_b)
    return op(h)


def _group_ones(width, group):
    g = jnp.arange(width) // group
    return (g[:, None] == g[None, :]).astype(F32)


def _layer_norm_rows(x, g, b, eps):
    mu = jnp.mean(x, axis=1, keepdims=True)
    var = jnp.mean(jnp.square(x - mu), axis=1, keepdims=True)
    return (x - mu) * lax.rsqrt(var + eps) * g + b


def _tri(L, strict=False):
    i = lax.broadcasted_iota(jnp.int32, (L, L), 0)
    j = lax.broadcasted_iota(jnp.int32, (L, L), 1)
    return (i > j) if strict else (i >= j)


def _cumsum_rows(x):
    H, L, _ = x.shape
    tri = jnp.broadcast_to(_tri(L).astype(F32)[None], (H, L, L))
    return jnp.einsum('hls,hsn->hln', tri, x, precision=HI)


def _col_to_row(c):
    L = c.shape[1]
    return jnp.sum(c * _tri_eye(L)[None], axis=1, keepdims=True)


def _row_to_col(r):
    N = r.shape[2]
    return jnp.sum(r * _tri_eye(N)[None], axis=2, keepdims=True)


def _scalar_col(t):
    return _row_to_col(t.reshape(t.shape[0], 1, t.shape[3]))


def _tri_eye(L):
    i = lax.broadcasted_iota(jnp.int32, (L, L), 0)
    j = lax.broadcasted_iota(jnp.int32, (L, L), 1)
    return (i == j).astype(F32)


def _unit_lower_inverse(n_strict):
    L = n_strict.shape[1]
    inv = _tri_eye(L)[None] + n_strict
    x = n_strict
    p = 2
    while p < L:
        x = jnp.einsum('hij,hjk->hik', x, x)
        inv = inv + jnp.einsum('hij,hjk->hik', inv, x)
        p *= 2
    return inv


def _rwkv_chunk(r, lw, k, v, a, b, h0):
    L = r.shape[1]
    mm = jnp.einsum
    cum = _cumsum_rows(lw)
    cum_l = jnp.sum(lw, axis=1, keepdims=True)
    e_neg = jnp.exp(-cum)
    rt, bt, kt, at = r * jnp.exp(cum), b * e_neg, k * e_neg, a * jnp.exp(cum - lw)
    to_end = jnp.exp(cum_l - cum)
    strict, incl = _tri(L, True)[None], _tri(L)[None]
    n = jnp.where(strict, mm('hld,hsd->hls', at, bt), 0.0)
    mk = jnp.where(strict, mm('hld,hsd->hls', at, kt), 0.0)
    u = mm('hls,hsv->hlv', _unit_lower_inverse(n), mm('hld,hdv->hlv', at, h0) + mm('hls,hsv->hlv', mk, v))
    y = (mm('hld,hdv->hlv', rt, h0) + mm('hls,hsv->hlv', jnp.where(incl, mm('hld,hsd->hls', rt, bt), 0.0), u)
         + mm('hls,hsv->hlv', jnp.where(incl, mm('hld,hsd->hls', rt, kt), 0.0), v))
    h1 = (_row_to_col(jnp.exp(cum_l)) * h0 + mm('hld,hlv->hdv', b * to_end, u) + mm('hld,hlv->hdv', k * to_end, v))
    return y, h1


def _gdn_chunk(q, k, v, beta, lg, h0):
    C, D = q.shape[1], q.shape[2]
    scale = D ** -0.5
    beta, lg = _scalar_col(beta), _scalar_col(lg)
    gc = _cumsum_rows(lg)
    gc_l = jnp.sum(lg, axis=1, keepdims=True)
    causal, strict = _tri(C)[None], _tri(C, True)[None]
    decay = jnp.exp(jnp.where(causal, gc - _col_to_row(gc), -jnp.inf))
    k_beta = k * beta
    m = jnp.where(strict, jnp.einsum('hcd,hsd->hcs', k_beta, k) * decay, 0.0)
    inv = _unit_lower_inverse(-m)
    e_gc = jnp.exp(gc)
    u = jnp.einsum('hcs,hsd->hcd', inv, v * beta)
    w = jnp.einsum('hcs,hsd->hcd', inv, k_beta * e_gc)
    attn = jnp.where(causal, jnp.einsum('hcd,hsd->hcs', q * scale, k) * decay, 0.0)
    v_new = u - jnp.einsum('hcd,hde->hce', w, h0)
    o = jnp.einsum('hcd,hde->hce', q * scale * e_gc, h0) + jnp.einsum('hcs,hse->hce', attn, v_new)
    h1 = h0 * jnp.exp(gc_l) + jnp.einsum('hcd,hce->hde', k * jnp.exp(gc_l - gc), v_new)
    return o, h1


def _ssd_chunk(xs, dt, aa, bm, cm, h0):
    H, L, _ = xs.shape
    G = bm.shape[0]
    per_head = lambda t: jnp.broadcast_to(t[:, None], (G, H // G) + t.shape[1:]).reshape((H,) + t.shape[1:])
    dt, aa = _scalar_col(dt), _scalar_col(aa)
    x = xs * dt
    cs = _cumsum_rows(aa)
    cs_l = jnp.sum(aa, axis=1, keepdims=True)
    causal = _tri(L)[None]
    cb = per_head(jnp.einsum('gln,gsn->gls', cm, bm))
    wd = jnp.where(causal, cb * jnp.exp(jnp.where(causal, cs - _col_to_row(cs), -jnp.inf)), 0.0)
    cmb, bmb = per_head(cm), per_head(bm)
    y = jnp.einsum('hls,hsp->hlp', wd, x) + jnp.einsum('hln,hnp->hlp', cmb, h0) * jnp.exp(cs)
    h1 = jnp.exp(cs_l) * h0 + jnp.einsum('hln,hlp->hnp', bmb, x * jnp.exp(cs_l - cs))
    return y, h1


def _chunk_scan(name, fn, seqs, hb, L, state_shape, out_width):
    seqs = tuple(seqs)
    ns = len(seqs)
    H = max(s.shape[0] for s in seqs)
    T = max(s.shape[1] for s in seqs)
    nc, nh = T // L, H // hb
    lead = [hb * s.shape[0] // H for s in seqs]
    st_block = (hb,) + state_shape

    def seq_spec(s, l, imap):
        if s.ndim == 4:
            return pl.BlockSpec((l, 1, 1, L), lambda h, c: imap(h, c) + (0,))
        return pl.BlockSpec((l, L, s.shape[2]), imap)

    fmap = lambda h, c: (h, c, 0)
    rmap = lambda h, c: (h, nc - 1 - c, 0)

    def fwd_call(seqs):
        def body(*refs):
            y_ref, st_ref, carry = refs[ns], refs[ns + 1], refs[ns + 2]

            @pl.when(pl.program_id(1) == 0)
            def _():
                carry[...] = jnp.zeros_like(carry)

            h0 = carry[...]
            st_ref[0] = h0
            y, h1 = fn(*[r[...] for r in refs[:ns]], h0)
            y_ref[...] = y
            carry[...] = h1

        return pl.pallas_call(
            body, grid=(nh, nc), in_specs=[seq_spec(s, l, fmap) for s, l in zip(seqs, lead)],
            out_specs=[pl.BlockSpec((hb, L, out_width), fmap),
                       pl.BlockSpec((1,) + st_block, lambda h, c: (c, h) + (0,) * len(state_shape))],
            out_shape=[SDS((H, T, out_width), F32), SDS((nc, H) + state_shape, F32)],
            scratch_shapes=[pltpu.VMEM(st_block, F32)], name=name + "_f",
            compiler_params=_cparams(("parallel", "arbitrary")))(*seqs)

    def bwd_call(seqs, states, dy):
        def body(*refs):
            st_ref, dy_ref = refs[ns], refs[ns + 1]
            outs, carry = refs[ns + 2:2 * ns + 2], refs[2 * ns + 2]

            @pl.when(pl.program_id(1) == 0)
            def _():
                carry[...] = jnp.zeros_like(carry)

            _, vjp = jax.vjp(fn, *[r[...] for r in refs[:ns]], st_ref[0])
            grads = vjp((dy_ref[...], carry[...]))
            for o_ref, g in zip(outs, grads[:ns]):
                o_ref[...] = g
            carry[...] = grads[ns]

        return pl.pallas_call(
            body, grid=(nh, nc),
            in_specs=[seq_spec(s, l, rmap) for s, l in zip(seqs, lead)]
            + [pl.BlockSpec((1,) + st_block, lambda h, c: (nc - 1 - c, h) + (0,) * len(state_shape)),
               pl.BlockSpec((hb, L, out_width), rmap)],
            out_specs=[seq_spec(s, l, rmap) for s, l in zip(seqs, lead)],
            out_shape=[SDS(s.shape, F32) for s in seqs],
            scratch_shapes=[pltpu.VMEM(st_block, F32)], name=name + "_b",
            compiler_params=_cparams(("parallel", "arbitrary")))(*seqs, states, dy)

    @jax.custom_vjp
    def op(seqs):
        return fwd_call(seqs)[0]

    def op_f(seqs):
        y, states = fwd_call(seqs)
        return y, (seqs, states)

    def op_b(res, dy):
        seqs, states = res
        return (tuple(bwd_call(seqs, states, dy)),)

    op.defvjp(op_f, op_b)
    return op(seqs)


def _lru_scan_call(a, u, h, reverse, name):
    T, C = a.shape
    cw = _pick(C, (1024, 512, 256, 128))
    tt = _pick(T, (512, 256, 128, 64, 32, 16, 8))
    nt, ng = T // tt, tt // 8
    sub = lambda: lax.broadcasted_iota(jnp.int32, (8, cw), 0)
    first = lambda: pl.program_id(1) == 0

    def fwd_body(a_ref, u_ref, h_ref, carry_ref):
        @pl.when(first())
        def _():
            carry_ref[...] = jnp.zeros_like(carry_ref)

        def group(i, carry):
            r0 = pl.multiple_of(i * 8, 8)
            ab, ub = a_ref[pl.ds(r0, 8), :], u_ref[pl.ds(r0, 8), :]
            out = jnp.zeros((8, cw), F32)
            for j in range(8):
                carry = ab[j:j + 1, :] * carry + ub[j:j + 1, :]
                out = jnp.where(sub() == j, carry, out)
            h_ref[pl.ds(r0, 8), :] = out
            return carry

        carry_ref[...] = lax.fori_loop(0, ng, group, carry_ref[...])

    def bwd_body(a_ref, u_ref, h_ref, hp_ref, g_ref, da_ref, cg_ref, ca_ref):
        @pl.when(first())
        def _():
            cg_ref[...] = jnp.zeros_like(cg_ref)
            ca_ref[...] = jnp.zeros_like(ca_ref)

        h_before = jnp.where(pl.program_id(1) < nt - 1, hp_ref[7:8, :], 0.0)

        def group(i, carry):
            g_next, a_next = carry
            gi = ng - 1 - i
            r0 = pl.multiple_of(gi * 8, 8)
            rp = pl.multiple_of(jnp.maximum(gi - 1, 0) * 8, 8)
            ab, ub, hb = a_ref[pl.ds(r0, 8), :], u_ref[pl.ds(r0, 8), :], h_ref[pl.ds(r0, 8), :]
            h_last_prev = jnp.where(gi > 0, h_ref[pl.ds(rp, 8), :][7:8, :], h_before)
            g_out = jnp.zeros((8, cw), F32)
            da_out = jnp.zeros((8, cw), F32)
            for j in range(7, -1, -1):
                g_next = ub[j:j + 1, :] + a_next * g_next
                a_next = ab[j:j + 1, :]
                h_prev = hb[j - 1:j, :] if j > 0 else h_last_prev
                g_out = jnp.where(sub() == j, g_next, g_out)
                da_out = jnp.where(sub() == j, g_next * h_prev, da_out)
            g_ref[pl.ds(r0, 8), :] = g_out
            da_ref[pl.ds(r0, 8), :] = da_out
            return g_next, a_next

        cg_ref[...], ca_ref[...] = lax.fori_loop(0, ng, group, (cg_ref[...], ca_ref[...]))

    row = pltpu.VMEM((1, cw), F32)
    if not reverse:
        spec = pl.BlockSpec((tt, cw), lambda i, t: (t, i))
        return pl.pallas_call(fwd_body, grid=(C // cw, nt), in_specs=[spec, spec], out_specs=spec,
                              out_shape=SDS((T, C), F32), scratch_shapes=[row], name=name,
                              compiler_params=_cparams(("parallel", "arbitrary")))(a, u)
    spec = pl.BlockSpec((tt, cw), lambda i, t: (nt - 1 - t, i))
    before = pl.BlockSpec((8, cw), lambda i, t: (jnp.maximum((nt - 1 - t) * ng - 1, 0), i))
    return pl.pallas_call(bwd_body, grid=(C // cw, nt), in_specs=[spec, spec, spec, before], out_specs=[spec, spec],
                          out_shape=[SDS((T, C), F32), SDS((T, C), F32)], scratch_shapes=[row, row], name=name,
                          compiler_params=_cparams(("parallel", "arbitrary")))(a, u, h, h)


@jax.custom_vjp
def _lru_scan(a, u):
    return _lru_scan_call(a, u, None, False, "lru_scan_f")


def _lru_scan_f(a, u):
    h = _lru_scan(a, u)
    return h, (a, h)


def _lru_scan_b(res, dh):
    a, h = res
    g, da = _lru_scan_call(a, dh, h, True, "lru_scan_b")
    return da, g


_lru_scan.defvjp(_lru_scan_f, _lru_scan_b)


def _heads_major(x, nheads):
    T, W = x.shape
    return jnp.transpose(x.reshape(T, nheads, W // nheads), (1, 0, 2))


def _tokens_major(x):
    H, T, N = x.shape
    return jnp.transpose(x, (1, 0, 2)).reshape(T, H * N)


def _pad_cols(w, offs, widths, total):
    parts, pos, src = [], 0, 0
    for o, wd in zip(offs, widths):
        if o > pos:
            parts.append(jnp.zeros((w.shape[0], o - pos), w.dtype))
        parts.append(w[:, src:src + wd])
        src += wd
        pos = o + wd
    if pos < total:
        parts.append(jnp.zeros((w.shape[0], total - pos), w.dtype))
    return jnp.concatenate(parts, axis=1)


def _unpad_cols(w, offs, widths):
    return jnp.concatenate([w[:, o:o + wd] for o, wd in zip(offs, widths)], axis=1)


def _aligned_layout(widths):
    offs, pos = [], 0
    for w in widths:
        offs.append(pos)
        pos += _rup(w, LANES)
    return offs, _rup(pos, 512)


def _pad_lanes(v, n):
    return jnp.pad(v, ((0, 0), (0, n - v.shape[1])))


def _even_mixer(x, q, wz, li):
    T = x.shape[0]
    ah, an = q['rwkv_r_k'].shape
    aw = ah * an
    bh, bn = q['gdn_A_log'].shape[0], q['gdn_norm_g'].shape[0]
    bw = bh * bn
    lw_, la_, lg_ = q['rwkv_w2'].shape[0], q['rwkv_a2'].shape[0], q['rwkv_g2'].shape[0]
    widths = [aw, aw, aw, lw_, la_, lg_, bw, bw, bw, bw, bh, bh]
    offs, total = _aligned_layout(widths)
    pw = [_rup(w, LANES) for w in widths]
    hcols = _matmul(x, _pad_cols(q['even_w_in'], offs, widths, total), wz['even_w_in'], f"even_in{li}")
    a_w = offs[6]
    a_cols, bq, bk, bv, bz, beta_raw, alpha_raw = _split_cols(hcols, [0] + offs[6:], [a_w] + pw[6:])

    mu = _pad_cols(q['rwkv_mu'][None], offs[:6], widths[:6], a_w)
    (xs,) = _colwise(f"rwkv_shift{li}", lambda h, m: (h + (_shift(h, 1) - h) * m,), [a_cols], [mu])
    r, k, v, w_lo, a_lo, g_lo = _split_cols(xs, offs[:6], pw[:6])
    tw, sg = _rowwise(f"rwkv_lora_act{li}", lambda w, g: (jnp.tanh(w), jax.nn.sigmoid(g)), [w_lo, g_lo])
    pad_rows = lambda w, n: jnp.pad(w, ((0, n - w.shape[0]), (0, 0)))
    wl = _matmul(tw, pad_rows(q['rwkv_w2'], pw[3]), wz['rwkv_w2'], f"rwkv_w2{li}")
    al = _matmul(a_lo, pad_rows(q['rwkv_a2'], pw[4]), wz['rwkv_a2'], f"rwkv_a2{li}")
    g = _matmul(sg, pad_rows(q['rwkv_g2'], pw[5]), wz['rwkv_g2'], f"rwkv_g2{li}")
    ones_a = _group_ones(aw, an)

    def pre(k, wl, al, w0, a0, k_k, k_a, ones):
        lw = -jnp.exp(-_softplus(-(w0 + wl)) - 0.5)
        a = jax.nn.sigmoid(a0 + al)
        kk = k * k_k
        kk = kk * lax.rsqrt(jnp.dot(kk * kk, ones) + L2_EPS)
        return lw, k * (1.0 + (a - 1.0) * k_a), -kk, kk * a

    lw, k2, sa, sb = _rowwise(f"rwkv_pre{li}", pre, [k, wl, al],
                              [q['rwkv_w0'][None], q['rwkv_a0'][None], q['rwkv_k_k'][None], q['rwkv_k_a'][None]], [ones_a])
    hm = lambda t: _heads_major(t, ah)
    out = _chunk_scan(f"rwkv_scan{li}", _rwkv_chunk, [hm(r), hm(lw), hm(k2), hm(v), hm(sa), hm(sb)],
                      min(ah, SCAN_HEADS), min(RWKV_CHUNK, T), (an, an), an)
    out = _tokens_major(out)

    def post(out, r, k2, v, g, gn_g, gn_b, r_k, ones):
        mean = jnp.dot(out, ones) * (1.0 / an)
        cen = out - mean
        var = jnp.dot(cen * cen, ones) * (1.0 / an)
        normed = cen * lax.rsqrt(var + A_GN_EPS) * gn_g + gn_b
        bonus = jnp.dot(r * k2 * r_k, ones) * v
        return ((normed + bonus) * g,)

    flat = lambda t: t.reshape(1, -1)
    (ya,) = _rowwise(f"rwkv_post{li}", post, [out, r, k2, v, g],
                     [flat(q['rwkv_gn_g']), flat(q['rwkv_gn_b']), flat(q['rwkv_r_k'])], [ones_a])

    cw = q['gdn_conv_w']

    def conv_l2(x, w):
        y = _silu(_causal_conv(x, w))
        return (y * lax.rsqrt(jnp.sum(y * y, axis=1, keepdims=True) + L2_EPS),)

    if bn == LANES:
        (gq,) = _colwise(f"gdn_conv_q{li}", conv_l2, [bq], [cw[:, :bw]])
        (gk,) = _colwise(f"gdn_conv_k{li}", conv_l2, [bk], [cw[:, bw:2 * bw]])
    else:
        raise NotImplementedError("gated DeltaNet head width must equal the lane count")
    (gv,) = _colwise(f"gdn_conv_v{li}", lambda x, w: (_silu(_causal_conv(x, w)),), [bv], [cw[:, 2 * bw:]])

    def gates(beta_raw, alpha_raw, a_log, dt_bias):
        return jax.nn.sigmoid(beta_raw), -jnp.exp(a_log) * _softplus(alpha_raw + dt_bias)

    beta, lg = _rowwise(f"gdn_gates{li}", gates, [beta_raw, alpha_raw],
                        [_pad_lanes(q['gdn_A_log'][None], pw[10]), _pad_lanes(q['gdn_dt_bias'][None], pw[11])])
    gl = min(GDN_CHUNK, T)
    col = lambda t: jnp.transpose(t[:, :bh]).reshape(bh, T // gl, 1, gl)
    hmb = lambda t: _heads_major(t, bh)
    o = _chunk_scan(f"gdn_scan{li}", _gdn_chunk, [hmb(gq), hmb(gk), hmb(gv), col(beta), col(lg)],
                    min(bh, SCAN_HEADS), gl, (bn, bn), bn)
    o = _tokens_major(o)
    ones_b = _group_ones(bw, bn)

    def gdn_post(o, z, ng, ones):
        ms = jnp.dot(o * o, ones) * (1.0 / bn)
        return (o * lax.rsqrt(ms + RMS_EPS) * ng * _silu(z),)

    (yb,) = _rowwise(f"gdn_post{li}", gdn_post, [o, bz], [jnp.tile(q['gdn_norm_g'][None], (1, bh))], [ones_b])
    return _matmul(jnp.concatenate([ya, yb], axis=1), q['even_w_out'], wz['even_w_out'], f"even_out{li}")


def _odd_mixer(x, q, wz, li):
    T = x.shape[0]
    ch = q['mamba_dt_bias'].shape[0]
    cwid = q['mamba_norm_g'].shape[0]
    cp = cwid // ch
    xbc_w = q['mamba_conv_w'].shape[1]
    cn = (xbc_w - cwid) // (2 * C_GROUPS)
    dw = q['lru_lambda'].shape[0]
    widths = [cwid, xbc_w, ch, dw, dw]
    offs, total = _aligned_layout(widths)
    pw = [_rup(w, LANES) for w in widths]
    hcols = _matmul(x, _pad_cols(q['odd_w_in'], offs, widths, total), wz['odd_w_in'], f"odd_in{li}")
    z, xbc, dt_raw, y_br, x_br = _split_cols(hcols, offs, pw)

    (xbc_c,) = _colwise(f"mamba_conv{li}", lambda x, w, b: (_silu(_causal_conv(x, w) + b),), [xbc],
                        [q['mamba_conv_w'], q['mamba_conv_b'][None]])
    gn = C_GROUPS * cn
    xs, bm, cm = _split_cols(xbc_c, [0, cwid, cwid + gn], [cwid, gn, gn])

    def dts(dt_raw, dt_bias, a_log):
        dt = _softplus(dt_raw + dt_bias)
        return dt, dt * (-jnp.exp(a_log))

    dt, aa = _rowwise(f"mamba_dt{li}", dts, [dt_raw],
                      [_pad_lanes(q['mamba_dt_bias'][None], pw[2]), _pad_lanes(q['mamba_A_log'][None], pw[2])])
    sl = min(SSD_CHUNK, T)
    col = lambda t: jnp.transpose(t[:, :ch]).reshape(ch, T // sl, 1, sl)
    y = _chunk_scan(f"ssd_scan{li}", _ssd_chunk,
                    [_heads_major(xs, ch), col(dt), col(aa), _heads_major(bm, C_GROUPS), _heads_major(cm, C_GROUPS)],
                    max(ch // C_GROUPS, min(ch, SCAN_HEADS)), sl, (cn, cp), cp)
    y = _tokens_major(y)
    gsz = cwid // C_GROUPS

    def mamba_post(y, xs, z, d, ng):
        yy = (y + xs * d) * _silu(z)
        lane = lax.broadcasted_iota(jnp.int32, yy.shape, 1)
        ms = jnp.zeros_like(yy)
        for gi in range(C_GROUPS):
            sel = (lane >= gi * gsz) & (lane < (gi + 1) * gsz)
            ms = jnp.where(sel, jnp.sum(jnp.where(sel, yy * yy, 0.0), axis=1, keepdims=True) * (1.0 / gsz), ms)
        return (yy * lax.rsqrt(ms + RMS_EPS) * ng,)

    (yc,) = _rowwise(f"mamba_post{li}", mamba_post, [y, xs, z],
                     [jnp.repeat(q['mamba_D'], cp)[None], q['mamba_norm_g'][None]])

    (xc,) = _colwise(f"lru_conv{li}", lambda x, w, b: (_causal_conv(x, w) + b,), [x_br],
                     [q['lru_conv_w'], q['lru_conv_b'][None]])
    ra, ia = _block_diag_pair(xc, q['lru_wa'], q['lru_wx'], f"lru_gates{li}")

    def lru_pre(ra, ia, xc, ba, bx, lam):
        r = jax.nn.sigmoid(ra + ba)
        i = jax.nn.sigmoid(ia + bx)
        log_a = LRU_C * r * (-_softplus(-lam))
        t = 2.0 * log_a
        series = t * (1.0 + t * (0.5 + t * (1.0 / 6.0 + t * (1.0 / 24.0 + t * (1.0 / 120.0 + t * (1.0 / 720.0))))))
        expm1 = jnp.where(t > -0.2, series, jnp.exp(t) - 1.0)
        return jnp.exp(log_a), jnp.sqrt(-expm1) * (i * xc)

    a, u = _rowwise(f"lru_pre{li}", lru_pre, [ra, ia, xc], [q['lru_ba'][None], q['lru_bx'][None], q['lru_lambda'][None]])
    h = _lru_scan(a, u)
    (yd,) = _rowwise(f"lru_post{li}", lambda h, y: (h * jax.nn.gelu(y),), [h, y_br])
    return _matmul(jnp.concatenate([yc, yd], axis=1), q['odd_w_out'], wz['odd_w_out'], f"odd_out{li}")


def _forward(x, wz, sp, p, w16, depth):
    alpha = (2.0 * depth) ** 0.25
    for i in range(depth):
        j = i // 2
        even = i % 2 == 0
        names = [n for n in WEIGHTS if n.startswith(('rwkv_', 'gdn_', 'even_') if even else ('mamba_', 'lru_', 'odd_'))]
        q = {n: (w16[n][j] if n in MATRICES else sp[n][j]) for n in names}
        wzl = {n: wz[f"{n}.{j}"] for n in names if n in MATRICES}
        y = (_even_mixer if even else _odd_mixer)(x, q, wzl, i)

        def ln_res(x, y, g, b):
            return (_layer_norm_rows(alpha * x + y, g, b, LN_EPS),)

        (h,) = _rowwise(f"ln1_{i}", ln_res, [x, y], [sp['ln1_g'][i][None], sp['ln1_b'][i][None]])
        dff = w16['ffn_up'].shape[2] // 2
        gate = _matmul(h, w16['ffn_up'][i][:, :dff], wz[f"ffn_up.{i}"], f"ffn_gate{i}")
        val = _matmul(h, w16['ffn_up'][i][:, dff:], wz[f"ffn_up_val.{i}"], f"ffn_val{i}")
        cw, cb = sp['ffn_conv_w'][i], sp['ffn_conv_b'][i][None]

        def ffn_act(gate, val, wg, wv, bg, bv):
            return (_silu(_causal_conv(gate, wg) + bg) * (_causal_conv(val, wv) + bv),)

        (act,) = _colwise(f"ffn_act{i}", ffn_act, [gate, val], [cw[:, :dff], cw[:, dff:], cb[:, :dff], cb[:, dff:]])
        f = _matmul(act, w16['ffn_down'][i], wz[f"ffn_down.{i}"], f"ffn_down{i}")
        (h2,) = _rowwise(f"ln2_{i}", ln_res, [h, f], [sp['ln2_g'][i][None], sp['ln2_b'][i][None]])
        e0 = _matmul(p[i], w16['ple_proj'][i], wz[f"ple_proj.{i}"], f"ple_proj{i}")
        gl = _matmul(h2, w16['ple_gate_w'][i], wz[f"ple_gate_w.{i}"], f"ple_gate{i}")

        def ple(h2, gl, e0, gb, ng):
            e = e0 * lax.rsqrt(jnp.mean(e0 * e0, axis=1, keepdims=True) + RMS_EPS) * ng
            return (h2 + jax.nn.sigmoid(gl + gb) * e,)

        (x,) = _rowwise(f"ple{i}", ple, [h2, gl, e0], [sp['ple_gate_b'][i][None], sp['ple_norm_g'][i][None]])
    return x


def _loss_head(y, target):
    T, D = y.shape
    tile = _pick(T, (256, 128, 64, 32, 16, 8))

    def body(y_ref, t_ref, dy_ref, l_ref):
        err = y_ref[...] - t_ref[...]
        dy_ref[...] = err * (1.0 / D)

        @pl.when(pl.program_id(0) == 0)
        def _():
            l_ref[...] = jnp.zeros_like(l_ref)

        l_ref[...] += jnp.sum(jnp.sum(err * err, axis=1, keepdims=True), axis=0, keepdims=True) * (0.5 / D) + jnp.zeros_like(l_ref)

    spec = pl.BlockSpec((tile, D), lambda i: (i, 0))
    dy, l = pl.pallas_call(body, grid=(T // tile,), in_specs=[spec, spec],
                           out_specs=[spec, pl.BlockSpec((8, LANES), lambda i: (0, 0))],
                           out_shape=[SDS((T, D), F32), SDS((8, LANES), F32)], name="loss_head",
                           compiler_params=_cparams(("arbitrary",)))(y, target)
    return l[0, 0], dy


def _my_index():
    return 4 * lax.axis_index("x") + 2 * lax.axis_index("y") + lax.axis_index("c")


def _hbm_specs(n):
    return [pl.BlockSpec(memory_space=pl.ANY)] * n


def _all_gather(blocks, name):
    blocks = tuple(blocks)
    n = len(blocks)
    half = [b.shape[0] // 2 for b in blocks]

    def body(*refs):
        ins, outs = refs[:n], refs[n:2 * n]
        send_sems, recv_sems, local_sems = refs[2 * n:]
        x, y, c = lax.axis_index("x"), lax.axis_index("y"), lax.axis_index("c")
        me, sibling, other = (x, y, c), (x, y, 1 - c), 1 - c
        xn, yn, dg = (1 - x, y), (x, 1 - y), (1 - x, 1 - y)

        def slot(i, px, py, pc, h=None):
            ref = outs[i].at[4 * px + 2 * py + pc]
            return ref if h is None else ref.at[pl.ds(h * half[i], half[i])]

        def copy(i, k, blk, to, h=None, src=None):
            dst = slot(i, *blk, h)
            return pltpu.make_async_remote_copy(
                src_ref=dst if src is None else src, dst_ref=dst, send_sem=send_sems.at[9 * i + k],
                recv_sem=recv_sems.at[9 * i + k], device_id=to, device_id_type=MESH)

        mine = [pltpu.make_async_copy(ins[i], slot(i, *me), local_sems.at[i]) for i in range(n)]
        sent = []
        for i in range(n):
            sent += [copy(i, 1, me, (*xn, c), src=ins[i]), copy(i, 2, me, (*yn, c), src=ins[i])]
        sent += [copy(i, 0, me, sibling, src=ins[i]) for i in range(n)]
        for cp in mine + sent:
            cp.start()

        def after(i, k_in, blk, h_in, forwards):
            copy(i, k_in, blk, me, h_in).wait_recv()
            for k_out, to, h_out in forwards:
                sent.append(copy(i, k_out, blk, to, h_out))
                sent[-1].start()

        for i in range(n):
            after(i, 1, (*xn, c), None, [(3, (*yn, c), 0), (5, sibling, None)])
        for i in range(n):
            after(i, 2, (*yn, c), None, [(4, (*xn, c), 1), (6, sibling, None)])
        for i in range(n):
            after(i, 3, (*dg, c), 0, [(7, sibling, 0)])
        for i in range(n):
            after(i, 4, (*dg, c), 1, [(8, sibling, 1)])
        for i in range(n):
            copy(i, 0, sibling, me).wait_recv()
            copy(i, 5, (*xn, other), me).wait_recv()
            copy(i, 6, (*yn, other), me).wait_recv()
            copy(i, 7, (*dg, other), me, 0).wait_recv()
            copy(i, 8, (*dg, other), me, 1).wait_recv()
        for cp in sent:
            cp.wait_send()
        for cp in mine:
            cp.wait()

    return pl.pallas_call(
        body, out_shape=[SDS((N_DEV,) + b.shape, b.dtype) for b in blocks], in_specs=_hbm_specs(n), out_specs=_hbm_specs(n),
        scratch_shapes=[pltpu.SemaphoreType.DMA((9 * n,)), pltpu.SemaphoreType.DMA((9 * n,)), pltpu.SemaphoreType.DMA((n,))],
        name=name)(*blocks)


def _sibling_exchange(parts, name):
    parts = tuple(parts)
    n = len(parts)

    def body(*refs):
        ins, outs = refs[:n], refs[n:2 * n]
        send_sems, recv_sems = refs[2 * n:]
        x, y, c = lax.axis_index("x"), lax.axis_index("y"), lax.axis_index("c")
        copies = [pltpu.make_async_remote_copy(
            src_ref=ins[i].at[q, 1 - c], dst_ref=outs[i].at[q], send_sem=send_sems.at[4 * i + q],
            recv_sem=recv_sems.at[4 * i + q], device_id=(x, y, 1 - c), device_id_type=MESH)
            for i in range(n) for q in range(4)]
        for cp in copies:
            cp.start()
        for cp in copies:
            cp.wait_recv()
        for cp in copies:
            cp.wait_send()

    return pl.pallas_call(
        body, out_shape=[SDS((4,) + p.shape[2:], p.dtype) for p in parts], in_specs=_hbm_specs(n), out_specs=_hbm_specs(n),
        scratch_shapes=[pltpu.SemaphoreType.DMA((4 * n,)), pltpu.SemaphoreType.DMA((4 * n,))], name=name)(*parts)


def _neighbour_exchange(bufs, k, name):
    bufs = tuple(bufs)
    n = len(bufs)

    def body(*refs):
        ins, outs = refs[:n], refs[n:2 * n]
        send_sems, recv_sems = refs[2 * n:]
        x, y, c = lax.axis_index("x"), lax.axis_index("y"), lax.axis_index("c")
        copies = [pltpu.make_async_remote_copy(
            src_ref=ins[i].at[pl.ds(d * k, k)], dst_ref=outs[i].at[pl.ds(d * k, k)], send_sem=send_sems.at[2 * i + d],
            recv_sem=recv_sems.at[2 * i + d], device_id=to, device_id_type=MESH)
            for d, to in enumerate([(1 - x, y, c), (x, 1 - y, c)]) for i in range(n)]
        for cp in copies:
            cp.start()
        for cp in copies:
            cp.wait_recv()
        for cp in copies:
            cp.wait_send()

    return pl.pallas_call(
        body, out_shape=[SDS((2 * k,) + b.shape[1:], b.dtype) for b in bufs], in_specs=_hbm_specs(n), out_specs=_hbm_specs(n),
        scratch_shapes=[pltpu.SemaphoreType.DMA((2 * n,)), pltpu.SemaphoreType.DMA((2 * n,))], name=name)(*bufs)


_FIRST_HOP_SLOT = ((0, 5), (4, 2), (1, 3))


def _place_ids():
    x, y, c = lax.axis_index("x"), lax.axis_index("y"), lax.axis_index("c")
    place = [c, 2 * x + y, 2 * (1 - x) + y, 2 * x + (1 - y), 2 * (1 - x) + (1 - y)]
    return jnp.stack(place + [jnp.int32(s) for pair in _FIRST_HOP_SLOT for s in pair]).astype(jnp.int32)


def _row_tile(rows, cols):
    best = None
    for t in range(16, rows + 1, 16):
        if rows % t == 0 and t * cols <= 256 * 1024:
            best = t
    return best or rows


def _chip_partials(part, recv_a, ids, dtype, name):
    _, _, R, C = part.shape
    tr = _row_tile(R // 2, C)
    nt = R // 2 // tr

    def body(ids_ref, p_ref, a_ref, o_ref):
        o_ref[...] = (p_ref[...] + a_ref[...]).astype(dtype)

    return pl.pallas_call(
        body, out_shape=SDS((6, R // 2, C), dtype),
        grid_spec=pltpu.PrefetchScalarGridSpec(
            num_scalar_prefetch=1, grid=(3, 2, nt),
            in_specs=[pl.BlockSpec((None, None, tr, C), lambda s, h, i, ids: (ids[2 + s], ids[0], h * nt + i, 0)),
                      pl.BlockSpec((None, tr, C), lambda s, h, i, ids: (ids[2 + s], h * nt + i, 0))],
            out_specs=pl.BlockSpec((None, tr, C), lambda s, h, i, ids: (ids[5 + 2 * s + h], i, 0))),
        name=name, compiler_params=_cparams(("parallel", "parallel", "parallel")))(ids, part, recv_a)


def _second_hop(first, recv1, name):
    _, R2, C = first.shape
    tr = _row_tile(R2, C)

    def body(f_ref, r_ref, o_ref):
        o_ref[...] = (f_ref[...].astype(F32) + r_ref[...].astype(F32)).astype(o_ref.dtype)

    return pl.pallas_call(
        body, out_shape=SDS((2, R2, C), first.dtype), grid=(2, R2 // tr),
        in_specs=[pl.BlockSpec((None, tr, C), lambda d, i: (5 - d, i, 0)),
                  pl.BlockSpec((None, tr, C), lambda d, i: (3 - 2 * d, i, 0))],
        out_specs=pl.BlockSpec((None, tr, C), lambda d, i: (d, i, 0)),
        name=name, compiler_params=_cparams(("parallel", "parallel")))(first, recv1)


def _adamw(part, recv_a, recv1, recv2, ids, w, m, v, name):
    R, C = w.shape
    tr = _row_tile(R // 2, C)
    nt = R // 2 // tr
    c1 = 1.0 / (1.0 - ADAM_B1 ** ADAM_STEP)
    c2 = 1.0 / (1.0 - ADAM_B2 ** ADAM_STEP)

    def body(ids_ref, p_ref, a_ref, x1_ref, y1_ref, x2_ref, y2_ref, w_ref, m_ref, v_ref, g_ref, d_ref, nm_ref, nv_ref):
        first = pl.program_id(0) == 0
        from_x = jnp.where(first, x1_ref[...], x2_ref[...]).astype(F32)
        from_y = jnp.where(first, y2_ref[...], y1_ref[...]).astype(F32)
        g = p_ref[...] + a_ref[...] + from_x + from_y
        nm = ADAM_B1 * m_ref[...] + (1.0 - ADAM_B1) * g
        nv = ADAM_B2 * v_ref[...] + (1.0 - ADAM_B2) * jnp.square(g)
        g_ref[...] = g
        nm_ref[...] = nm
        nv_ref[...] = nv
        d_ref[...] = -ADAM_LR * ((nm * c1) / (jnp.sqrt(nv * c2) + ADAM_EPS) + ADAM_WD * w_ref[...])

    spec = pl.BlockSpec((tr, C), lambda h, i, ids: (h * nt + i, 0))
    half = lambda slot: pl.BlockSpec((None, tr, C), lambda h, i, ids: (slot, i, 0))
    return pl.pallas_call(
        body, out_shape=[SDS((R, C), F32)] * 4,
        grid_spec=pltpu.PrefetchScalarGridSpec(
            num_scalar_prefetch=1, grid=(2, nt),
            in_specs=[pl.BlockSpec((None, None, tr, C), lambda h, i, ids: (ids[1], ids[0], h * nt + i, 0)),
                      pl.BlockSpec((None, tr, C), lambda h, i, ids: (ids[1], h * nt + i, 0)),
                      half(0), half(2), half(0), half(1), spec, spec, spec],
            out_specs=[spec] * 4),
        name=name, compiler_params=_cparams(("parallel", "parallel")))(ids, part, recv_a, recv1, recv1, recv2, recv2, w, m, v)


def _to_flat(vecs, quantum):
    flat = jnp.concatenate([v.reshape(-1) for v in vecs])
    n = _rup(flat.shape[0], quantum * FLAT_COLS)
    return jnp.pad(flat, (0, n - flat.shape[0])).reshape(n // FLAT_COLS, FLAT_COLS)


def _gathered_to_full(g, names, blocks):
    flat = g.reshape(N_DEV, -1)
    out, pos = {}, 0
    for n in names:
        shp = blocks[n]
        size = math.prod(shp)
        out[n] = _blocks_to_full(flat[:, pos:pos + size].reshape((N_DEV,) + shp), SHARD_AXIS[n])
        pos += size
    return out


def _blocks_to_full(g, ax):
    shp = g.shape[1:]
    return jnp.moveaxis(g, 0, ax).reshape(shp[:ax] + (N_DEV * shp[ax],) + shp[ax + 1:])


def _full_to_blocks(g, ax, ndev=N_DEV):
    shp = g.shape
    t = g.reshape(shp[:ax] + (ndev, shp[ax] // ndev) + shp[ax + 1:])
    return jnp.moveaxis(t, ax, 0)


def kernel(x, p, ln1_g, ln1_b, ln2_g, ln2_b, ffn_up, ffn_conv_w, ffn_conv_b, ffn_down, ple_proj, ple_norm_g, ple_gate_w, ple_gate_b, even_w_in, even_w_out, rwkv_mu, rwkv_w0, rwkv_w2, rwkv_a0, rwkv_a2, rwkv_g2, rwkv_k_k, rwkv_k_a, rwkv_r_k, rwkv_gn_g, rwkv_gn_b, gdn_conv_w, gdn_A_log, gdn_dt_bias, gdn_norm_g, odd_w_in, odd_w_out, mamba_conv_w, mamba_conv_b, mamba_dt_bias, mamba_A_log, mamba_D, mamba_norm_g, lru_conv_w, lru_conv_b, lru_wa, lru_ba, lru_wx, lru_bx, lru_lambda, loss_target, m_ln1_g, m_ln1_b, m_ln2_g, m_ln2_b, m_ffn_up, m_ffn_conv_w, m_ffn_conv_b, m_ffn_down, m_ple_proj, m_ple_norm_g, m_ple_gate_w, m_ple_gate_b, m_even_w_in, m_even_w_out, m_rwkv_mu, m_rwkv_w0, m_rwkv_w2, m_rwkv_a0, m_rwkv_a2, m_rwkv_g2, m_rwkv_k_k, m_rwkv_k_a, m_rwkv_r_k, m_rwkv_gn_g, m_rwkv_gn_b, m_gdn_conv_w, m_gdn_A_log, m_gdn_dt_bias, m_gdn_norm_g, m_odd_w_in, m_odd_w_out, m_mamba_conv_w, m_mamba_conv_b, m_mamba_dt_bias, m_mamba_A_log, m_mamba_D, m_mamba_norm_g, m_lru_conv_w, m_lru_conv_b, m_lru_wa, m_lru_ba, m_lru_wx, m_lru_bx, m_lru_lambda, v_ln1_g, v_ln1_b, v_ln2_g, v_ln2_b, v_ffn_up, v_ffn_conv_w, v_ffn_conv_b, v_ffn_down, v_ple_proj, v_ple_norm_g, v_ple_gate_w, v_ple_gate_b, v_even_w_in, v_even_w_out, v_rwkv_mu, v_rwkv_w0, v_rwkv_w2, v_rwkv_a0, v_rwkv_a2, v_rwkv_g2, v_rwkv_k_k, v_rwkv_k_a, v_rwkv_r_k, v_rwkv_gn_g, v_rwkv_gn_b, v_gdn_conv_w, v_gdn_A_log, v_gdn_dt_bias, v_gdn_norm_g, v_odd_w_in, v_odd_w_out, v_mamba_conv_w, v_mamba_conv_b, v_mamba_dt_bias, v_mamba_A_log, v_mamba_D, v_mamba_norm_g, v_lru_conv_w, v_lru_conv_b, v_lru_wa, v_lru_ba, v_lru_wx, v_lru_bx, v_lru_lambda):
    args = locals()
    w = {n: args[n] for n in WEIGHTS}
    m = {n: args["m_" + n] for n in WEIGHTS}
    v = {n: args["v_" + n] for n in WEIGHTS}
    depth = ln1_g.shape[0]
    me = _my_index()
    blocks = {n: w[n].shape for n in WEIGHTS}

    as_rows = lambda t: t.reshape(-1, t.shape[-1])
    small = _to_flat([w[n] for n in SMALL_SHARDED], 16)
    gathered = _all_gather([as_rows(w[n].astype(BF16)) for n in MATRICES] + [small], "gather_params")
    w16 = {n: _blocks_to_full(g.reshape((N_DEV,) + blocks[n]), SHARD_AXIS[n]) for n, g in zip(MATRICES, gathered)}
    sp = _gathered_to_full(gathered[-1], SMALL_SHARDED, blocks)
    sp.update({n: w[n] for n in REPLICATED})

    lay = _matrix_layouts(w16, sp)
    wz = {k: jnp.zeros(shape, F32) for k, shape in lay['padded'].items()}
    y, vjp = jax.vjp(lambda x_, wz_, sp_: _forward(x_, wz_, sp_, p[:, 0], w16, depth), x[0], wz, sp)
    loss_local, dy = _loss_head(y, loss_target[0])
    dx, dwz, dsp = vjp(dy)
    loss = lax.psum(loss_local, ("x", "y", "c"))
    gfull = dict(dsp)
    gblocks = {}
    for n in MATRICES:
        layers = range(w16[n].shape[0])
        ax = SHARD_AXIS[n]
        if n == 'ffn_up':
            gblocks[n] = jnp.concatenate(
                [jnp.concatenate([_full_to_blocks(dwz[f"{k}.{j}"][None], ax, N_DEV // 2) for j in layers], axis=1)
                 for k in ("ffn_up", "ffn_up_val")], axis=0)
        else:
            gblocks[n] = jnp.concatenate(
                [_full_to_blocks(lay['unpad'][n](dwz[f"{n}.{j}"])[None], ax) for j in layers], axis=1)

    rep_flat = jnp.concatenate([gfull[n].reshape(-1) for n in REPLICATED])
    rep_n = rep_flat.shape[0]
    piece = _rup(rep_n, N_DEV * LANES) // N_DEV
    rep_pad = lambda t: jnp.pad(t, (0, N_DEV * piece - rep_n))
    small_parts = jnp.concatenate([_full_to_blocks(gfull[n], SHARD_AXIS[n]).reshape(N_DEV, -1) for n in SMALL_SHARDED]
                                  + [rep_pad(rep_flat).reshape(N_DEV, piece)], axis=1)
    n_flat = small_parts.shape[1]
    n_pad = _rup(n_flat, 16 * FLAT_COLS)
    small_parts = jnp.pad(small_parts, ((0, 0), (0, n_pad - n_flat)))

    def my_small(d):
        rep = rep_pad(jnp.concatenate([d[n].reshape(-1) for n in REPLICATED]))
        mine = lax.dynamic_slice(rep, (me * piece,), (piece,))
        flat = jnp.concatenate([d[n].reshape(-1) for n in SMALL_SHARDED] + [mine])
        return jnp.pad(flat, (0, n_pad - n_flat)).reshape(n_pad // FLAT_COLS, FLAT_COLS)

    by_chip = lambda t, cols: t.reshape(4, 2, -1, cols)
    parts = [by_chip(gblocks[n], blocks[n][-1]) for n in MATRICES]
    parts.append(by_chip(small_parts, FLAT_COLS))
    wire = [BF16] * len(MATRICES) + [F32]
    tags = MATRICES + ["small"]
    ids = _place_ids()
    recv_a = _sibling_exchange(parts, "reduce_sibling")
    first = [_chip_partials(pt, ra, ids, dt, f"chip_partials_{t}") for pt, ra, dt, t in zip(parts, recv_a, wire, tags)]
    recv1 = _neighbour_exchange(first, 2, "reduce_first_hop")
    recv2 = _neighbour_exchange([_second_hop(f, r1, f"second_hop_{t}") for f, r1, t in zip(first, recv1, tags)],
                                1, "reduce_second_hop")
    mine = [(as_rows(w[n]), as_rows(m[n]), as_rows(v[n])) for n in MATRICES] + [(my_small(w), my_small(m), my_small(v))]
    results = [_adamw(pt, ra, r1, r2, ids, *wmv, f"adamw_{t}")
               for pt, ra, r1, r2, wmv, t in zip(parts, recv_a, recv1, recv2, mine, tags)]
    small_res = [r.reshape(-1) for r in results[-1]]
    rep_res = jnp.stack([r[n_flat - piece:n_flat] for r in small_res])
    rep_rows = _rup(4 * piece, 8 * FLAT_COLS) // FLAT_COLS
    rep_blk = jnp.pad(rep_res.reshape(-1), (0, rep_rows * FLAT_COLS - 4 * piece)).reshape(rep_rows, FLAT_COLS)
    (rep_all,) = _all_gather([rep_blk], "gather_replicated")
    rep_all = rep_all.reshape(N_DEV, -1)[:, :4 * piece]
    rep_all = jnp.transpose(rep_all.reshape(N_DEV, 4, piece), (1, 0, 2)).reshape(4, N_DEV * piece)

    outs = [{}, {}, {}, {}]
    for k in range(4):
        for n, res in zip(MATRICES, results):
            outs[k][n] = res[k].reshape(blocks[n])
        pos = 0
        for n in SMALL_SHARDED:
            size = math.prod(blocks[n])
            outs[k][n] = small_res[k][pos:pos + size].reshape(blocks[n])
            pos += size
        pos = 0
        for n in REPLICATED:
            size = math.prod(blocks[n])
            outs[k][n] = rep_all[k, pos:pos + size].reshape(blocks[n])
            pos += size
    return (loss, dx[None], *[outs[0][n] for n in WEIGHTS], *[outs[1][n] for n in WEIGHTS],
            *[outs[2][n] for n in WEIGHTS], *[outs[3][n] for n in WEIGHTS])


def _matrix_layouts(w16, sp):
    padded, unpad = {}, {}
    ident = lambda g: g
    for n in ('ffn_down', 'ple_proj', 'ple_gate_w', 'even_w_out', 'odd_w_out'):
        for j in range(w16[n].shape[0]):
            padded[f"{n}.{j}"] = w16[n].shape[1:]
        unpad[n] = ident
    for j in range(w16['ffn_up'].shape[0]):
        half = (w16['ffn_up'].shape[1], w16['ffn_up'].shape[2] // 2)
        padded[f"ffn_up.{j}"] = padded[f"ffn_up_val.{j}"] = half
    for n in ('rwkv_w2', 'rwkv_a2', 'rwkv_g2'):
        rows, cols = w16[n].shape[1:]
        for j in range(w16[n].shape[0]):
            padded[f"{n}.{j}"] = (_rup(rows, LANES), cols)
        unpad[n] = functools.partial(lambda g, rows: g[:rows], rows=rows)
    ah, an = sp['rwkv_r_k'].shape[1:]
    bh, bn = sp['gdn_A_log'].shape[1], sp['gdn_norm_g'].shape[1]
    ew = [ah * an] * 3 + [w16['rwkv_w2'].shape[1], w16['rwkv_a2'].shape[1], w16['rwkv_g2'].shape[1]] + [bh * bn] * 4 + [bh, bh]
    cwid, ch = sp['mamba_norm_g'].shape[1], sp['mamba_dt_bias'].shape[1]
    dw = sp['lru_lambda'].shape[1]
    ow = [cwid, sp['mamba_conv_w'].shape[2], ch, dw, dw]
    for n, widths in (('even_w_in', ew), ('odd_w_in', ow)):
        offs, total = _aligned_layout(widths)
        for j in range(w16[n].shape[0]):
            padded[f"{n}.{j}"] = (w16[n].shape[1], total)
        unpad[n] = functools.partial(_unpad_cols, offs=offs, widths=widths)
    return {'padded': padded, 'unpad': unpad}
```
